```python
import math
import jax
import jax.numpy as jnp
from jax import lax
import numpy as np

D_MODEL = 1024
BATCH = 16
SEQ = 2048
DEPTH = 4
DEC_BATCH = 8
DEC_SEQ = 2048
PAST_LEN = 128

GRID_W = 64
ROPE_THETA = 10000.0
Q_BLOCK = 128
HEAD_DIM = 64
A_HEADS = 4
A_KV_HEADS = 2
A_WIDTH = A_HEADS * HEAD_DIM
B_HEADS = 4
Q_LORA = 384
KV_LORA = 128
MLA_NOPE = 64
MLA_ROPE = 32
MLA_V = 64
B_WIDTH = B_HEADS * MLA_V
C_HEADS = 8
C_HEAD_DIM = 64
SSM_GROUPS = 2
D_STATE = 64
D_CONV = 5
SSD_CHUNK = 128
C_INNER = C_HEADS * C_HEAD_DIM
CONV_DIM = C_INNER + 2 * SSM_GROUPS * D_STATE
MIX_WIDTH = A_WIDTH + B_WIDTH + C_INNER
IN_SPLITS = (A_HEADS * HEAD_DIM, A_KV_HEADS * HEAD_DIM, A_KV_HEADS * HEAD_DIM,
             Q_LORA, KV_LORA, MLA_ROPE, C_INNER, CONV_DIM, 2 * C_HEADS)
D_IN_PROJ = 2352
N_EGROUPS = 4
EXPERTS_PER_GROUP = 8
N_EXPERTS = N_EGROUPS * EXPERTS_PER_GROUP
TOP_K = 2
D_FF_EXPERT = 256
MOE_BLOCK = 128
DN_ALPHA = (2 * DEPTH) ** 0.25
DN_BETA = (8 * DEPTH) ** -0.25
EPS = 1e-6

kernel_name = 'hymba_mla_ssd_hmoe_encoder'

F32 = jnp.float32


def rms_norm(x, g):
    xf = x.astype(F32)
    y = xf * lax.rsqrt(jnp.mean(xf * xf, axis=-1, keepdims=True) + EPS)
    return (y * g.astype(F32)).astype(x.dtype)


def layer_norm(x, g, b):
    xf = x.astype(F32)
    xc = xf - jnp.mean(xf, axis=-1, keepdims=True)
    var = jnp.mean(xc * xc, axis=-1, keepdims=True)
    return (xc * lax.rsqrt(var + EPS) * g.astype(F32) + b.astype(F32)).astype(x.dtype)


def axial_angles(seq_len, d_axis):
    rows = seq_len // GRID_W
    row = jnp.repeat(jnp.arange(rows, dtype=F32), GRID_W)
    col = jnp.tile(jnp.arange(GRID_W, dtype=F32), rows)
    inv = ROPE_THETA ** (-jnp.arange(0, d_axis, 2, dtype=F32) / d_axis)
    return (row[:, None] * inv, col[:, None] * inv)


def rope_axis(x, ang):
    half = x.shape[-1] // 2
    cos = jnp.cos(ang)[:, None, :].astype(x.dtype)
    sin = jnp.sin(ang)[:, None, :].astype(x.dtype)
    x1, x2 = x[..., :half], x[..., half:]
    return jnp.concatenate([x1 * cos - x2 * sin, x1 * sin + x2 * cos], axis=-1)


def rope_2d(x, angs):
    d = x.shape[-1] // 2
    return jnp.concatenate([rope_axis(x[..., :d], angs[0]), rope_axis(x[..., d:], angs[1])], axis=-1)


def block_attention(q, k, v, scale):
    bsz, seq, n_heads, dqk = q.shape
    n_kv = k.shape[2]
    grp = n_heads // n_kv
    nb = seq // Q_BLOCK
    qb = q.reshape(bsz, nb, Q_BLOCK, n_kv, grp, dqk).transpose(1, 0, 2, 3, 4, 5)

    def one_block(q_blk):
        s = jnp.einsum('bqkgd,bskd->bkgqs', q_blk, k, preferred_element_type=F32) * scale
        p = jax.nn.softmax(s, axis=-1).astype(v.dtype)
        return jnp.einsum('bkgqs,bskd->bqkgd', p, v)

    out = lax.map(one_block, qb)
    return out.transpose(1, 0, 2, 3, 4, 5).reshape(bsz, seq, n_heads * v.shape[-1])


def dwconv_centred(u, w, b):
    ch = u.shape[-1]
    y = lax.conv_general_dilated(u, w.astype(u.dtype)[:, None, :], window_strides=(1,),
                                 padding=[(D_CONV // 2, D_CONV // 2)],
                                 dimension_numbers=('NWC', 'WIO', 'NWC'),
                                 feature_group_count=ch)
    return y + b.astype(u.dtype)


def segsum(a):
    t = a.shape[-1]
    cs = jnp.cumsum(a, axis=-1)
    seg = cs[..., :, None] - cs[..., None, :]
    return jnp.where(jnp.tril(jnp.ones((t, t), dtype=bool)), seg, -jnp.inf)


def ssd_scan(xs, dt, a_coef, bm, cm):
    bsz, seq, n_heads, hd = xs.shape
    ng, n = bm.shape[2], bm.shape[3]
    r = n_heads // ng
    nc = seq // SSD_CHUNK
    xdt = (xs * dt[..., None]).reshape(bsz, nc, SSD_CHUNK, ng, r, hd)
    a = (dt * a_coef).reshape(bsz, nc, SSD_CHUNK, ng, r).transpose(0, 3, 4, 1, 2)
    bc = bm.reshape(bsz, nc, SSD_CHUNK, ng, n)
    cc = cm.reshape(bsz, nc, SSD_CHUNK, ng, n)
    a_cs = jnp.cumsum(a, axis=-1)
    lmat = jnp.exp(segsum(a))
    y_diag = jnp.einsum('bclgn,bcsgn,bgrcls,bcsgrp->bclgrp', cc, bc, lmat, xdt)
    decay_states = jnp.exp(a_cs[..., -1:] - a_cs)
    states = jnp.einsum('bclgn,bgrcl,bclgrp->bcgrpn', bc, decay_states, xdt)
    chunk_decay = jnp.exp(a_cs[..., -1])

    def step(hs, inp):
        s_c, d_c = inp
        return d_c[..., None, None] * hs + s_c, hs

    _, prev = lax.scan(step, jnp.zeros((bsz, ng, r, hd, n), F32),
                       (states.transpose(1, 0, 2, 3, 4, 5), chunk_decay.transpose(3, 0, 1, 2)))
    prev = prev.transpose(1, 0, 2, 3, 4, 5)
    y_off = jnp.einsum('bclgn,bcgrpn,bgrcl->bclgrp', cc, prev, jnp.exp(a_cs))
    return (y_diag + y_off).reshape(bsz, seq, n_heads, hd)


def ssm_group(z, xbc, dt_raw, p, l):
    bsz, seq, _ = z.shape
    xbc = jax.nn.silu(dwconv_centred(xbc, p['conv_w'][l], p['conv_b'][l])).astype(F32)
    nbc = SSM_GROUPS * D_STATE
    xs = xbc[..., :C_INNER].reshape(bsz, seq, C_HEADS, C_HEAD_DIM)
    bm = xbc[..., C_INNER:C_INNER + nbc].reshape(bsz, seq, SSM_GROUPS, D_STATE)
    cm = xbc[..., C_INNER + nbc:].reshape(bsz, seq, SSM_GROUPS, D_STATE)
    dt = jax.nn.softplus(dt_raw.astype(F32).reshape(bsz, seq, 2, C_HEADS) + p['dt_bias'][l].astype(F32))
    a_coef = -jnp.exp(p['a_log'][l].astype(F32))

    def flip(t):
        return jnp.flip(t, axis=1)

    y_fwd = ssd_scan(xs, dt[:, :, 0], a_coef[0], bm, cm)
    y_bwd = flip(ssd_scan(flip(xs), flip(dt[:, :, 1]), a_coef[1], flip(bm), flip(cm)))
    y = (y_fwd + y_bwd + p['d_skip'][l].astype(F32)[:, None] * xs).reshape(bsz, seq, C_INNER)
    y = y * jax.nn.silu(z.astype(F32))
    return rms_norm(y, p['ssm_norm_g'][l]).astype(z.dtype)


def mixer(h, angs_a, angs_b, p, l):
    bsz, seq, _ = h.shape
    proj = h @ p['w_in'][l]
    qa, ka, va, cq, ckv, kr, z, xbc, dt_raw = jnp.split(proj, np.cumsum(IN_SPLITS)[:-1].tolist(), axis=-1)
    qa = rope_2d(rms_norm(qa.reshape(bsz, seq, A_HEADS, HEAD_DIM), p['qa_norm_g'][l]), angs_a)
    ka = rope_2d(rms_norm(ka.reshape(bsz, seq, A_KV_HEADS, HEAD_DIM), p['ka_norm_g'][l]), angs_a)
    va = va.reshape(bsz, seq, A_KV_HEADS, HEAD_DIM)
    out_a = rms_norm(block_attention(qa, ka, va, HEAD_DIM ** -0.5), p['oa_norm_g'][l])
    qb = (rms_norm(cq, p['qb_norm_g'][l]) @ p['w_uq'][l]).reshape(bsz, seq, B_HEADS, MLA_NOPE + MLA_ROPE)
    qb = jnp.concatenate([qb[..., :MLA_NOPE], rope_2d(qb[..., MLA_NOPE:], angs_b)], axis=-1)
    kvb = (rms_norm(ckv, p['kvb_norm_g'][l]) @ p['w_ukv'][l]).reshape(bsz, seq, B_HEADS, MLA_NOPE + MLA_V)
    k_rope = jnp.broadcast_to(rope_2d(kr[:, :, None, :], angs_b), (bsz, seq, B_HEADS, MLA_ROPE))
    kb = jnp.concatenate([kvb[..., :MLA_NOPE], k_rope], axis=-1)
    out_b = rms_norm(block_attention(qb, kb, kvb[..., MLA_NOPE:], (MLA_NOPE + MLA_ROPE) ** -0.5),
                     p['ob_norm_g'][l])
    out_c = ssm_group(z, xbc, dt_raw, p, l)
    return jnp.concatenate([out_a, out_b, out_c], axis=-1) @ p['w_out'][l]


def hier_moe(h, p, l):
    bsz, seq, dm = h.shape
    n_tok = bsz * seq
    xf = h.reshape(n_tok, dm)
    xr = xf.astype(F32)
    g_logits = xr @ p['w_rg'][l].astype(F32) + p['b_rg'][l].astype(F32)
    e_logits = (xr @ p['w_re'][l].astype(F32) + p['b_re'][l].astype(F32)).reshape(
        n_tok, N_EGROUPS, EXPERTS_PER_GROUP)
    tok = jnp.arange(n_tok, dtype=jnp.int32)
    g_sel = jnp.argmax(g_logits, axis=-1).astype(jnp.int32)
    g_w = jax.nn.softmax(g_logits, axis=-1)[tok, g_sel]
    top_p, top_i = lax.top_k(jax.nn.softmax(e_logits[tok, g_sel], axis=-1), TOP_K)
    gate = g_w[:, None] * top_p / jnp.sum(top_p, axis=-1, keepdims=True)
    expert = (g_sel[:, None] * EXPERTS_PER_GROUP + top_i).astype(jnp.int32)
    n_as = n_tok * TOP_K
    flat_e = expert.reshape(-1)
    order = jnp.argsort(flat_e)
    se = flat_e[order]
    stok = jnp.repeat(tok, TOP_K)[order]
    sw = gate.reshape(-1)[order]
    counts = jnp.bincount(flat_e, length=N_EXPERTS).astype(jnp.int32)
    padded = (counts + MOE_BLOCK - 1) // MOE_BLOCK * MOE_BLOCK
    pend = jnp.cumsum(padded)
    dest = (pend - padded)[se] + jnp.arange(n_as, dtype=jnp.int32) - (jnp.cumsum(counts) - counts)[se]
    n_rows = -(-n_as // MOE_BLOCK) * MOE_BLOCK + N_EXPERTS * MOE_BLOCK
    n_blocks = n_rows // MOE_BLOCK
    row_tok = jnp.full((n_rows,), n_tok, dtype=jnp.int32).at[dest].set(stok)
    row_w = jnp.zeros((n_rows,), F32).at[dest].set(sw)
    block_e = jnp.minimum(jnp.searchsorted(pend, jnp.arange(n_blocks, dtype=jnp.int32) * MOE_BLOCK,
                                           side='right'), N_EXPERTS - 1)
    x_rows = jnp.concatenate([xf, jnp.zeros((1, dm), xf.dtype)], axis=0)[row_tok].reshape(
        n_blocks, MOE_BLOCK, dm)
    wg, wu, wd = p['w_e_gate'][l], p['w_e_up'][l], p['w_e_down'][l]

    def expert_block(args):
        xb, e = args
        return (jax.nn.silu(xb @ wg[e]) * (xb @ wu[e])) @ wd[e]

    y_rows = lax.map(expert_block, (x_rows, block_e)).reshape(n_rows, dm)
    y = jax.ops.segment_sum(y_rows.astype(F32) * row_w[:, None], row_tok, num_segments=n_tok + 1)[:n_tok]
    return y.astype(h.dtype).reshape(bsz, seq, dm)


def encoder_trunk(x, c, p):
    seq = x.shape[1]
    angs_a = axial_angles(seq, HEAD_DIM // 2)
    angs_b = axial_angles(seq, MLA_ROPE // 2)
    x = layer_norm(x, p['ln0_g'], p['ln0_b'])
    cs = jax.nn.silu(c)
    for l in range(DEPTH):
        mod = (cs @ p['w_ada'][l] + p['b_ada'][l])[:, None, :]
        sh1, sc1, g1, sh2, sc2, g2 = jnp.split(mod, 6, axis=-1)
        h = x * (1.0 + sc1) + sh1
        x = layer_norm(DN_ALPHA * x + (1.0 + g1) * mixer(h, angs_a, angs_b, p, l), p['ln1_g'][l], p['ln1_b'][l])
        h = x * (1.0 + sc2) + sh2
        x = layer_norm(DN_ALPHA * x + (1.0 + g2) * hier_moe(h, p, l), p['ln2_g'][l], p['ln2_b'][l])
    return x


def setup_inputs(seed: int = 0) -> dict:
    key = jax.random.key(seed)
    ks = iter(jax.random.split(key, 64))

    def nrm(shape, scale):
        return scale * jax.random.normal(next(ks), shape, F32)

    def gain(shape, s=0.02):
        return 1.0 + nrm(shape, s)

    u = jax.random.uniform(next(ks), (DEPTH, 2, C_HEADS), F32)
    dt0 = jnp.exp(u * (math.log(0.1) - math.log(0.001)) + math.log(0.001))
    return {
        'x_prompt': nrm((BATCH, SEQ, D_MODEL), 1.0),
        'x_sample': nrm((DEC_BATCH, DEC_SEQ, D_MODEL), 1.0),
        'c_prompt': nrm((BATCH, D_MODEL), 1.0),
        'c_sample': nrm((DEC_BATCH, D_MODEL), 1.0),
        'ln0_g': gain((D_MODEL,)),
        'ln0_b': nrm((D_MODEL,), 0.02),
        'w_ada': nrm((DEPTH, D_MODEL, 6 * D_MODEL), 0.1 * D_MODEL ** -0.5),
        'b_ada': nrm((DEPTH, 6 * D_MODEL), 0.02),
        'w_in': nrm((DEPTH, D_MODEL, D_IN_PROJ), D_MODEL ** -0.5),
        'qa_norm_g': gain((DEPTH, HEAD_DIM)),
        'ka_norm_g': gain((DEPTH, HEAD_DIM)),
        'oa_norm_g': gain((DEPTH, A_WIDTH)),
        'qb_norm_g': gain((DEPTH, Q_LORA)),
        'w_uq': nrm((DEPTH, Q_LORA, B_HEADS * (MLA_NOPE + MLA_ROPE)), Q_LORA ** -0.5),
        'kvb_norm_g': gain((DEPTH, KV_LORA)),
        'w_ukv': nrm((DEPTH, KV_LORA, B_HEADS * (MLA_NOPE + MLA_V)), KV_LORA ** -0.5),
        'ob_norm_g': gain((DEPTH, B_WIDTH)),
        'conv_w': nrm((DEPTH, D_CONV, CONV_DIM), D_CONV ** -0.5),
        'conv_b': nrm((DEPTH, CONV_DIM), 0.02),
        'dt_bias': dt0 + jnp.log(-jnp.expm1(-dt0)),
        'a_log': jnp.log(jax.random.uniform(next(ks), (DEPTH, 2, C_HEADS), F32, 1.0, 16.0)),
        'd_skip': gain((DEPTH, C_HEADS), 0.1),
        'ssm_norm_g': gain((DEPTH, C_INNER)),
        'w_out': nrm((DEPTH, MIX_WIDTH, D_MODEL), DN_BETA * MIX_WIDTH ** -0.5),
        'ln1_g': gain((DEPTH, D_MODEL)),
        'ln1_b': nrm((DEPTH, D_MODEL), 0.02),
        'w_rg': nrm((DEPTH, D_MODEL, N_EGROUPS), D_MODEL ** -0.5),
        'b_rg': nrm((DEPTH, N_EGROUPS), 0.01),
        'w_re': nrm((DEPTH, D_MODEL, N_EXPERTS), D_MODEL ** -0.5),
        'b_re': nrm((DEPTH, N_EXPERTS), 0.01),
        'w_e_gate': nrm((DEPTH, N_EXPERTS, D_MODEL, D_FF_EXPERT), D_MODEL ** -0.5),
        'w_e_up': nrm((DEPTH, N_EXPERTS, D_MODEL, D_FF_EXPERT), D_MODEL ** -0.5),
        'w_e_down': nrm((DEPTH, N_EXPERTS, D_FF_EXPERT, D_MODEL), DN_BETA * D_FF_EXPERT ** -0.5),
        'ln2_g': gain((DEPTH, D_MODEL)),
        'ln2_b': nrm((DEPTH, D_MODEL), 0.02),
    }


def reference(x_prompt, x_sample, c_prompt, c_sample, ln0_g, ln0_b, w_ada, b_ada, w_in,
              qa_norm_g, ka_norm_g, oa_norm_g, qb_norm_g, w_uq, kvb_norm_g, w_ukv, ob_norm_g,
              conv_w, conv_b, dt_bias, a_log, d_skip, ssm_norm_g, w_out, ln1_g, ln1_b,
              w_rg, b_rg, w_re, b_re, w_e_gate, w_e_up, w_e_down, ln2_g, ln2_b):
    p = dict(ln0_g=ln0_g, ln0_b=ln0_b, w_ada=w_ada, b_ada=b_ada, w_in=w_in,
             qa_norm_g=qa_norm_g, ka_norm_g=ka_norm_g, oa_norm_g=oa_norm_g,
             qb_norm_g=qb_norm_g, w_uq=w_uq, kvb_norm_g=kvb_norm_g, w_ukv=w_ukv, ob_norm_g=ob_norm_g,
             conv_w=conv_w, conv_b=conv_b, dt_bias=dt_bias, a_log=a_log, d_skip=d_skip,
             ssm_norm_g=ssm_norm_g, w_out=w_out, ln1_g=ln1_g, ln1_b=ln1_b,
             w_rg=w_rg, b_rg=b_rg, w_re=w_re, b_re=b_re,
             w_e_gate=w_e_gate, w_e_up=w_e_up, w_e_down=w_e_down, ln2_g=ln2_g, ln2_b=ln2_b)
    y_prompt = encoder_trunk(x_prompt, c_prompt, p)
    y_sample = encoder_trunk(x_sample, c_sample, p)
    return (y_prompt, y_sample)
```

```python
import functools

import jax
import jax.numpy as jnp
import numpy as np
from jax import lax
from jax.experimental import pallas as pl
from jax.experimental.pallas import tpu as pltpu

F32 = jnp.float32
BF16 = jnp.bfloat16

D_MODEL = 1024
DEPTH = 4
GRID_W = 64
ROPE_THETA = 10000.0
HEAD_DIM = 64
A_HEADS = 4
A_KV_HEADS = 2
B_HEADS = 4
Q_LORA = 384
KV_LORA = 128
MLA_NOPE = 64
MLA_ROPE = 32
MLA_V = 64
C_HEADS = 8
C_HEAD_DIM = 64
SSM_GROUPS = 2
D_STATE = 64
D_CONV = 5
C_INNER = C_HEADS * C_HEAD_DIM
CONV_DIM = C_INNER + 2 * SSM_GROUPS * D_STATE
N_EGROUPS = 4
EXPERTS_PER_GROUP = 8
N_EXPERTS = N_EGROUPS * EXPERTS_PER_GROUP
D_FF_EXPERT = 256
DN_ALPHA = (2 * DEPTH) ** 0.25
EPS = 1e-6

LANES = 128
SSD_Q = 128
NEG_BIG = -1e30

P_QA, P_KA, P_VA, P_CQ, P_CKV, P_KR, P_Z, P_XBC, P_DT, P_END = (
    0, 256, 384, 512, 896, 1024, 1536, 2048, 2816, 2944)


def _params(sem, vmem_mb):
    return pltpu.CompilerParams(dimension_semantics=sem, vmem_limit_bytes=vmem_mb * 1024 * 1024)


def _silu(x):
    return x * (1.0 / (1.0 + jnp.exp(-x)))


def _ln_rows(x, g, b):
    mu = jnp.mean(x, axis=-1, keepdims=True)
    xc = x - mu
    var = jnp.mean(xc * xc, axis=-1, keepdims=True)
    return xc * lax.rsqrt(var + EPS) * g + b


def _rms_rows(x, g):
    return x * lax.rsqrt(jnp.mean(x * x, axis=-1, keepdims=True) + EPS) * g


def _split_bf16(x):
    hi = x.astype(BF16)
    lo = (x - hi.astype(F32)).astype(BF16)
    return hi, lo


def _dot(a, b):
    return jnp.dot(a, b, preferred_element_type=F32)


def _ln0_kernel(x_ref, g_ref, b_ref, o_ref):
    o_ref[...] = _ln_rows(x_ref[...], g_ref[...], b_ref[...])


def _ln0(x2d, g, b, tm):
    t, d = x2d.shape
    return pl.pallas_call(
        _ln0_kernel,
        out_shape=jax.ShapeDtypeStruct((t, d), F32),
        grid=(t // tm,),
        in_specs=[pl.BlockSpec((tm, d), lambda i: (i, 0)),
                  pl.BlockSpec((1, d), lambda i: (0, 0)),
                  pl.BlockSpec((1, d), lambda i: (0, 0))],
        out_specs=pl.BlockSpec((tm, d), lambda i: (i, 0)),
        compiler_params=_params(("parallel",), 32),
        name="ln0",
    )(x2d, g, b)


def _ada_kernel(c_ref, w_ref, b_ref, o_ref):
    cs = _silu(c_ref[...]).astype(BF16)
    o_ref[0] = _dot(cs, w_ref[0]) + b_ref[0]


def _ada(c, w_bf, b, tn=2048):
    s, d = c.shape
    depth, _, n = w_bf.shape
    return pl.pallas_call(
        _ada_kernel,
        out_shape=jax.ShapeDtypeStruct((depth, s, n), F32),
        grid=(depth, n // tn),
        in_specs=[pl.BlockSpec((s, d), lambda l, j: (0, 0)),
                  pl.BlockSpec((1, d, tn), lambda l, j: (l, 0, j)),
                  pl.BlockSpec((1, 1, tn), lambda l, j: (l, 0, j))],
        out_specs=pl.BlockSpec((1, s, tn), lambda l, j: (l, 0, j)),
        compiler_params=_params(("parallel", "parallel"), 32),
        name="ada",
    )(c, w_bf, b.reshape(depth, 1, n))


def _rope(x, c, p, m, shift):
    return x * c + pltpu.roll(x, shift, 1) * p + pltpu.roll(x, LANES - shift, 1) * m


def _headnorm2(x, g):
    xx = x * x
    lo = lax.broadcasted_iota(jnp.int32, x.shape, 1) < HEAD_DIM
    s0 = jnp.sum(jnp.where(lo, xx, 0.0), axis=-1, keepdims=True)
    s1 = jnp.sum(jnp.where(lo, 0.0, xx), axis=-1, keepdims=True)
    inv = jnp.where(lo, lax.rsqrt(s0 * (1.0 / HEAD_DIM) + EPS), lax.rsqrt(s1 * (1.0 / HEAD_DIM) + EPS))
    return x * inv * g


def _inproj_kernel(x_ref, mod_ref, w_ref, wuq_ref, wukk_ref, wukv_ref,
                   gqa_ref, gka_ref, gqb_ref, gkvb_ref,
                   ca_ref, pa_ref, ma_ref, cb_ref, pb_ref, mb_ref,
                   qa_o, ka_o, va_o, qb_o, kb_o, vb_o, z_o, xbc_o, dt_o):
    m = mod_ref[0]
    h = (x_ref[0] * (1.0 + m[1:2]) + m[0:1]).astype(BF16)

    def proj(a, b):
        return _dot(h, w_ref[:, a:b])

    ca, pa, ma = ca_ref[...], pa_ref[...], ma_ref[...]
    cb, pb, mb = cb_ref[...], pb_ref[...], mb_ref[...]
    half = HEAD_DIM // 2
    for ch in range(A_HEADS // 2):
        xa = _headnorm2(proj(P_QA + ch * LANES, P_QA + (ch + 1) * LANES), gqa_ref[...])
        xa = _rope(xa, ca, pa, ma, half // 2) * (HEAD_DIM ** -0.5)
        qa_o[0, 2 * ch] = xa[:, :HEAD_DIM].astype(BF16)
        qa_o[0, 2 * ch + 1] = xa[:, HEAD_DIM:].astype(BF16)
    xk = _rope(_headnorm2(proj(P_KA, P_VA), gka_ref[...]), ca, pa, ma, half // 2)
    ka_o[0, 0] = xk[:, :HEAD_DIM].astype(BF16)
    ka_o[0, 1] = xk[:, HEAD_DIM:].astype(BF16)
    xv = proj(P_VA, P_CQ)
    va_o[0, 0] = xv[:, :HEAD_DIM].astype(BF16)
    va_o[0, 1] = xv[:, HEAD_DIM:].astype(BF16)
    cqn = _rms_rows(proj(P_CQ, P_CKV), gqb_ref[...]).astype(BF16)
    qb = _dot(cqn, wuq_ref[...])
    ckvn = _rms_rows(proj(P_CKV, P_KR), gkvb_ref[...]).astype(BF16)
    kn = _dot(ckvn, wukk_ref[...])
    vv = _dot(ckvn, wukv_ref[...])
    krt = proj(P_KR, P_Z)
    scale_b = (MLA_NOPE + MLA_ROPE) ** -0.5
    for hh in range(B_HEADS):
        sl = slice(hh * LANES, (hh + 1) * LANES)
        qb_o[0, hh] = (_rope(qb[:, sl], cb, pb, mb, MLA_ROPE // 4) * scale_b).astype(BF16)
        kb_o[0, hh] = (kn[:, sl] + _rope(krt[:, sl], cb, pb, mb, MLA_ROPE // 4)).astype(BF16)
        vb_o[0, hh] = vv[:, hh * MLA_V:(hh + 1) * MLA_V].astype(BF16)
    z_o[0] = proj(P_Z, P_XBC)
    xbc_o[0] = proj(P_XBC, P_DT)
    dt_o[0] = proj(P_DT, P_END)


def _inproj(x, mod, lw, tabs, tm):
    s, l, d = x.shape
    nt = l // tm
    full = lambda shape: pl.BlockSpec(shape, lambda si, i: (0,) * len(shape))
    tab = pl.BlockSpec((tm, LANES), lambda si, i: (i, 0))
    head = lambda nh, dh: pl.BlockSpec((1, nh, tm, dh), lambda si, i: (si, 0, i, 0))
    row = lambda w: pl.BlockSpec((1, tm, w), lambda si, i: (si, i, 0))
    out_shape = (
        jax.ShapeDtypeStruct((s, A_HEADS, l, HEAD_DIM), BF16),
        jax.ShapeDtypeStruct((s, A_KV_HEADS, l, HEAD_DIM), BF16),
        jax.ShapeDtypeStruct((s, A_KV_HEADS, l, HEAD_DIM), BF16),
        jax.ShapeDtypeStruct((s, B_HEADS, l, LANES), BF16),
        jax.ShapeDtypeStruct((s, B_HEADS, l, LANES), BF16),
        jax.ShapeDtypeStruct((s, B_HEADS, l, MLA_V), BF16),
        jax.ShapeDtypeStruct((s, l, C_INNER), F32),
        jax.ShapeDtypeStruct((s, l, CONV_DIM), F32),
        jax.ShapeDtypeStruct((s, l, LANES), F32),
    )
    out_specs = (head(A_HEADS, HEAD_DIM), head(A_KV_HEADS, HEAD_DIM), head(A_KV_HEADS, HEAD_DIM),
                 head(B_HEADS, LANES), head(B_HEADS, LANES), head(B_HEADS, MLA_V),
                 row(C_INNER), row(CONV_DIM), row(LANES))
    return pl.pallas_call(
        _inproj_kernel,
        out_shape=out_shape,
        grid=(s, nt),
        in_specs=[row(d),
                  pl.BlockSpec((1, 6, d), lambda si, i: (si, 0, 0)),
                  full((d, P_END)), full((Q_LORA, B_HEADS * LANES)),
                  full((KV_LORA, B_HEADS * LANES)), full((KV_LORA, B_HEADS * MLA_V)),
                  full((1, LANES)), full((1, LANES)), full((1, Q_LORA)), full((1, KV_LORA)),
                  tab, tab, tab, tab, tab, tab],
        out_specs=out_specs,
        compiler_params=_params(("parallel", "parallel"), 48),
        name="inproj",
    )(x, mod, lw["w_in"], lw["w_uq"], lw["w_ukk"], lw["w_ukv"],
      lw["gqa"], lw["gka"], lw["gqb"], lw["gkvb"], *tabs)


def _attn_kernel(q_ref, k_ref, v_ref, o_ref, *, shared_kv):
    outs = []
    for j in range(2):
        kj = 0 if shared_kv else j
        q = q_ref[0, j]
        s = lax.dot_general(q, k_ref[0, kj], (((1,), (1,)), ((), ())), preferred_element_type=F32)
        mx = jnp.max(s, axis=-1, keepdims=True)
        p = jnp.exp(s - mx)
        den = jnp.sum(p, axis=-1, keepdims=True)
        outs.append(_dot(p.astype(BF16), v_ref[0, kj]) / den)
    o_ref[0] = jnp.concatenate(outs, axis=-1)


def _attention(q, k, v, tq, name):
    s, hq, l, dk = q.shape
    hk, dv = k.shape[1], v.shape[3]
    shared = hq // hk == 2
    kvb = 1 if shared else 2
    return pl.pallas_call(
        functools.partial(_attn_kernel, shared_kv=shared),
        out_shape=jax.ShapeDtypeStruct((s, l, hq * dv), F32),
        grid=(s, hq // 2, l // tq),
        in_specs=[pl.BlockSpec((1, 2, tq, dk), lambda si, hp, qi: (si, hp, qi, 0)),
                  pl.BlockSpec((1, kvb, l, dk), lambda si, hp, qi: (si, hp, 0, 0)),
                  pl.BlockSpec((1, kvb, l, dv), lambda si, hp, qi: (si, hp, 0, 0))],
        out_specs=pl.BlockSpec((1, tq, 2 * dv), lambda si, hp, qi: (si, qi, hp)),
        compiler_params=_params(("parallel", "parallel", "arbitrary"), 48),
        name=name,
    )(q, k, v)


def _ssd_kernel(xbc_ref, dt_ref, z_ref, cw_ref, cb_ref, dtb_ref, alog_ref, dskip_ref, g_ref,
                exf_ref, exb_ref, bd_ref, o_ref, xpad, xact, ad_s, hf, hb):
    l = xbc_ref.shape[1]
    nc = l // SSD_Q
    q = SSD_Q
    pad = 8
    xpad[0:pad, :] = jnp.zeros((pad, CONV_DIM), F32)
    xpad[l + pad:l + 2 * pad, :] = jnp.zeros((pad, CONV_DIM), F32)
    xpad[pad:l + pad, :] = xbc_ref[0]
    for c in range(nc):
        acc = jnp.zeros((q, CONV_DIM), F32) + cb_ref[...]
        for k in range(D_CONV):
            r = c * q + pad - D_CONV // 2 + k
            acc = acc + xpad[r:r + q, :] * cw_ref[k:k + 1, :]
        xact[c * q:(c + 1) * q, :] = _silu(acc)
    dtr = dt_ref[0] + dtb_ref[...]
    dt = jnp.maximum(dtr, 0.0) + jnp.log1p(jnp.exp(-jnp.abs(dtr)))
    lane_l = lax.broadcasted_iota(jnp.int32, dt.shape, 1)
    a = dt * (-jnp.exp(alog_ref[...]))
    ad_s[...] = jnp.where(lane_l < 2 * C_HEADS, a, pltpu.roll(dt, 2 * C_HEADS, 1))

    ri = lax.broadcasted_iota(jnp.int32, (q, q), 0)
    ci = lax.broadcasted_iota(jnp.int32, (q, q), 1)
    lower = ri >= ci
    upper = ci >= ri
    tril = jnp.where(lower, 1.0, 0.0).astype(BF16)
    lane_q = lax.broadcasted_iota(jnp.int32, (q, LANES), 1)
    first8 = lane_q < C_HEADS
    mid8 = (lane_q >= C_HEADS) & (lane_q < 2 * C_HEADS)
    lo64 = lane_q < C_HEAD_DIM
    bd = bd_ref[...] > 0.5

    def chunk_decay(r0):
        ad = ad_s[pl.ds(r0, q), :]
        hi, lo = _split_bf16(ad)
        inc = _dot(tril, hi) + _dot(tril, lo)
        exc = inc - ad
        tot = inc[q - 1:q, :]
        dts = pltpu.roll(ad, LANES - 2 * C_HEADS, 1)
        return ad, inc, exc, tot, dts

    def expand(v, ex_ref):
        hi, lo = _split_bf16(v)
        return _dot(hi, ex_ref[...]) + _dot(lo, ex_ref[...])

    hf[...] = jnp.zeros_like(hf)
    hb[...] = jnp.zeros_like(hb)

    def fwd_body(c, carry):
        r0 = pl.multiple_of(c * q, q)
        ad, inc, exc, tot, dts = chunk_decay(r0)
        inc_t = inc.T
        ad_t = ad.T
        exc_t = inc_t - ad_t
        xq = xact[pl.ds(r0, q), :]
        xs = xq[:, :C_INNER]
        xs_bf = xs.astype(BF16)
        bm = xq[:, C_INNER:C_INNER + LANES]
        cm = xq[:, C_INNER + LANES:]
        bt_bf = bm.T.astype(BF16)
        cm_bf = cm.astype(BF16)
        gmat = []
        for g in range(SSM_GROUPS):
            cg = jnp.where(lo64 if g == 0 else jnp.logical_not(lo64), cm, 0.0).astype(BF16)
            gmat.append(_dot(cg, bt_bf))
        for pr in range(C_HEADS // 2):
            ys = []
            for hh in (2 * pr, 2 * pr + 1):
                icol = jnp.broadcast_to(inc[:, hh:hh + 1], (q, q))
                irow = jnp.broadcast_to(inc_t[hh:hh + 1, :], (q, q))
                lf = jnp.exp(jnp.where(lower, icol - irow, NEG_BIG)) * ad_t[16 + hh:17 + hh, :]
                ecol = jnp.broadcast_to(exc[:, 8 + hh:9 + hh], (q, q))
                erow = jnp.broadcast_to(exc_t[8 + hh:9 + hh, :], (q, q))
                lb = jnp.exp(jnp.where(upper, erow - ecol, NEG_BIG)) * ad_t[24 + hh:25 + hh, :]
                w = (gmat[hh // (C_HEADS // SSM_GROUPS)] * (lf + lb)).astype(BF16)
                ys.append(_dot(w, xs_bf[:, pr * LANES:(pr + 1) * LANES]))
            o_ref[0, pl.ds(r0, q), pr * LANES:(pr + 1) * LANES] = jnp.where(lo64, ys[0], ys[1])
        pf = expand(jnp.where(first8, jnp.exp(inc), 0.0), exf_ref)
        sf = expand(jnp.where(first8, jnp.exp(tot - inc) * dts, 0.0), exf_ref)
        y_off = _dot(cm_bf, hf[...].astype(BF16)) * pf
        o_ref[0, pl.ds(r0, q), :] = o_ref[0, pl.ds(r0, q), :] + y_off
        st = _dot(bt_bf, (xs * sf).astype(BF16))
        hf[...] = pf[q - 1:q, :] * hf[...] + jnp.where(bd, st, 0.0)
        return carry

    lax.fori_loop(0, nc, fwd_body, 0)

    def bwd_body(i, carry):
        c = nc - 1 - i
        r0 = pl.multiple_of(c * q, q)
        ad, inc, exc, tot, dts = chunk_decay(r0)
        xq = xact[pl.ds(r0, q), :]
        xs = xq[:, :C_INNER]
        bt_bf = xq[:, C_INNER:C_INNER + LANES].T.astype(BF16)
        cm_bf = xq[:, C_INNER + LANES:].astype(BF16)
        pbk = expand(jnp.where(mid8, jnp.exp(tot - exc), 0.0), exb_ref)
        sbk = expand(jnp.where(mid8, jnp.exp(exc) * dts, 0.0), exb_ref)
        y = o_ref[0, pl.ds(r0, q), :] + _dot(cm_bf, hb[...].astype(BF16)) * pbk
        st = _dot(bt_bf, (xs * sbk).astype(BF16))
        hb[...] = pbk[0:1, :] * hb[...] + jnp.where(bd, st, 0.0)
        y = (y + dskip_ref[...] * xs) * _silu(z_ref[0, pl.ds(r0, q), :])
        o_ref[0, pl.ds(r0, q), :] = _rms_rows(y, g_ref[...])
        return carry

    lax.fori_loop(0, nc, bwd_body, 0)


def _ssd(xbc, dt, z, lw, consts):
    s, l, _ = xbc.shape
    seq = lambda w: pl.BlockSpec((1, l, w), lambda si: (si, 0, 0))
    full = lambda shape: pl.BlockSpec(shape, lambda si: (0,) * len(shape))
    return pl.pallas_call(
        _ssd_kernel,
        out_shape=jax.ShapeDtypeStruct((s, l, C_INNER), F32),
        grid=(s,),
        in_specs=[seq(CONV_DIM), seq(LANES), seq(C_INNER),
                  full((D_CONV, CONV_DIM)), full((1, CONV_DIM)), full((1, LANES)), full((1, LANES)),
                  full((1, C_INNER)), full((1, C_INNER)),
                  full((LANES, C_INNER)), full((LANES, C_INNER)), full((LANES, C_INNER))],
        out_specs=seq(C_INNER),
        scratch_shapes=[pltpu.VMEM((l + 16, CONV_DIM), F32), pltpu.VMEM((l, CONV_DIM), F32),
                        pltpu.VMEM((l, LANES), F32),
                        pltpu.VMEM((LANES, C_INNER), F32), pltpu.VMEM((LANES, C_INNER), F32)],
        compiler_params=_params(("parallel",), 56),
        name="ssd",
    )(xbc, dt, z, lw["conv_w"], lw["conv_b"], lw["dt_bias"], lw["a_log"], lw["d_skip"], lw["ssm_g"],
      consts["exf"], consts["exb"], consts["bd"])


def _route(logits):
    lane = lax.broadcasted_iota(jnp.int32, logits.shape, 1)
    lane_f = lane.astype(F32)
    neg_inf = -jnp.inf
    isg = (lane >= N_EXPERTS) & (lane < N_EXPERTS + N_EGROUPS)
    gl = jnp.where(isg, logits, neg_inf)
    gmax = jnp.max(gl, axis=-1, keepdims=True)
    gsel = jnp.min(jnp.where(gl == gmax, lane_f - N_EXPERTS, 1e9), axis=-1, keepdims=True)
    gw = 1.0 / jnp.sum(jnp.where(isg, jnp.exp(gl - gmax), 0.0), axis=-1, keepdims=True)
    grp = jnp.floor(lane_f * (1.0 / EXPERTS_PER_GROUP))
    el = jnp.where((lane < N_EXPERTS) & (grp == gsel), logits, neg_inf)
    l1 = jnp.max(el, axis=-1, keepdims=True)
    i1 = jnp.min(jnp.where(el == l1, lane_f, 1e9), axis=-1, keepdims=True)
    el2 = jnp.where(lane_f == i1, neg_inf, el)
    l2 = jnp.max(el2, axis=-1, keepdims=True)
    i2 = jnp.min(jnp.where(el2 == l2, lane_f, 1e9), axis=-1, keepdims=True)
    r = jnp.exp(l2 - l1)
    g1 = gw / (1.0 + r)
    g2 = g1 * r
    return jnp.where(lane == 0, i1, jnp.where(lane == 1, i2, jnp.where(lane == 2, g1,
                                                                    jnp.where(lane == 3, g2, 0.0))))


def _outproj_kernel(oa_ref, ob_ref, oc_ref, x_ref, mod_ref, wout_ref, goa_ref, gob_ref,
                    l1g_ref, l1b_ref, wrh_ref, wrl_ref, br_ref, x1_o, h2_o, route_o):
    oa = _rms_rows(oa_ref[0], goa_ref[...]).astype(BF16)
    ob = _rms_rows(ob_ref[0], gob_ref[...]).astype(BF16)
    oc = oc_ref[0].astype(BF16)
    wa = A_HEADS * HEAD_DIM
    wb = wa + B_HEADS * MLA_V
    y = _dot(oa, wout_ref[0:wa, :]) + _dot(ob, wout_ref[wa:wb, :]) + _dot(oc, wout_ref[wb:, :])
    m = mod_ref[0]
    x1 = _ln_rows(DN_ALPHA * x_ref[0] + (1.0 + m[2:3]) * y, l1g_ref[...], l1b_ref[...])
    x1_o[0] = x1
    h2 = x1 * (1.0 + m[4:5]) + m[3:4]
    hi, lo = _split_bf16(h2)
    h2_o[0] = hi
    logits = _dot(hi, wrh_ref[...]) + _dot(lo, wrh_ref[...]) + _dot(hi, wrl_ref[...]) + br_ref[...]
    route_o[0] = _route(logits)


def _outproj(oa, ob, oc, x, mod, lw, tm):
    s, l, d = x.shape
    row = lambda w: pl.BlockSpec((1, tm, w), lambda si, i: (si, i, 0))
    full = lambda shape: pl.BlockSpec(shape, lambda si, i: (0,) * len(shape))
    return pl.pallas_call(
        _outproj_kernel,
        out_shape=(jax.ShapeDtypeStruct((s, l, d), F32), jax.ShapeDtypeStruct((s, l, d), BF16),
                   jax.ShapeDtypeStruct((s, l, LANES), F32)),
        grid=(s, l // tm),
        in_specs=[row(A_HEADS * HEAD_DIM), row(B_HEADS * MLA_V), row(C_INNER), row(d),
                  pl.BlockSpec((1, 6, d), lambda si, i: (si, 0, 0)),
                  full((d, d)), full((1, A_HEADS * HEAD_DIM)), full((1, B_HEADS * MLA_V)),
                  full((1, d)), full((1, d)), full((d, LANES)), full((d, LANES)), full((1, LANES))],
        out_specs=(row(d), row(d), row(LANES)),
        compiler_params=_params(("parallel", "parallel"), 48),
        name="outproj",
    )(oa, ob, oc, x, mod, lw["w_out"], lw["goa"], lw["gob"], lw["ln1_g"], lw["ln1_b"],
      lw["w_r_hi"], lw["w_r_lo"], lw["b_r"])


def _moe_kernel(h2_ref, route_ref, x1_ref, mod_ref, wg_ref, wu_ref, wd_ref, l2g_ref, l2b_ref,
                o_ref, acc):
    e = pl.program_id(2)

    @pl.when(e == 0)
    def _():
        acc[...] = jnp.zeros_like(acc)

    r = route_ref[0]
    ef = e.astype(F32)
    gate = jnp.where(r[:, 0:1] == ef, r[:, 2:3], 0.0) + jnp.where(r[:, 1:2] == ef, r[:, 3:4], 0.0)
    xb = h2_ref[0]
    mid = (_silu(_dot(xb, wg_ref[0])) * _dot(xb, wu_ref[0])).astype(BF16)
    acc[...] += _dot(mid, wd_ref[0]) * gate

    @pl.when(e == N_EXPERTS - 1)
    def _():
        m = mod_ref[0]
        o_ref[0] = _ln_rows(DN_ALPHA * x1_ref[0] + (1.0 + m[5:6]) * acc[...], l2g_ref[...], l2b_ref[...])


def _moe(h2, route, x1, mod, lw, tm):
    s, l, d = x1.shape
    row = lambda w: pl.BlockSpec((1, tm, w), lambda si, i, e: (si, i, 0))
    full = lambda shape: pl.BlockSpec(shape, lambda si, i, e: (0,) * len(shape))
    return pl.pallas_call(
        _moe_kernel,
        out_shape=jax.ShapeDtypeStruct((s, l, d), F32),
        grid=(s, l // tm, N_EXPERTS),
        in_specs=[row(d), row(LANES), row(d),
                  pl.BlockSpec((1, 6, d), lambda si, i, e: (si, 0, 0)),
                  pl.BlockSpec((1, d, D_FF_EXPERT), lambda si, i, e: (e, 0, 0)),
                  pl.BlockSpec((1, d, D_FF_EXPERT), lambda si, i, e: (e, 0, 0)),
                  pl.BlockSpec((1, D_FF_EXPERT, d), lambda si, i, e: (e, 0, 0)),
                  full((1, d)), full((1, d))],
        out_specs=row(d),
        scratch_shapes=[pltpu.VMEM((tm, d), F32)],
        compiler_params=_params(("parallel", "parallel", "arbitrary"), 48),
        name="moe",
    )(h2, route, x1, mod, lw["w_e_gate"], lw["w_e_up"], lw["w_e_down"], lw["ln2_g"], lw["ln2_b"])


def _rope_tables(seq_len):
    t = jnp.arange(seq_len, dtype=jnp.int32)
    row = (t // GRID_W).astype(F32)[:, None]
    col = (t % GRID_W).astype(F32)[:, None]

    def axis_tables(d_axis):
        inv = ROPE_THETA ** (-jnp.arange(0, d_axis, 2, dtype=F32) / d_axis)
        zero = jnp.zeros((seq_len, d_axis // 2), F32)
        cs, ps, ms = [], [], []
        for pos in (row, col):
            ang = pos * inv
            c, sn = jnp.cos(ang), jnp.sin(ang)
            cs += [c, c]
            ps += [zero, sn]
            ms += [-sn, zero]
        return [jnp.concatenate(v, axis=-1) for v in (cs, ps, ms)]

    ta = [jnp.tile(v, (1, 2)) for v in axis_tables(HEAD_DIM // 2)]
    one = jnp.ones((seq_len, MLA_NOPE), F32)
    zero = jnp.zeros((seq_len, MLA_NOPE), F32)
    pad1 = jnp.ones((seq_len, LANES - MLA_NOPE - MLA_ROPE), F32)
    pad0 = jnp.zeros((seq_len, LANES - MLA_NOPE - MLA_ROPE), F32)
    cb, pb, mb = axis_tables(MLA_ROPE // 2)
    tb = [jnp.concatenate([one, cb, pad1], -1), jnp.concatenate([zero, pb, pad0], -1),
          jnp.concatenate([zero, mb, pad0], -1)]
    return ta + tb


def _ssd_consts():
    r = np.arange(LANES)[:, None]
    c = np.arange(C_INNER)[None, :]
    exf = ((r < C_HEADS) & (c // C_HEAD_DIM == r)).astype(np.float32)
    exb = ((r >= C_HEADS) & (r < 2 * C_HEADS) & (c // C_HEAD_DIM == r - C_HEADS)).astype(np.float32)
    heads_per_group = C_HEADS // SSM_GROUPS
    bd = (r // D_STATE == c // (C_HEAD_DIM * heads_per_group)).astype(np.float32)
    return {"exf": jnp.asarray(exf, BF16), "exb": jnp.asarray(exb, BF16), "bd": jnp.asarray(bd, F32)}


def _pack_weights(p):
    w_in = p["w_in"]
    depth = w_in.shape[0]
    off = np.cumsum((0, 256, 128, 128, Q_LORA, KV_LORA, MLA_ROPE, C_INNER, CONV_DIM, 2 * C_HEADS))
    w_kr = w_in[:, :, off[5]:off[6]]
    zpad = lambda n: jnp.zeros((depth, D_MODEL, n), F32)
    kr_t = jnp.concatenate(
        [jnp.concatenate([zpad(MLA_NOPE), w_kr, zpad(LANES - MLA_NOPE - MLA_ROPE)], -1)] * B_HEADS, -1)
    w_in_p = jnp.concatenate([w_in[:, :, :off[5]], kr_t, w_in[:, :, off[6]:off[8]],
                              w_in[:, :, off[8]:], zpad(LANES - 2 * C_HEADS)], -1).astype(BF16)
    dq = MLA_NOPE + MLA_ROPE
    w_uq = p["w_uq"].reshape(depth, Q_LORA, B_HEADS, dq)
    w_uq_p = jnp.concatenate([w_uq, jnp.zeros((depth, Q_LORA, B_HEADS, LANES - dq), F32)], -1)
    w_ukv = p["w_ukv"].reshape(depth, KV_LORA, B_HEADS, MLA_NOPE + MLA_V)
    w_ukk = jnp.concatenate([w_ukv[..., :MLA_NOPE],
                             jnp.zeros((depth, KV_LORA, B_HEADS, LANES - MLA_NOPE), F32)], -1)
    pad_lanes = lambda v: jnp.concatenate([v, jnp.zeros(v.shape[:-1] + (LANES - v.shape[-1],), F32)], -1)
    w_r = pad_lanes(jnp.concatenate([p["w_re"], p["w_rg"]], -1))
    w_r_hi = w_r.astype(BF16)
    return {
        "w_ada": p["w_ada"].astype(BF16), "b_ada": p["b_ada"],
        "w_in": w_in_p,
        "w_uq": w_uq_p.reshape(depth, Q_LORA, B_HEADS * LANES).astype(BF16),
        "w_ukk": w_ukk.reshape(depth, KV_LORA, B_HEADS * LANES).astype(BF16),
        "w_ukv": w_ukv[..., MLA_NOPE:].reshape(depth, KV_LORA, B_HEADS * MLA_V).astype(BF16),
        "gqa": jnp.tile(p["qa_norm_g"], (1, 2))[:, None, :], "gka": jnp.tile(p["ka_norm_g"], (1, 2))[:, None, :],
        "gqb": p["qb_norm_g"][:, None, :], "gkvb": p["kvb_norm_g"][:, None, :],
        "goa": p["oa_norm_g"][:, None, :], "gob": p["ob_norm_g"][:, None, :],
        "conv_w": p["conv_w"], "conv_b": p["conv_b"][:, None, :],
        "dt_bias": pad_lanes(p["dt_bias"].reshape(depth, 2 * C_HEADS))[:, None, :],
        "a_log": pad_lanes(p["a_log"].reshape(depth, 2 * C_HEADS))[:, None, :],
        "d_skip": jnp.repeat(p["d_skip"], C_HEAD_DIM, axis=-1)[:, None, :],
        "ssm_g": p["ssm_norm_g"][:, None, :],
        "w_out": p["w_out"].astype(BF16),
        "ln1_g": p["ln1_g"][:, None, :], "ln1_b": p["ln1_b"][:, None, :],
        "w_r_hi": w_r_hi, "w_r_lo": (w_r - w_r_hi.astype(F32)).astype(BF16),
        "b_r": pad_lanes(jnp.concatenate([p["b_re"], p["b_rg"]], -1))[:, None, :],
        "w_e_gate": p["w_e_gate"].astype(BF16), "w_e_up": p["w_e_up"].astype(BF16),
        "w_e_down": p["w_e_down"].astype(BF16),
        "ln2_g": p["ln2_g"][:, None, :], "ln2_b": p["ln2_b"][:, None, :],
    }


def _trunk(x, c, p, tm=512, tq=256):
    s, l, d = x.shape
    tm = min(tm, l)
    tq = min(tq, l)
    pw = _pack_weights(p)
    tabs = _rope_tables(l)
    consts = _ssd_consts()
    mods = _ada(c, pw["w_ada"], pw["b_ada"]).reshape(DEPTH, s, 6, d)
    x = _ln0(x.reshape(s * l, d), p["ln0_g"][None, :], p["ln0_b"][None, :], tm).reshape(s, l, d)
    for layer in range(DEPTH):
        lw = {k: v[layer] for k, v in pw.items() if k not in ("w_ada", "b_ada")}
        mod = mods[layer]
        qa, ka, va, qb, kb, vb, z, xbc, dt = _inproj(x, mod, lw, tabs, tm)
        oa = _attention(qa, ka, va, tq, "attn_a")
        ob = _attention(qb, kb, vb, tq, "attn_b")
        oc = _ssd(xbc, dt, z, lw, consts)
        x1, h2, route = _outproj(oa, ob, oc, x, mod, lw, tm)
        x = _moe(h2, route, x1, mod, lw, tm)
    return x


def kernel(x_prompt, x_sample, c_prompt, c_sample, ln0_g, ln0_b, w_ada, b_ada, w_in, qa_norm_g, ka_norm_g, oa_norm_g, qb_norm_g, w_uq, kvb_norm_g, w_ukv, ob_norm_g, conv_w, conv_b, dt_bias, a_log, d_skip, ssm_norm_g, w_out, ln1_g, ln1_b, w_rg, b_rg, w_re, b_re, w_e_gate, w_e_up, w_e_down, ln2_g, ln2_b):
    p = dict(ln0_g=ln0_g, ln0_b=ln0_b, w_ada=w_ada, b_ada=b_ada, w_in=w_in,
             qa_norm_g=qa_norm_g, ka_norm_g=ka_norm_g, oa_norm_g=oa_norm_g,
             qb_norm_g=qb_norm_g, w_uq=w_uq, kvb_norm_g=kvb_norm_g, w_ukv=w_ukv, ob_norm_g=ob_norm_g,
             conv_w=conv_w, conv_b=conv_b, dt_bias=dt_bias, a_log=a_log, d_skip=d_skip,
             ssm_norm_g=ssm_norm_g, w_out=w_out, ln1_g=ln1_g, ln1_b=ln1_b,
             w_rg=w_rg, b_rg=b_rg, w_re=w_re, b_re=b_re,
             w_e_gate=w_e_gate, w_e_up=w_e_up, w_e_down=w_e_down, ln2_g=ln2_g, ln2_b=ln2_b)
    nb = x_prompt.shape[0]
    y = _trunk(jnp.concatenate([x_prompt, x_sample], 0), jnp.concatenate([c_prompt, c_sample], 0), p)
    return (y[:nb], y[nb:])
```

```python
import functools

import jax
import jax.numpy as jnp
import numpy as np
from jax import lax
from jax.experimental import pallas as pl
from jax.experimental.pallas import tpu as pltpu

F32 = jnp.float32
BF16 = jnp.bfloat16

D_MODEL = 1024
DEPTH = 4
GRID_W = 64
ROPE_THETA = 10000.0
HEAD_DIM = 64
A_HEADS = 4
A_KV_HEADS = 2
B_HEADS = 4
Q_LORA = 384
KV_LORA = 128
MLA_NOPE = 64
MLA_ROPE = 32
MLA_V = 64
C_HEADS = 8
C_HEAD_DIM = 64
SSM_GROUPS = 2
D_STATE = 64
D_CONV = 5
C_INNER = C_HEADS * C_HEAD_DIM
CONV_DIM = C_INNER + 2 * SSM_GROUPS * D_STATE
N_EGROUPS = 4
EXPERTS_PER_GROUP = 8
N_EXPERTS = N_EGROUPS * EXPERTS_PER_GROUP
D_FF_EXPERT = 256
DN_ALPHA = (2 * DEPTH) ** 0.25
EPS = 1e-6

LANES = 128
SSD_Q = 128
NEG_BIG = -1e30

P_QA, P_KA, P_VA, P_CQ, P_CKV, P_KR, P_Z, P_XBC, P_DT, P_END = (
    0, 256, 384, 512, 896, 1024, 1536, 2048, 2816, 2944)


def _params(sem, vmem_mb):
    return pltpu.CompilerParams(dimension_semantics=sem, vmem_limit_bytes=vmem_mb * 1024 * 1024)


def _silu(x):
    return x * (1.0 / (1.0 + jnp.exp(-x)))


def _ln_rows(x, g, b):
    mu = jnp.mean(x, axis=-1, keepdims=True)
    xc = x - mu
    var = jnp.mean(xc * xc, axis=-1, keepdims=True)
    return xc * lax.rsqrt(var + EPS) * g + b


def _rms_rows(x, g):
    return x * lax.rsqrt(jnp.mean(x * x, axis=-1, keepdims=True) + EPS) * g


def _split_bf16(x):
    hi = x.astype(BF16)
    lo = (x - hi.astype(F32)).astype(BF16)
    return hi, lo


def _dot(a, b):
    return jnp.dot(a, b, preferred_element_type=F32)


def _ln0_kernel(x_ref, g_ref, b_ref, o_ref):
    o_ref[...] = _ln_rows(x_ref[...], g_ref[...], b_ref[...])


def _ln0(x2d, g, b, tm):
    t, d = x2d.shape
    return pl.pallas_call(
        _ln0_kernel,
        out_shape=jax.ShapeDtypeStruct((t, d), F32),
        grid=(t // tm,),
        in_specs=[pl.BlockSpec((tm, d), lambda i: (i, 0)),
                  pl.BlockSpec((1, d), lambda i: (0, 0)),
                  pl.BlockSpec((1, d), lambda i: (0, 0))],
        out_specs=pl.BlockSpec((tm, d), lambda i: (i, 0)),
        compiler_params=_params(("parallel",), 32),
        name="ln0",
    )(x2d, g, b)


def _ada_kernel(c_ref, w_ref, b_ref, o_ref):
    cs = _silu(c_ref[...]).astype(BF16)
    o_ref[0] = _dot(cs, w_ref[0]) + b_ref[0]


def _ada(c, w_bf, b, tn=2048):
    s, d = c.shape
    depth, _, n = w_bf.shape
    return pl.pallas_call(
        _ada_kernel,
        out_shape=jax.ShapeDtypeStruct((depth, s, n), F32),
        grid=(depth, n // tn),
        in_specs=[pl.BlockSpec((s, d), lambda l, j: (0, 0)),
                  pl.BlockSpec((1, d, tn), lambda l, j: (l, 0, j)),
                  pl.BlockSpec((1, 1, tn), lambda l, j: (l, 0, j))],
        out_specs=pl.BlockSpec((1, s, tn), lambda l, j: (l, 0, j)),
        compiler_params=_params(("parallel", "parallel"), 32),
        name="ada",
    )(c, w_bf, b.reshape(depth, 1, n))


def _rope(x, c, p, m, shift):
    return x * c + pltpu.roll(x, shift, 1) * p + pltpu.roll(x, LANES - shift, 1) * m


def _headnorm2(x, g):
    xx = x * x
    lo = lax.broadcasted_iota(jnp.int32, x.shape, 1) < HEAD_DIM
    s0 = jnp.sum(jnp.where(lo, xx, 0.0), axis=-1, keepdims=True)
    s1 = jnp.sum(jnp.where(lo, 0.0, xx), axis=-1, keepdims=True)
    inv = jnp.where(lo, lax.rsqrt(s0 * (1.0 / HEAD_DIM) + EPS), lax.rsqrt(s1 * (1.0 / HEAD_DIM) + EPS))
    return x * inv * g


def _inproj_kernel(x_ref, mod_ref, w_ref, wuq_ref, wukk_ref, wukv_ref,
                   gqa_ref, gka_ref, gqb_ref, gkvb_ref,
                   ca_ref, pa_ref, ma_ref, cb_ref, pb_ref, mb_ref,
                   qa_o, ka_o, va_o, qb_o, kb_o, vb_o, z_o, xbc_o, dt_o):
    m = mod_ref[0]
    h = (x_ref[0] * (1.0 + m[1:2]) + m[0:1]).astype(BF16)

    def proj(a, b):
        return _dot(h, w_ref[:, a:b])

    ca, pa, ma = ca_ref[...], pa_ref[...], ma_ref[...]
    cb, pb, mb = cb_ref[...], pb_ref[...], mb_ref[...]
    half = HEAD_DIM // 2
    for ch in range(A_HEADS // 2):
        xa = _headnorm2(proj(P_QA + ch * LANES, P_QA + (ch + 1) * LANES), gqa_ref[...])
        xa = _rope(xa, ca, pa, ma, half // 2) * (HEAD_DIM ** -0.5)
        qa_o[0, 2 * ch] = xa[:, :HEAD_DIM].astype(BF16)
        qa_o[0, 2 * ch + 1] = xa[:, HEAD_DIM:].astype(BF16)
    xk = _rope(_headnorm2(proj(P_KA, P_VA), gka_ref[...]), ca, pa, ma, half // 2)
    ka_o[0, 0] = xk[:, :HEAD_DIM].astype(BF16)
    ka_o[0, 1] = xk[:, HEAD_DIM:].astype(BF16)
    xv = proj(P_VA, P_CQ)
    va_o[0, 0] = xv[:, :HEAD_DIM].astype(BF16)
    va_o[0, 1] = xv[:, HEAD_DIM:].astype(BF16)
    cqn = _rms_rows(proj(P_CQ, P_CKV), gqb_ref[...]).astype(BF16)
    qb = _dot(cqn, wuq_ref[...])
    ckvn = _rms_rows(proj(P_CKV, P_KR), gkvb_ref[...]).astype(BF16)
    kn = _dot(ckvn, wukk_ref[...])
    vv = _dot(ckvn, wukv_ref[...])
    krt = proj(P_KR, P_Z)
    scale_b = (MLA_NOPE + MLA_ROPE) ** -0.5
    for hh in range(B_HEADS):
        sl = slice(hh * LANES, (hh + 1) * LANES)
        qb_o[0, hh] = (_rope(qb[:, sl], cb, pb, mb, MLA_ROPE // 4) * scale_b).astype(BF16)
        kb_o[0, hh] = (kn[:, sl] + _rope(krt[:, sl], cb, pb, mb, MLA_ROPE // 4)).astype(BF16)
        vb_o[0, hh] = vv[:, hh * MLA_V:(hh + 1) * MLA_V].astype(BF16)
    z_o[0] = proj(P_Z, P_XBC)
    xbc_o[0] = proj(P_XBC, P_DT)
    dt_o[0] = proj(P_DT, P_END)


def _inproj(x, mod, lw, tabs, tm):
    s, l, d = x.shape
    nt = l // tm
    full = lambda shape: pl.BlockSpec(shape, lambda si, i: (0,) * len(shape))
    tab = pl.BlockSpec((tm, LANES), lambda si, i: (i, 0))
    head = lambda nh, dh: pl.BlockSpec((1, nh, tm, dh), lambda si, i: (si, 0, i, 0))
    row = lambda w: pl.BlockSpec((1, tm, w), lambda si, i: (si, i, 0))
    out_shape = (
        jax.ShapeDtypeStruct((s, A_HEADS, l, HEAD_DIM), BF16),
        jax.ShapeDtypeStruct((s, A_KV_HEADS, l, HEAD_DIM), BF16),
        jax.ShapeDtypeStruct((s, A_KV_HEADS, l, HEAD_DIM), BF16),
        jax.ShapeDtypeStruct((s, B_HEADS, l, LANES), BF16),
        jax.ShapeDtypeStruct((s, B_HEADS, l, LANES), BF16),
        jax.ShapeDtypeStruct((s, B_HEADS, l, MLA_V), BF16),
        jax.ShapeDtypeStruct((s, l, C_INNER), F32),
        jax.ShapeDtypeStruct((s, l, CONV_DIM), F32),
        jax.ShapeDtypeStruct((s, l, LANES), F32),
    )
    out_specs = (head(A_HEADS, HEAD_DIM), head(A_KV_HEADS, HEAD_DIM), head(A_KV_HEADS, HEAD_DIM),
                 head(B_HEADS, LANES), head(B_HEADS, LANES), head(B_HEADS, MLA_V),
                 row(C_INNER), row(CONV_DIM), row(LANES))
    return pl.pallas_call(
        _inproj_kernel,
        out_shape=out_shape,
        grid=(s, nt),
        in_specs=[row(d),
                  pl.BlockSpec((1, 6, d), lambda si, i: (si, 0, 0)),
                  full((d, P_END)), full((Q_LORA, B_HEADS * LANES)),
                  full((KV_LORA, B_HEADS * LANES)), full((KV_LORA, B_HEADS * MLA_V)),
                  full((1, LANES)), full((1, LANES)), full((1, Q_LORA)), full((1, KV_LORA)),
                  tab, tab, tab, tab, tab, tab],
        out_specs=out_specs,
        compiler_params=_params(("parallel", "parallel"), 48),
        name="inproj",
    )(x, mod, lw["w_in"], lw["w_uq"], lw["w_ukk"], lw["w_ukv"],
      lw["gqa"], lw["gka"], lw["gqb"], lw["gkvb"], *tabs)


def _attn_kernel(q_ref, k_ref, v_ref, o_ref, *, shared_kv):
    outs = []
    for j in range(2):
        kj = 0 if shared_kv else j
        q = q_ref[0, j]
        s = lax.dot_general(q, k_ref[0, kj], (((1,), (1,)), ((), ())), preferred_element_type=F32)
        mx = jnp.max(s, axis=-1, keepdims=True)
        p = jnp.exp(s - mx)
        den = jnp.sum(p, axis=-1, keepdims=True)
        outs.append(_dot(p.astype(BF16), v_ref[0, kj]) / den)
    o_ref[0] = jnp.concatenate(outs, axis=-1)


def _attention(q, k, v, tq, name):
    s, hq, l, dk = q.shape
    hk, dv = k.shape[1], v.shape[3]
    shared = hq // hk == 2
    kvb = 1 if shared else 2
    return pl.pallas_call(
        functools.partial(_attn_kernel, shared_kv=shared),
        out_shape=jax.ShapeDtypeStruct((s, l, hq * dv), F32),
        grid=(s, hq // 2, l // tq),
        in_specs=[pl.BlockSpec((1, 2, tq, dk), lambda si, hp, qi: (si, hp, qi, 0)),
                  pl.BlockSpec((1, kvb, l, dk), lambda si, hp, qi: (si, hp, 0, 0)),
                  pl.BlockSpec((1, kvb, l, dv), lambda si, hp, qi: (si, hp, 0, 0))],
        out_specs=pl.BlockSpec((1, tq, 2 * dv), lambda si, hp, qi: (si, qi, hp)),
        compiler_params=_params(("parallel", "parallel", "arbitrary"), 48),
        name=name,
    )(q, k, v)


def _ssd_kernel(xbc_ref, dt_ref, z_ref, cw_ref, cb_ref, dtb_ref, alog_ref, dskip_ref, g_ref,
                exf_ref, exb_ref, bd_ref, o_ref, xpad, xact, ad_s, hf, hb):
    l = xbc_ref.shape[1]
    nc = l // SSD_Q
    q = SSD_Q
    pad = 8
    xpad[0:pad, :] = jnp.zeros((pad, CONV_DIM), F32)
    xpad[l + pad:l + 2 * pad, :] = jnp.zeros((pad, CONV_DIM), F32)
    xpad[pad:l + pad, :] = xbc_ref[0]
    for c in range(nc):
        acc = jnp.zeros((q, CONV_DIM), F32) + cb_ref[...]
        for k in range(D_CONV):
            r = c * q + pad - D_CONV // 2 + k
            acc = acc + xpad[r:r + q, :] * cw_ref[k:k + 1, :]
        xact[c * q:(c + 1) * q, :] = _silu(acc)
    dtr = dt_ref[0] + dtb_ref[...]
    dt = jnp.maximum(dtr, 0.0) + jnp.log1p(jnp.exp(-jnp.abs(dtr)))
    lane_l = lax.broadcasted_iota(jnp.int32, dt.shape, 1)
    a = dt * (-jnp.exp(alog_ref[...]))
    ad_s[...] = jnp.where(lane_l < 2 * C_HEADS, a, pltpu.roll(dt, 2 * C_HEADS, 1))

    ri = lax.broadcasted_iota(jnp.int32, (q, q), 0)
    ci = lax.broadcasted_iota(jnp.int32, (q, q), 1)
    lower = ri >= ci
    upper = ci >= ri
    tril = jnp.where(lower, 1.0, 0.0).astype(BF16)
    lane_q = lax.broadcasted_iota(jnp.int32, (q, LANES), 1)
    first8 = lane_q < C_HEADS
    mid8 = (lane_q >= C_HEADS) & (lane_q < 2 * C_HEADS)
    lo64 = lane_q < C_HEAD_DIM
    bd = bd_ref[...] > 0.5

    def chunk_decay(r0):
        ad = ad_s[pl.ds(r0, q), :]
        hi, lo = _split_bf16(ad)
        inc = _dot(tril, hi) + _dot(tril, lo)
        exc = inc - ad
        tot = inc[q - 1:q, :]
        dts = pltpu.roll(ad, LANES - 2 * C_HEADS, 1)
        return ad, inc, exc, tot, dts

    def expand(v, ex_ref):
        hi, lo = _split_bf16(v)
        return _dot(hi, ex_ref[...]) + _dot(lo, ex_ref[...])

    hf[...] = jnp.zeros_like(hf)
    hb[...] = jnp.zeros_like(hb)

    def fwd_body(c, carry):
        r0 = pl.multiple_of(c * q, q)
        ad, inc, exc, tot, dts = chunk_decay(r0)
        inc_t = inc.T
        ad_t = ad.T
        exc_t = inc_t - ad_t
        xq = xact[pl.ds(r0, q), :]
        xs = xq[:, :C_INNER]
        xs_bf = xs.astype(BF16)
        bm = xq[:, C_INNER:C_INNER + LANES]
        cm = xq[:, C_INNER + LANES:]
        bt_bf = bm.T.astype(BF16)
        cm_bf = cm.astype(BF16)
        gmat = []
        for g in range(SSM_GROUPS):
            cg = jnp.where(lo64 if g == 0 else jnp.logical_not(lo64), cm, 0.0).astype(BF16)
            gmat.append(_dot(cg, bt_bf))
        for pr in range(C_HEADS // 2):
            ys = []
            for hh in (2 * pr, 2 * pr + 1):
                icol = jnp.broadcast_to(inc[:, hh:hh + 1], (q, q))
                irow = jnp.broadcast_to(inc_t[hh:hh + 1, :], (q, q))
                lf = jnp.exp(jnp.where(lower, icol - irow, NEG_BIG)) * ad_t[16 + hh:17 + hh, :]
                ecol = jnp.broadcast_to(exc[:, 8 + hh:9 + hh], (q, q))
                erow = jnp.broadcast_to(exc_t[8 + hh:9 + hh, :], (q, q))
                lb = jnp.exp(jnp.where(upper, erow - ecol, NEG_BIG)) * ad_t[24 + hh:25 + hh, :]
                w = (gmat[hh // (C_HEADS // SSM_GROUPS)] * (lf + lb)).astype(BF16)
                ys.append(_dot(w, xs_bf[:, pr * LANES:(pr + 1) * LANES]))
            o_ref[0, pl.ds(r0, q), pr * LANES:(pr + 1) * LANES] = jnp.where(lo64, ys[0], ys[1])
        pf = expand(jnp.where(first8, jnp.exp(inc), 0.0), exf_ref)
        sf = expand(jnp.where(first8, jnp.exp(tot - inc) * dts, 0.0), exf_ref)
        y_off = _dot(cm_bf, hf[...].astype(BF16)) * pf
        o_ref[0, pl.ds(r0, q), :] = o_ref[0, pl.ds(r0, q), :] + y_off
        st = _dot(bt_bf, (xs * sf).astype(BF16))
        hf[...] = pf[q - 1:q, :] * hf[...] + jnp.where(bd, st, 0.0)
        return carry

    lax.fori_loop(0, nc, fwd_body, 0)

    def bwd_body(i, carry):
        c = nc - 1 - i
        r0 = pl.multiple_of(c * q, q)
        ad, inc, exc, tot, dts = chunk_decay(r0)
        xq = xact[pl.ds(r0, q), :]
        xs = xq[:, :C_INNER]
        bt_bf = xq[:, C_INNER:C_INNER + LANES].T.astype(BF16)
        cm_bf = xq[:, C_INNER + LANES:].astype(BF16)
        pbk = expand(jnp.where(mid8, jnp.exp(tot - exc), 0.0), exb_ref)
        sbk = expand(jnp.where(mid8, jnp.exp(exc) * dts, 0.0), exb_ref)
        y = o_ref[0, pl.ds(r0, q), :] + _dot(cm_bf, hb[...].astype(BF16)) * pbk
        st = _dot(bt_bf, (xs * sbk).astype(BF16))
        hb[...] = pbk[0:1, :] * hb[...] + jnp.where(bd, st, 0.0)
        y = (y + dskip_ref[...] * xs) * _silu(z_ref[0, pl.ds(r0, q), :])
        o_ref[0, pl.ds(r0, q), :] = _rms_rows(y, g_ref[...])
        return carry

    lax.fori_loop(0, nc, bwd_body, 0)


def _ssd(xbc, dt, z, lw, consts):
    s, l, _ = xbc.shape
    seq = lambda w: pl.BlockSpec((1, l, w), lambda si: (si, 0, 0))
    full = lambda shape: pl.BlockSpec(shape, lambda si: (0,) * len(shape))
    return pl.pallas_call(
        _ssd_kernel,
        out_shape=jax.ShapeDtypeStruct((s, l, C_INNER), F32),
        grid=(s,),
        in_specs=[seq(CONV_DIM), seq(LANES), seq(C_INNER),
                  full((D_CONV, CONV_DIM)), full((1, CONV_DIM)), full((1, LANES)), full((1, LANES)),
                  full((1, C_INNER)), full((1, C_INNER)),
                  full((LANES, C_INNER)), full((LANES, C_INNER)), full((LANES, C_INNER))],
        out_specs=seq(C_INNER),
        scratch_shapes=[pltpu.VMEM((l + 16, CONV_DIM), F32), pltpu.VMEM((l, CONV_DIM), F32),
                        pltpu.VMEM((l, LANES), F32),
                        pltpu.VMEM((LANES, C_INNER), F32), pltpu.VMEM((LANES, C_INNER), F32)],
        compiler_params=_params(("parallel",), 56),
        name="ssd",
    )(xbc, dt, z, lw["conv_w"], lw["conv_b"], lw["dt_bias"], lw["a_log"], lw["d_skip"], lw["ssm_g"],
      consts["exf"], consts["exb"], consts["bd"])


def _route(logits):
    lane = lax.broadcasted_iota(jnp.int32, logits.shape, 1)
    lane_f = lane.astype(F32)
    neg_inf = -jnp.inf
    isg = (lane >= N_EXPERTS) & (lane < N_EXPERTS + N_EGROUPS)
    gl = jnp.where(isg, logits, neg_inf)
    gmax = jnp.max(gl, axis=-1, keepdims=True)
    gsel = jnp.min(jnp.where(gl == gmax, lane_f - N_EXPERTS, 1e9), axis=-1, keepdims=True)
    gw = 1.0 / jnp.sum(jnp.where(isg, jnp.exp(gl - gmax), 0.0), axis=-1, keepdims=True)
    grp = jnp.floor(lane_f * (1.0 / EXPERTS_PER_GROUP))
    el = jnp.where((lane < N_EXPERTS) & (grp == gsel), logits, neg_inf)
    l1 = jnp.max(el, axis=-1, keepdims=True)
    i1 = jnp.min(jnp.where(el == l1, lane_f, 1e9), axis=-1, keepdims=True)
    el2 = jnp.where(lane_f == i1, neg_inf, el)
    l2 = jnp.max(el2, axis=-1, keepdims=True)
    i2 = jnp.min(jnp.where(el2 == l2, lane_f, 1e9), axis=-1, keepdims=True)
    r = jnp.exp(l2 - l1)
    g1 = gw / (1.0 + r)
    g2 = g1 * r
    return jnp.where(lane == 0, i1, jnp.where(lane == 1, i2, jnp.where(lane == 2, g1,
                                                                    jnp.where(lane == 3, g2, 0.0))))


def _outproj_kernel(oa_ref, ob_ref, oc_ref, x_ref, mod_ref, wout_ref, goa_ref, gob_ref,
                    l1g_ref, l1b_ref, wrh_ref, wrl_ref, br_ref, x1_o, h2_o, route_o):
    oa = _rms_rows(oa_ref[0], goa_ref[...]).astype(BF16)
    ob = _rms_rows(ob_ref[0], gob_ref[...]).astype(BF16)
    oc = oc_ref[0].astype(BF16)
    wa = A_HEADS * HEAD_DIM
    wb = wa + B_HEADS * MLA_V
    y = _dot(oa, wout_ref[0:wa, :]) + _dot(ob, wout_ref[wa:wb, :]) + _dot(oc, wout_ref[wb:, :])
    m = mod_ref[0]
    x1 = _ln_rows(DN_ALPHA * x_ref[0] + (1.0 + m[2:3]) * y, l1g_ref[...], l1b_ref[...])
    x1_o[0] = x1
    h2 = x1 * (1.0 + m[4:5]) + m[3:4]
    h2_o[0] = h2
    hi, lo = _split_bf16(h2)
    logits = _dot(hi, wrh_ref[...]) + _dot(lo, wrh_ref[...]) + _dot(hi, wrl_ref[...]) + br_ref[...]
    route_o[0] = _route(logits)


def _outproj(oa, ob, oc, x, mod, lw, tm):
    s, l, d = x.shape
    row = lambda w: pl.BlockSpec((1, tm, w), lambda si, i: (si, i, 0))
    full = lambda shape: pl.BlockSpec(shape, lambda si, i: (0,) * len(shape))
    return pl.pallas_call(
        _outproj_kernel,
        out_shape=(jax.ShapeDtypeStruct((s, l, d), F32), jax.ShapeDtypeStruct((s, l, d), F32),
                   jax.ShapeDtypeStruct((s, l, LANES), F32)),
        grid=(s, l // tm),
        in_specs=[row(A_HEADS * HEAD_DIM), row(B_HEADS * MLA_V), row(C_INNER), row(d),
                  pl.BlockSpec((1, 6, d), lambda si, i: (si, 0, 0)),
                  full((d, d)), full((1, A_HEADS * HEAD_DIM)), full((1, B_HEADS * MLA_V)),
                  full((1, d)), full((1, d)), full((d, LANES)), full((d, LANES)), full((1, LANES))],
        out_specs=(row(d), row(d), row(LANES)),
        compiler_params=_params(("parallel", "parallel"), 48),
        name="outproj",
    )(oa, ob, oc, x, mod, lw["w_out"], lw["goa"], lw["gob"], lw["ln1_g"], lw["ln1_b"],
      lw["w_r_hi"], lw["w_r_lo"], lw["b_r"])


MOE_ROWS = 128
ROW_STRIDE = MOE_ROWS + 8


def _slot_rows(l):
    return 2 * l + N_EXPERTS * 8 + MOE_ROWS


def _route_sort_kernel(route_ref, slots_o, gates_o, meta_o):
    r = route_ref[0]
    l = r.shape[0]
    lane = lax.broadcasted_iota(jnp.int32, (l, LANES), 1)
    lane_f = lane.astype(F32)
    oh1 = lane_f == jnp.broadcast_to(r[:, 0:1], (l, LANES))
    oh2 = lane_f == jnp.broadcast_to(r[:, 1:2], (l, LANES))
    member = jnp.where(oh1 | oh2, 1.0, 0.0).astype(BF16)
    tb = min(256, l)
    ri = lax.broadcasted_iota(jnp.int32, (tb, tb), 0)
    ci = lax.broadcasted_iota(jnp.int32, (tb, tb), 1)
    stril = jnp.where(ri > ci, 1.0, 0.0).astype(BF16)
    run = jnp.zeros((1, LANES), F32)
    ranks = []
    for i in range(l // tb):
        blk = member[i * tb:(i + 1) * tb]
        ranks.append(_dot(stril, blk) + run)
        run = run + jnp.sum(blk.astype(F32), axis=0, keepdims=True)
    rank_all = jnp.concatenate(ranks, axis=0)
    padded = jnp.floor((run + 7.0) * 0.125) * 8.0
    hi, lo = _split_bf16(jnp.broadcast_to(padded, (8, LANES)))
    rl = lax.broadcasted_iota(jnp.int32, (LANES, LANES), 0)
    cl = lax.broadcasted_iota(jnp.int32, (LANES, LANES), 1)
    upper = jnp.where(rl < cl, 1.0, 0.0).astype(BF16)
    base = (_dot(hi, upper) + _dot(lo, upper))[0:1]
    pos = base + rank_all
    slot1 = jnp.sum(jnp.where(oh1, pos, 0.0), axis=-1, keepdims=True)
    slot2 = jnp.sum(jnp.where(oh2, pos, 0.0), axis=-1, keepdims=True)
    packed = jnp.where(lane == 0, slot1, jnp.where(lane == 1, slot2, jnp.where(lane < 4, r, 0.0)))
    pt = packed.T
    slots_o[0] = pt[0:2].astype(jnp.int32)
    gates_o[0] = pt[2:4]
    meta_o[0] = jnp.concatenate([base, run, jnp.zeros((6, LANES), F32)], axis=0).astype(jnp.int32)


def _route_sort(route):
    s, l, _ = route.shape
    return pl.pallas_call(
        _route_sort_kernel,
        out_shape=(jax.ShapeDtypeStruct((s, 2, l), jnp.int32), jax.ShapeDtypeStruct((s, 2, l), F32),
                   jax.ShapeDtypeStruct((s, 8, LANES), jnp.int32)),
        grid=(s,),
        in_specs=[pl.BlockSpec((1, l, LANES), lambda si: (si, 0, 0))],
        out_specs=(pl.BlockSpec((1, 2, l), lambda si: (si, 0, 0)), pl.BlockSpec((1, 2, l), lambda si: (si, 0, 0)),
                   pl.BlockSpec((1, 8, LANES), lambda si: (si, 0, 0))),
        compiler_params=_params(("parallel",), 32),
        name="route_sort",
    )(route)


def _moe_kernel(meta_ref, h2v_ref, slots_hbm, gates_hbm, wg_ref, wu_ref, wd_ref, o_ref,
                slot_s, gate_s, tok_s, gat_s, acc_v, xt, yt, sem):
    s = pl.program_id(0)
    e = pl.program_id(1)
    l = h2v_ref.shape[0] // 8
    n_slots = tok_s.shape[0]
    nch = D_MODEL // LANES

    @pl.when(e == 0)
    def _():
        c_slots = pltpu.make_async_copy(slots_hbm.at[s], slot_s, sem.at[0])
        c_gates = pltpu.make_async_copy(gates_hbm.at[s], gate_s, sem.at[1])
        c_slots.start()
        c_gates.start()
        acc_v[...] = jnp.zeros_like(acc_v)
        xt[...] = jnp.zeros_like(xt)

        def clear(i, c):
            tok_s[i] = l
            gat_s[i] = 0.0
            return c

        lax.fori_loop(0, n_slots, clear, 0)
        c_slots.wait()
        c_gates.wait()

        def invert(t, c):
            for k in range(2):
                sl = slot_s[k, t]
                tok_s[sl] = t
                gat_s[sl] = gate_s[k, t]
            return c

        lax.fori_loop(0, l, invert, 0)

    base = meta_ref[s, 0, e]
    ngrp = (meta_ref[s, 1, e] + 7) // 8
    grp_per_blk = MOE_ROWS // 8

    def block(blk, c):
        g_lo = blk * grp_per_blk
        ng = jnp.minimum(grp_per_blk, ngrp - g_lo)

        def gather(g, cc):
            r0 = base + (g_lo + g) * 8
            for j in range(8):
                t = jnp.minimum(tok_s[r0 + j], l - 1)
                row = h2v_ref[pl.ds(pl.multiple_of(t * 8, 8), 8), :]
                xt[pl.ds(g * 8 + j, nch, stride=ROW_STRIDE), :] = row
            return cc

        lax.fori_loop(0, ng, gather, 0)
        xb = jnp.concatenate([xt[j * ROW_STRIDE:j * ROW_STRIDE + MOE_ROWS, :] for j in range(nch)],
                             axis=-1).astype(BF16)
        mid = (_silu(_dot(xb, wg_ref[0])) * _dot(xb, wu_ref[0])).astype(BF16)
        y = _dot(mid, wd_ref[0])
        for j in range(nch):
            yt[j * ROW_STRIDE:j * ROW_STRIDE + MOE_ROWS, :] = y[:, j * LANES:(j + 1) * LANES]

        def combine(g, cc):
            r0 = base + (g_lo + g) * 8
            toks = [pl.multiple_of(tok_s[r0 + j] * 8, 8) for j in range(8)]
            vals = [acc_v[pl.ds(toks[j], 8), :]
                    + gat_s[r0 + j] * yt[pl.ds(g * 8 + j, nch, stride=ROW_STRIDE), :] for j in range(8)]
            for j in range(8):
                acc_v[pl.ds(toks[j], 8), :] = vals[j]
            return cc

        lax.fori_loop(0, ng, combine, 0)
        return c

    lax.fori_loop(0, (ngrp + grp_per_blk - 1) // grp_per_blk, block, 0)

    @pl.when(e == N_EXPERTS - 1)
    def _():
        o_ref[...] = acc_v[0:l * 8, :]


def _moe(h2, slots, gates, meta, lw):
    s, l, d = h2.shape
    h2v = h2.reshape(s * l * (d // LANES), LANES)
    n_slots = _slot_rows(l)
    out = pl.pallas_call(
        _moe_kernel,
        out_shape=jax.ShapeDtypeStruct(h2v.shape, F32),
        grid=(s, N_EXPERTS),
        in_specs=[pl.BlockSpec(memory_space=pltpu.SMEM),
                  pl.BlockSpec((l * 8, LANES), lambda si, e: (si, 0)),
                  pl.BlockSpec(memory_space=pl.ANY), pl.BlockSpec(memory_space=pl.ANY),
                  pl.BlockSpec((1, d, D_FF_EXPERT), lambda si, e: (e, 0, 0)),
                  pl.BlockSpec((1, d, D_FF_EXPERT), lambda si, e: (e, 0, 0)),
                  pl.BlockSpec((1, D_FF_EXPERT, d), lambda si, e: (e, 0, 0))],
        out_specs=pl.BlockSpec((l * 8, LANES), lambda si, e: (si, 0)),
        scratch_shapes=[pltpu.SMEM((2, l), jnp.int32), pltpu.SMEM((2, l), F32),
                        pltpu.SMEM((n_slots,), jnp.int32), pltpu.SMEM((n_slots,), F32),
                        pltpu.VMEM(((l + 1) * 8, LANES), F32),
                        pltpu.VMEM((8 * ROW_STRIDE, LANES), F32), pltpu.VMEM((8 * ROW_STRIDE, LANES), F32),
                        pltpu.SemaphoreType.DMA((2,))],
        compiler_params=_params(("parallel", "arbitrary"), 56),
        name="moe",
    )(meta, h2v, slots, gates, lw["w_e_gate"], lw["w_e_up"], lw["w_e_down"])
    return out.reshape(s, l, d)


def _ln2_kernel(x1_ref, y_ref, mod_ref, g_ref, b_ref, o_ref):
    m = mod_ref[0]
    o_ref[0] = _ln_rows(DN_ALPHA * x1_ref[0] + (1.0 + m[5:6]) * y_ref[0], g_ref[...], b_ref[...])


def _ln2(x1, y, mod, lw, tm):
    s, l, d = x1.shape
    row = pl.BlockSpec((1, tm, d), lambda si, i: (si, i, 0))
    vec = pl.BlockSpec((1, d), lambda si, i: (0, 0))
    return pl.pallas_call(
        _ln2_kernel,
        out_shape=jax.ShapeDtypeStruct((s, l, d), F32),
        grid=(s, l // tm),
        in_specs=[row, row, pl.BlockSpec((1, 6, d), lambda si, i: (si, 0, 0)), vec, vec],
        out_specs=row,
        compiler_params=_params(("parallel", "parallel"), 32),
        name="ln2",
    )(x1, y, mod, lw["ln2_g"], lw["ln2_b"])


def _rope_tables(seq_len):
    t = jnp.arange(seq_len, dtype=jnp.int32)
    row = (t // GRID_W).astype(F32)[:, None]
    col = (t % GRID_W).astype(F32)[:, None]

    def axis_tables(d_axis):
        inv = ROPE_THETA ** (-jnp.arange(0, d_axis, 2, dtype=F32) / d_axis)
        zero = jnp.zeros((seq_len, d_axis // 2), F32)
        cs, ps, ms = [], [], []
        for pos in (row, col):
            ang = pos * inv
            c, sn = jnp.cos(ang), jnp.sin(ang)
            cs += [c, c]
            ps += [zero, sn]
            ms += [-sn, zero]
        return [jnp.concatenate(v, axis=-1) for v in (cs, ps, ms)]

    ta = [jnp.tile(v, (1, 2)) for v in axis_tables(HEAD_DIM // 2)]
    one = jnp.ones((seq_len, MLA_NOPE), F32)
    zero = jnp.zeros((seq_len, MLA_NOPE), F32)
    pad1 = jnp.ones((seq_len, LANES - MLA_NOPE - MLA_ROPE), F32)
    pad0 = jnp.zeros((seq_len, LANES - MLA_NOPE - MLA_ROPE), F32)
    cb, pb, mb = axis_tables(MLA_ROPE // 2)
    tb = [jnp.concatenate([one, cb, pad1], -1), jnp.concatenate([zero, pb, pad0], -1),
          jnp.concatenate([zero, mb, pad0], -1)]
    return ta + tb


def _ssd_consts():
    r = np.arange(LANES)[:, None]
    c = np.arange(C_INNER)[None, :]
    exf = ((r < C_HEADS) & (c // C_HEAD_DIM == r)).astype(np.float32)
    exb = ((r >= C_HEADS) & (r < 2 * C_HEADS) & (c // C_HEAD_DIM == r - C_HEADS)).astype(np.float32)
    heads_per_group = C_HEADS // SSM_GROUPS
    bd = (r // D_STATE == c // (C_HEAD_DIM * heads_per_group)).astype(np.float32)
    return {"exf": jnp.asarray(exf, BF16), "exb": jnp.asarray(exb, BF16), "bd": jnp.asarray(bd, F32)}


def _pack_weights(p):
    w_in = p["w_in"]
    depth = w_in.shape[0]
    off = np.cumsum((0, 256, 128, 128, Q_LORA, KV_LORA, MLA_ROPE, C_INNER, CONV_DIM, 2 * C_HEADS))
    w_kr = w_in[:, :, off[5]:off[6]]
    zpad = lambda n: jnp.zeros((depth, D_MODEL, n), F32)
    kr_t = jnp.concatenate(
        [jnp.concatenate([zpad(MLA_NOPE), w_kr, zpad(LANES - MLA_NOPE - MLA_ROPE)], -1)] * B_HEADS, -1)
    w_in_p = jnp.concatenate([w_in[:, :, :off[5]], kr_t, w_in[:, :, off[6]:off[8]],
                              w_in[:, :, off[8]:], zpad(LANES - 2 * C_HEADS)], -1).astype(BF16)
    dq = MLA_NOPE + MLA_ROPE
    w_uq = p["w_uq"].reshape(depth, Q_LORA, B_HEADS, dq)
    w_uq_p = jnp.concatenate([w_uq, jnp.zeros((depth, Q_LORA, B_HEADS, LANES - dq), F32)], -1)
    w_ukv = p["w_ukv"].reshape(depth, KV_LORA, B_HEADS, MLA_NOPE + MLA_V)
    w_ukk = jnp.concatenate([w_ukv[..., :MLA_NOPE],
                             jnp.zeros((depth, KV_LORA, B_HEADS, LANES - MLA_NOPE), F32)], -1)
    pad_lanes = lambda v: jnp.concatenate([v, jnp.zeros(v.shape[:-1] + (LANES - v.shape[-1],), F32)], -1)
    w_r = pad_lanes(jnp.concatenate([p["w_re"], p["w_rg"]], -1))
    w_r_hi = w_r.astype(BF16)
    return {
        "w_ada": p["w_ada"].astype(BF16), "b_ada": p["b_ada"],
        "w_in": w_in_p,
        "w_uq": w_uq_p.reshape(depth, Q_LORA, B_HEADS * LANES).astype(BF16),
        "w_ukk": w_ukk.reshape(depth, KV_LORA, B_HEADS * LANES).astype(BF16),
        "w_ukv": w_ukv[..., MLA_NOPE:].reshape(depth, KV_LORA, B_HEADS * MLA_V).astype(BF16),
        "gqa": jnp.tile(p["qa_norm_g"], (1, 2))[:, None, :], "gka": jnp.tile(p["ka_norm_g"], (1, 2))[:, None, :],
        "gqb": p["qb_norm_g"][:, None, :], "gkvb": p["kvb_norm_g"][:, None, :],
        "goa": p["oa_norm_g"][:, None, :], "gob": p["ob_norm_g"][:, None, :],
        "conv_w": p["conv_w"], "conv_b": p["conv_b"][:, None, :],
        "dt_bias": pad_lanes(p["dt_bias"].reshape(depth, 2 * C_HEADS))[:, None, :],
        "a_log": pad_lanes(p["a_log"].reshape(depth, 2 * C_HEADS))[:, None, :],
        "d_skip": jnp.repeat(p["d_skip"], C_HEAD_DIM, axis=-1)[:, None, :],
        "ssm_g": p["ssm_norm_g"][:, None, :],
        "w_out": p["w_out"].astype(BF16),
        "ln1_g": p["ln1_g"][:, None, :], "ln1_b": p["ln1_b"][:, None, :],
        "w_r_hi": w_r_hi, "w_r_lo": (w_r - w_r_hi.astype(F32)).astype(BF16),
        "b_r": pad_lanes(jnp.concatenate([p["b_re"], p["b_rg"]], -1))[:, None, :],
        "w_e_gate": p["w_e_gate"].astype(BF16), "w_e_up": p["w_e_up"].astype(BF16),
        "w_e_down": p["w_e_down"].astype(BF16),
        "ln2_g": p["ln2_g"][:, None, :], "ln2_b": p["ln2_b"][:, None, :],
    }


def _trunk(x, c, p, tm=512, tq=256):
    s, l, d = x.shape
    tm = min(tm, l)
    tq = min(tq, l)
    pw = _pack_weights(p)
    tabs = _rope_tables(l)
    consts = _ssd_consts()
    mods = _ada(c, pw["w_ada"], pw["b_ada"]).reshape(DEPTH, s, 6, d)
    x = _ln0(x.reshape(s * l, d), p["ln0_g"][None, :], p["ln0_b"][None, :], tm).reshape(s, l, d)
    for layer in range(DEPTH):
        lw = {k: v[layer] for k, v in pw.items() if k not in ("w_ada", "b_ada")}
        mod = mods[layer]
        qa, ka, va, qb, kb, vb, z, xbc, dt = _inproj(x, mod, lw, tabs, tm)
        oa = _attention(qa, ka, va, tq, "attn_a")
        ob = _attention(qb, kb, vb, tq, "attn_b")
        oc = _ssd(xbc, dt, z, lw, consts)
        x1, h2, route = _outproj(oa, ob, oc, x, mod, lw, tm)
        slots, gates, meta = _route_sort(route)
        x = _ln2(x1, _moe(h2, slots, gates, meta, lw), mod, lw, tm)
    return x


def kernel(x_prompt, x_sample, c_prompt, c_sample, ln0_g, ln0_b, w_ada, b_ada, w_in, qa_norm_g, ka_norm_g, oa_norm_g, qb_norm_g, w_uq, kvb_norm_g, w_ukv, ob_norm_g, conv_w, conv_b, dt_bias, a_log, d_skip, ssm_norm_g, w_out, ln1_g, ln1_b, w_rg, b_rg, w_re, b_re, w_e_gate, w_e_up, w_e_down, ln2_g, ln2_b):
    p = dict(ln0_g=ln0_g, ln0_b=ln0_b, w_ada=w_ada, b_ada=b_ada, w_in=w_in,
             qa_norm_g=qa_norm_g, ka_norm_g=ka_norm_g, oa_norm_g=oa_norm_g,
             qb_norm_g=qb_norm_g, w_uq=w_uq, kvb_norm_g=kvb_norm_g, w_ukv=w_ukv, ob_norm_g=ob_norm_g,
             conv_w=conv_w, conv_b=conv_b, dt_bias=dt_bias, a_log=a_log, d_skip=d_skip,
             ssm_norm_g=ssm_norm_g, w_out=w_out, ln1_g=ln1_g, ln1_b=ln1_b,
             w_rg=w_rg, b_rg=b_rg, w_re=w_re, b_re=b_re,
             w_e_gate=w_e_gate, w_e_up=w_e_up, w_e_down=w_e_down, ln2_g=ln2_g, ln2_b=ln2_b)
    nb = x_prompt.shape[0]
    y = _trunk(jnp.concatenate([x_prompt, x_sample], 0), jnp.concatenate([c_prompt, c_sample], 0), p)
    return (y[:nb], y[nb:])
```

```python
import functools

import jax
import jax.numpy as jnp
import numpy as np
from jax import lax
from jax.experimental import pallas as pl
from jax.experimental.pallas import tpu as pltpu

F32 = jnp.float32
BF16 = jnp.bfloat16

D_MODEL = 1024
DEPTH = 4
GRID_W = 64
ROPE_THETA = 10000.0
HEAD_DIM = 64
A_HEADS = 4
A_KV_HEADS = 2
B_HEADS = 4
Q_LORA = 384
KV_LORA = 128
MLA_NOPE = 64
MLA_ROPE = 32
MLA_V = 64
C_HEADS = 8
C_HEAD_DIM = 64
SSM_GROUPS = 2
D_STATE = 64
D_CONV = 5
C_INNER = C_HEADS * C_HEAD_DIM
CONV_DIM = C_INNER + 2 * SSM_GROUPS * D_STATE
N_EGROUPS = 4
EXPERTS_PER_GROUP = 8
N_EXPERTS = N_EGROUPS * EXPERTS_PER_GROUP
D_FF_EXPERT = 256
DN_ALPHA = (2 * DEPTH) ** 0.25
EPS = 1e-6

LANES = 128
SSD_Q = 128
NEG_BIG = -1e30

P_QA, P_KA, P_VA, P_CQ, P_CKV, P_KR, P_Z, P_XBC, P_DT, P_END = (
    0, 256, 384, 512, 896, 1024, 1536, 2048, 2816, 2944)


def _params(sem, vmem_mb):
    return pltpu.CompilerParams(dimension_semantics=sem, vmem_limit_bytes=vmem_mb * 1024 * 1024)


def _silu(x):
    return x * (1.0 / (1.0 + jnp.exp(-x)))


def _ln_rows(x, g, b):
    mu = jnp.mean(x, axis=-1, keepdims=True)
    xc = x - mu
    var = jnp.mean(xc * xc, axis=-1, keepdims=True)
    return xc * lax.rsqrt(var + EPS) * g + b


def _rms_rows(x, g):
    return x * lax.rsqrt(jnp.mean(x * x, axis=-1, keepdims=True) + EPS) * g


def _split_bf16(x):
    hi = x.astype(BF16)
    lo = (x - hi.astype(F32)).astype(BF16)
    return hi, lo


def _dot(a, b):
    return jnp.dot(a, b, preferred_element_type=F32)


def _pitch(rows):
    return rows + 8


def _rows_to_tiles(x, ct, out_ref):
    m = x.shape[0]
    nch = x.shape[1] // LANES
    p = _pitch(m)
    for j in range(nch):
        ct[j * p:j * p + m, :] = x[:, j * LANES:(j + 1) * LANES]

    def grp(g, c):
        for j in range(8):
            r = g * 8 + j
            out_ref[pl.ds(pl.multiple_of(r * 8, 8), 8), :] = ct[pl.ds(r, nch, stride=p), :]
        return c

    lax.fori_loop(0, m // 8, grp, 0)


def _tiles_to_rows(src_ref, row0, m, ct):
    nch = D_MODEL // LANES
    p = _pitch(m)

    def grp(g, c):
        for j in range(8):
            r = g * 8 + j
            ct[pl.ds(r, nch, stride=p), :] = src_ref[pl.ds(pl.multiple_of((row0 + r) * 8, 8), 8), :]
        return c

    lax.fori_loop(0, m // 8, grp, 0)
    return jnp.concatenate([ct[j * p:j * p + m, :] for j in range(nch)], axis=-1)


def _ln0_kernel(x_ref, g_ref, b_ref, o_ref):
    o_ref[...] = _ln_rows(x_ref[...], g_ref[...], b_ref[...])


def _ln0(x2d, g, b, tm):
    t, d = x2d.shape
    return pl.pallas_call(
        _ln0_kernel,
        out_shape=jax.ShapeDtypeStruct((t, d), F32),
        grid=(t // tm,),
        in_specs=[pl.BlockSpec((tm, d), lambda i: (i, 0)),
                  pl.BlockSpec((1, d), lambda i: (0, 0)),
                  pl.BlockSpec((1, d), lambda i: (0, 0))],
        out_specs=pl.BlockSpec((tm, d), lambda i: (i, 0)),
        compiler_params=_params(("parallel",), 32),
        name="ln0",
    )(x2d, g, b)


def _ada_kernel(c_ref, w_ref, b_ref, o_ref):
    cs = _silu(c_ref[...]).astype(BF16)
    o_ref[0] = _dot(cs, w_ref[0]) + b_ref[0]


def _ada(c, w_bf, b, tn=2048):
    s, d = c.shape
    depth, _, n = w_bf.shape
    return pl.pallas_call(
        _ada_kernel,
        out_shape=jax.ShapeDtypeStruct((depth, s, n), F32),
        grid=(depth, n // tn),
        in_specs=[pl.BlockSpec((s, d), lambda l, j: (0, 0)),
                  pl.BlockSpec((1, d, tn), lambda l, j: (l, 0, j)),
                  pl.BlockSpec((1, 1, tn), lambda l, j: (l, 0, j))],
        out_specs=pl.BlockSpec((1, s, tn), lambda l, j: (l, 0, j)),
        compiler_params=_params(("parallel", "parallel"), 32),
        name="ada",
    )(c, w_bf, b.reshape(depth, 1, n))


def _rope(x, c, p, m, shift):
    return x * c + pltpu.roll(x, shift, 1) * p + pltpu.roll(x, LANES - shift, 1) * m


def _headnorm2(x, g):
    xx = x * x
    lo = lax.broadcasted_iota(jnp.int32, x.shape, 1) < HEAD_DIM
    s0 = jnp.sum(jnp.where(lo, xx, 0.0), axis=-1, keepdims=True)
    s1 = jnp.sum(jnp.where(lo, 0.0, xx), axis=-1, keepdims=True)
    inv = jnp.where(lo, lax.rsqrt(s0 * (1.0 / HEAD_DIM) + EPS), lax.rsqrt(s1 * (1.0 / HEAD_DIM) + EPS))
    return x * inv * g


def _inproj_kernel(x_ref, mod_ref, w_ref, wuq_ref, wukk_ref, wukv_ref,
                   gqa_ref, gka_ref, gqb_ref, gkvb_ref,
                   ca_ref, pa_ref, ma_ref, cb_ref, pb_ref, mb_ref,
                   qa_o, ka_o, va_o, qb_o, kb_o, vb_o, z_o, xbc_o, dt_o):
    m = mod_ref[0]
    h = (x_ref[0] * (1.0 + m[1:2]) + m[0:1]).astype(BF16)

    def proj(a, b):
        return _dot(h, w_ref[:, a:b])

    ca, pa, ma = ca_ref[...], pa_ref[...], ma_ref[...]
    cb, pb, mb = cb_ref[...], pb_ref[...], mb_ref[...]
    half = HEAD_DIM // 2
    for ch in range(A_HEADS // 2):
        xa = _headnorm2(proj(P_QA + ch * LANES, P_QA + (ch + 1) * LANES), gqa_ref[...])
        xa = _rope(xa, ca, pa, ma, half // 2) * (HEAD_DIM ** -0.5)
        qa_o[0, 2 * ch] = xa[:, :HEAD_DIM].astype(BF16)
        qa_o[0, 2 * ch + 1] = xa[:, HEAD_DIM:].astype(BF16)
    xk = _rope(_headnorm2(proj(P_KA, P_VA), gka_ref[...]), ca, pa, ma, half // 2)
    ka_o[0, 0] = xk[:, :HEAD_DIM].astype(BF16)
    ka_o[0, 1] = xk[:, HEAD_DIM:].astype(BF16)
    xv = proj(P_VA, P_CQ)
    va_o[0, 0] = xv[:, :HEAD_DIM].astype(BF16)
    va_o[0, 1] = xv[:, HEAD_DIM:].astype(BF16)
    cqn = _rms_rows(proj(P_CQ, P_CKV), gqb_ref[...]).astype(BF16)
    qb = _dot(cqn, wuq_ref[...])
    ckvn = _rms_rows(proj(P_CKV, P_KR), gkvb_ref[...]).astype(BF16)
    kn = _dot(ckvn, wukk_ref[...])
    vv = _dot(ckvn, wukv_ref[...])
    krt = proj(P_KR, P_Z)
    scale_b = (MLA_NOPE + MLA_ROPE) ** -0.5
    for hh in range(B_HEADS):
        sl = slice(hh * LANES, (hh + 1) * LANES)
        qb_o[0, hh] = (_rope(qb[:, sl], cb, pb, mb, MLA_ROPE // 4) * scale_b).astype(BF16)
        kb_o[0, hh] = (kn[:, sl] + _rope(krt[:, sl], cb, pb, mb, MLA_ROPE // 4)).astype(BF16)
        vb_o[0, hh] = vv[:, hh * MLA_V:(hh + 1) * MLA_V].astype(BF16)
    z_o[0] = proj(P_Z, P_XBC)
    xbc_o[0] = proj(P_XBC, P_DT)
    dt_o[0] = proj(P_DT, P_END)


def _inproj(x, mod, lw, tabs, tm):
    s, l, d = x.shape
    nt = l // tm
    full = lambda shape: pl.BlockSpec(shape, lambda si, i: (0,) * len(shape))
    tab = pl.BlockSpec((tm, LANES), lambda si, i: (i, 0))
    head = lambda nh, dh: pl.BlockSpec((1, nh, tm, dh), lambda si, i: (si, 0, i, 0))
    row = lambda w: pl.BlockSpec((1, tm, w), lambda si, i: (si, i, 0))
    out_shape = (
        jax.ShapeDtypeStruct((s, A_HEADS, l, HEAD_DIM), BF16),
        jax.ShapeDtypeStruct((s, A_KV_HEADS, l, HEAD_DIM), BF16),
        jax.ShapeDtypeStruct((s, A_KV_HEADS, l, HEAD_DIM), BF16),
        jax.ShapeDtypeStruct((s, B_HEADS, l, LANES), BF16),
        jax.ShapeDtypeStruct((s, B_HEADS, l, LANES), BF16),
        jax.ShapeDtypeStruct((s, B_HEADS, l, MLA_V), BF16),
        jax.ShapeDtypeStruct((s, l, C_INNER), F32),
        jax.ShapeDtypeStruct((s, l, CONV_DIM), F32),
        jax.ShapeDtypeStruct((s, l, LANES), F32),
    )
    out_specs = (head(A_HEADS, HEAD_DIM), head(A_KV_HEADS, HEAD_DIM), head(A_KV_HEADS, HEAD_DIM),
                 head(B_HEADS, LANES), head(B_HEADS, LANES), head(B_HEADS, MLA_V),
                 row(C_INNER), row(CONV_DIM), row(LANES))
    return pl.pallas_call(
        _inproj_kernel,
        out_shape=out_shape,
        grid=(s, nt),
        in_specs=[row(d),
                  pl.BlockSpec((1, 6, d), lambda si, i: (si, 0, 0)),
                  full((d, P_END)), full((Q_LORA, B_HEADS * LANES)),
                  full((KV_LORA, B_HEADS * LANES)), full((KV_LORA, B_HEADS * MLA_V)),
                  full((1, LANES)), full((1, LANES)), full((1, Q_LORA)), full((1, KV_LORA)),
                  tab, tab, tab, tab, tab, tab],
        out_specs=out_specs,
        compiler_params=_params(("parallel", "parallel"), 48),
        name="inproj",
    )(x, mod, lw["w_in"], lw["w_uq"], lw["w_ukk"], lw["w_ukv"],
      lw["gqa"], lw["gka"], lw["gqb"], lw["gkvb"], *tabs)


def _attn_kernel(q_ref, k_ref, v_ref, o_ref, *, shared_kv):
    outs = []
    for j in range(2):
        kj = 0 if shared_kv else j
        q = q_ref[0, j]
        s = lax.dot_general(q, k_ref[0, kj], (((1,), (1,)), ((), ())), preferred_element_type=F32)
        mx = jnp.max(s, axis=-1, keepdims=True)
        p = jnp.exp(s - mx)
        den = jnp.sum(p, axis=-1, keepdims=True)
        outs.append(_dot(p.astype(BF16), v_ref[0, kj]) / den)
    o_ref[0] = jnp.concatenate(outs, axis=-1)


def _attention(q, k, v, tq, name):
    s, hq, l, dk = q.shape
    hk, dv = k.shape[1], v.shape[3]
    shared = hq // hk == 2
    kvb = 1 if shared else 2
    return pl.pallas_call(
        functools.partial(_attn_kernel, shared_kv=shared),
        out_shape=jax.ShapeDtypeStruct((s, l, hq * dv), F32),
        grid=(s, hq // 2, l // tq),
        in_specs=[pl.BlockSpec((1, 2, tq, dk), lambda si, hp, qi: (si, hp, qi, 0)),
                  pl.BlockSpec((1, kvb, l, dk), lambda si, hp, qi: (si, hp, 0, 0)),
                  pl.BlockSpec((1, kvb, l, dv), lambda si, hp, qi: (si, hp, 0, 0))],
        out_specs=pl.BlockSpec((1, tq, 2 * dv), lambda si, hp, qi: (si, qi, hp)),
        compiler_params=_params(("parallel", "parallel", "arbitrary"), 48),
        name=name,
    )(q, k, v)


def _ssd_kernel(xbc_ref, dt_ref, z_ref, cw_ref, cb_ref, dtb_ref, alog_ref, dskip_ref, g_ref,
                exf_ref, exb_ref, bd_ref, o_ref, xpad, xact, ad_s, hf, hb):
    l = xbc_ref.shape[1]
    nc = l // SSD_Q
    q = SSD_Q
    pad = 8
    xpad[0:pad, :] = jnp.zeros((pad, CONV_DIM), F32)
    xpad[l + pad:l + 2 * pad, :] = jnp.zeros((pad, CONV_DIM), F32)
    xpad[pad:l + pad, :] = xbc_ref[0]
    for c in range(nc):
        acc = jnp.zeros((q, CONV_DIM), F32) + cb_ref[...]
        for k in range(D_CONV):
            r = c * q + pad - D_CONV // 2 + k
            acc = acc + xpad[r:r + q, :] * cw_ref[k:k + 1, :]
        xact[c * q:(c + 1) * q, :] = _silu(acc)
    dtr = dt_ref[0] + dtb_ref[...]
    dt = jnp.maximum(dtr, 0.0) + jnp.log1p(jnp.exp(-jnp.abs(dtr)))
    lane_l = lax.broadcasted_iota(jnp.int32, dt.shape, 1)
    a = dt * (-jnp.exp(alog_ref[...]))
    ad_s[...] = jnp.where(lane_l < 2 * C_HEADS, a, pltpu.roll(dt, 2 * C_HEADS, 1))

    ri = lax.broadcasted_iota(jnp.int32, (q, q), 0)
    ci = lax.broadcasted_iota(jnp.int32, (q, q), 1)
    lower = ri >= ci
    upper = ci >= ri
    tril = jnp.where(lower, 1.0, 0.0).astype(BF16)
    lane_q = lax.broadcasted_iota(jnp.int32, (q, LANES), 1)
    first8 = lane_q < C_HEADS
    mid8 = (lane_q >= C_HEADS) & (lane_q < 2 * C_HEADS)
    lo64 = lane_q < C_HEAD_DIM
    bd = bd_ref[...] > 0.5

    def chunk_decay(r0):
        ad = ad_s[pl.ds(r0, q), :]
        hi, lo = _split_bf16(ad)
        inc = _dot(tril, hi) + _dot(tril, lo)
        exc = inc - ad
        tot = inc[q - 1:q, :]
        dts = pltpu.roll(ad, LANES - 2 * C_HEADS, 1)
        return ad, inc, exc, tot, dts

    def expand(v, ex_ref):
        hi, lo = _split_bf16(v)
        return _dot(hi, ex_ref[...]) + _dot(lo, ex_ref[...])

    hf[...] = jnp.zeros_like(hf)
    hb[...] = jnp.zeros_like(hb)

    def fwd_body(c, carry):
        r0 = pl.multiple_of(c * q, q)
        ad, inc, exc, tot, dts = chunk_decay(r0)
        inc_t = inc.T
        ad_t = ad.T
        exc_t = inc_t - ad_t
        xq = xact[pl.ds(r0, q), :]
        xs = xq[:, :C_INNER]
        xs_bf = xs.astype(BF16)
        bm = xq[:, C_INNER:C_INNER + LANES]
        cm = xq[:, C_INNER + LANES:]
        bt_bf = bm.T.astype(BF16)
        cm_bf = cm.astype(BF16)
        gmat = []
        for g in range(SSM_GROUPS):
            cg = jnp.where(lo64 if g == 0 else jnp.logical_not(lo64), cm, 0.0).astype(BF16)
            gmat.append(_dot(cg, bt_bf))
        for pr in range(C_HEADS // 2):
            ys = []
            for hh in (2 * pr, 2 * pr + 1):
                icol = jnp.broadcast_to(inc[:, hh:hh + 1], (q, q))
                irow = jnp.broadcast_to(inc_t[hh:hh + 1, :], (q, q))
                lf = jnp.exp(jnp.where(lower, icol - irow, NEG_BIG)) * ad_t[16 + hh:17 + hh, :]
                ecol = jnp.broadcast_to(exc[:, 8 + hh:9 + hh], (q, q))
                erow = jnp.broadcast_to(exc_t[8 + hh:9 + hh, :], (q, q))
                lb = jnp.exp(jnp.where(upper, erow - ecol, NEG_BIG)) * ad_t[24 + hh:25 + hh, :]
                w = (gmat[hh // (C_HEADS // SSM_GROUPS)] * (lf + lb)).astype(BF16)
                ys.append(_dot(w, xs_bf[:, pr * LANES:(pr + 1) * LANES]))
            o_ref[0, pl.ds(r0, q), pr * LANES:(pr + 1) * LANES] = jnp.where(lo64, ys[0], ys[1])
        pf = expand(jnp.where(first8, jnp.exp(inc), 0.0), exf_ref)
        sf = expand(jnp.where(first8, jnp.exp(tot - inc) * dts, 0.0), exf_ref)
        y_off = _dot(cm_bf, hf[...].astype(BF16)) * pf
        o_ref[0, pl.ds(r0, q), :] = o_ref[0, pl.ds(r0, q), :] + y_off
        st = _dot(bt_bf, (xs * sf).astype(BF16))
        hf[...] = pf[q - 1:q, :] * hf[...] + jnp.where(bd, st, 0.0)
        return carry

    lax.fori_loop(0, nc, fwd_body, 0)

    def bwd_body(i, carry):
        c = nc - 1 - i
        r0 = pl.multiple_of(c * q, q)
        ad, inc, exc, tot, dts = chunk_decay(r0)
        xq = xact[pl.ds(r0, q), :]
        xs = xq[:, :C_INNER]
        bt_bf = xq[:, C_INNER:C_INNER + LANES].T.astype(BF16)
        cm_bf = xq[:, C_INNER + LANES:].astype(BF16)
        pbk = expand(jnp.where(mid8, jnp.exp(tot - exc), 0.0), exb_ref)
        sbk = expand(jnp.where(mid8, jnp.exp(exc) * dts, 0.0), exb_ref)
        y = o_ref[0, pl.ds(r0, q), :] + _dot(cm_bf, hb[...].astype(BF16)) * pbk
        st = _dot(bt_bf, (xs * sbk).astype(BF16))
        hb[...] = pbk[0:1, :] * hb[...] + jnp.where(bd, st, 0.0)
        y = (y + dskip_ref[...] * xs) * _silu(z_ref[0, pl.ds(r0, q), :])
        o_ref[0, pl.ds(r0, q), :] = _rms_rows(y, g_ref[...])
        return carry

    lax.fori_loop(0, nc, bwd_body, 0)


def _ssd(xbc, dt, z, lw, consts):
    s, l, _ = xbc.shape
    seq = lambda w: pl.BlockSpec((1, l, w), lambda si: (si, 0, 0))
    full = lambda shape: pl.BlockSpec(shape, lambda si: (0,) * len(shape))
    return pl.pallas_call(
        _ssd_kernel,
        out_shape=jax.ShapeDtypeStruct((s, l, C_INNER), F32),
        grid=(s,),
        in_specs=[seq(CONV_DIM), seq(LANES), seq(C_INNER),
                  full((D_CONV, CONV_DIM)), full((1, CONV_DIM)), full((1, LANES)), full((1, LANES)),
                  full((1, C_INNER)), full((1, C_INNER)),
                  full((LANES, C_INNER)), full((LANES, C_INNER)), full((LANES, C_INNER))],
        out_specs=seq(C_INNER),
        scratch_shapes=[pltpu.VMEM((l + 16, CONV_DIM), F32), pltpu.VMEM((l, CONV_DIM), F32),
                        pltpu.VMEM((l, LANES), F32),
                        pltpu.VMEM((LANES, C_INNER), F32), pltpu.VMEM((LANES, C_INNER), F32)],
        compiler_params=_params(("parallel",), 56),
        name="ssd",
    )(xbc, dt, z, lw["conv_w"], lw["conv_b"], lw["dt_bias"], lw["a_log"], lw["d_skip"], lw["ssm_g"],
      consts["exf"], consts["exb"], consts["bd"])


def _route(logits):
    lane = lax.broadcasted_iota(jnp.int32, logits.shape, 1)
    lane_f = lane.astype(F32)
    neg_inf = -jnp.inf
    isg = (lane >= N_EXPERTS) & (lane < N_EXPERTS + N_EGROUPS)
    gl = jnp.where(isg, logits, neg_inf)
    gmax = jnp.max(gl, axis=-1, keepdims=True)
    gsel = jnp.min(jnp.where(gl == gmax, lane_f - N_EXPERTS, 1e9), axis=-1, keepdims=True)
    gw = 1.0 / jnp.sum(jnp.where(isg, jnp.exp(gl - gmax), 0.0), axis=-1, keepdims=True)
    grp = jnp.floor(lane_f * (1.0 / EXPERTS_PER_GROUP))
    el = jnp.where((lane < N_EXPERTS) & (grp == gsel), logits, neg_inf)
    l1 = jnp.max(el, axis=-1, keepdims=True)
    i1 = jnp.min(jnp.where(el == l1, lane_f, 1e9), axis=-1, keepdims=True)
    el2 = jnp.where(lane_f == i1, neg_inf, el)
    l2 = jnp.max(el2, axis=-1, keepdims=True)
    i2 = jnp.min(jnp.where(el2 == l2, lane_f, 1e9), axis=-1, keepdims=True)
    r = jnp.exp(l2 - l1)
    g1 = gw / (1.0 + r)
    g2 = g1 * r
    return jnp.where(lane == 0, i1, jnp.where(lane == 1, i2, jnp.where(lane == 2, g1,
                                                                    jnp.where(lane == 3, g2, 0.0))))


def _outproj_kernel(oa_ref, ob_ref, oc_ref, x_ref, mod_ref, wout_ref, goa_ref, gob_ref,
                    l1g_ref, l1b_ref, wrh_ref, wrl_ref, br_ref, x1_o, h2v_o, route_o, ct):
    oa = _rms_rows(oa_ref[0], goa_ref[...]).astype(BF16)
    ob = _rms_rows(ob_ref[0], gob_ref[...]).astype(BF16)
    oc = oc_ref[0].astype(BF16)
    wa = A_HEADS * HEAD_DIM
    wb = wa + B_HEADS * MLA_V
    y = _dot(oa, wout_ref[0:wa, :]) + _dot(ob, wout_ref[wa:wb, :]) + _dot(oc, wout_ref[wb:, :])
    m = mod_ref[0]
    x1 = _ln_rows(DN_ALPHA * x_ref[0] + (1.0 + m[2:3]) * y, l1g_ref[...], l1b_ref[...])
    x1_o[0] = x1
    h2 = x1 * (1.0 + m[4:5]) + m[3:4]
    _rows_to_tiles(h2, ct, h2v_o)
    hi, lo = _split_bf16(h2)
    logits = _dot(hi, wrh_ref[...]) + _dot(lo, wrh_ref[...]) + _dot(hi, wrl_ref[...]) + br_ref[...]
    route_o[0] = _route(logits)


def _outproj(oa, ob, oc, x, mod, lw, tm):
    s, l, d = x.shape
    nt = l // tm
    nch = d // LANES
    row = lambda w: pl.BlockSpec((1, tm, w), lambda si, i: (si, i, 0))
    full = lambda shape: pl.BlockSpec(shape, lambda si, i: (0,) * len(shape))
    return pl.pallas_call(
        _outproj_kernel,
        out_shape=(jax.ShapeDtypeStruct((s, l, d), F32), jax.ShapeDtypeStruct((s * l * nch, LANES), F32),
                   jax.ShapeDtypeStruct((s, l, LANES), F32)),
        grid=(s, l // tm),
        in_specs=[row(A_HEADS * HEAD_DIM), row(B_HEADS * MLA_V), row(C_INNER), row(d),
                  pl.BlockSpec((1, 6, d), lambda si, i: (si, 0, 0)),
                  full((d, d)), full((1, A_HEADS * HEAD_DIM)), full((1, B_HEADS * MLA_V)),
                  full((1, d)), full((1, d)), full((d, LANES)), full((d, LANES)), full((1, LANES))],
        out_specs=(row(d), pl.BlockSpec((tm * nch, LANES), lambda si, i: (si * nt + i, 0)), row(LANES)),
        scratch_shapes=[pltpu.VMEM((nch * _pitch(tm), LANES), F32)],
        compiler_params=_params(("parallel", "parallel"), 48),
        name="outproj",
    )(oa, ob, oc, x, mod, lw["w_out"], lw["goa"], lw["gob"], lw["ln1_g"], lw["ln1_b"],
      lw["w_r_hi"], lw["w_r_lo"], lw["b_r"])


MOE_ROWS = 128
ROW_STRIDE = _pitch(MOE_ROWS)


def _slot_rows(l):
    return 2 * l + N_EXPERTS * 8


def _route_sort_kernel(route_ref, slots_o, gates_o, meta_o):
    r = route_ref[0]
    l = r.shape[0]
    lane = lax.broadcasted_iota(jnp.int32, (l, LANES), 1)
    lane_f = lane.astype(F32)
    oh1 = lane_f == jnp.broadcast_to(r[:, 0:1], (l, LANES))
    oh2 = lane_f == jnp.broadcast_to(r[:, 1:2], (l, LANES))
    member = jnp.where(oh1 | oh2, 1.0, 0.0).astype(BF16)
    tb = min(256, l)
    ri = lax.broadcasted_iota(jnp.int32, (tb, tb), 0)
    ci = lax.broadcasted_iota(jnp.int32, (tb, tb), 1)
    stril = jnp.where(ri > ci, 1.0, 0.0).astype(BF16)
    run = jnp.zeros((1, LANES), F32)
    ranks = []
    for i in range(l // tb):
        blk = member[i * tb:(i + 1) * tb]
        ranks.append(_dot(stril, blk) + run)
        run = run + jnp.sum(blk.astype(F32), axis=0, keepdims=True)
    rank_all = jnp.concatenate(ranks, axis=0)
    padded = jnp.floor((run + 7.0) * 0.125) * 8.0
    hi, lo = _split_bf16(jnp.broadcast_to(padded, (8, LANES)))
    rl = lax.broadcasted_iota(jnp.int32, (LANES, LANES), 0)
    cl = lax.broadcasted_iota(jnp.int32, (LANES, LANES), 1)
    upper = jnp.where(rl < cl, 1.0, 0.0).astype(BF16)
    base = (_dot(hi, upper) + _dot(lo, upper))[0:1]
    pos = base + rank_all
    slot1 = jnp.sum(jnp.where(oh1, pos, 0.0), axis=-1, keepdims=True)
    slot2 = jnp.sum(jnp.where(oh2, pos, 0.0), axis=-1, keepdims=True)
    packed = jnp.where(lane == 0, slot1, jnp.where(lane == 1, slot2, jnp.where(lane < 4, r, 0.0)))
    pt = packed.T
    slots_o[0] = pt[0:2].astype(jnp.int32)
    gates_o[0] = pt[2:4]
    meta_o[0] = jnp.concatenate([base, run, jnp.zeros((6, LANES), F32)], axis=0).astype(jnp.int32)


def _route_sort(route):
    s, l, _ = route.shape
    return pl.pallas_call(
        _route_sort_kernel,
        out_shape=(jax.ShapeDtypeStruct((s, 2, l), jnp.int32), jax.ShapeDtypeStruct((s, 2, l), F32),
                   jax.ShapeDtypeStruct((s, 8, LANES), jnp.int32)),
        grid=(s,),
        in_specs=[pl.BlockSpec((1, l, LANES), lambda si: (si, 0, 0))],
        out_specs=(pl.BlockSpec((1, 2, l), lambda si: (si, 0, 0)), pl.BlockSpec((1, 2, l), lambda si: (si, 0, 0)),
                   pl.BlockSpec((1, 8, LANES), lambda si: (si, 0, 0))),
        compiler_params=_params(("parallel",), 32),
        name="route_sort",
    )(route)


def _moe_kernel(meta_ref, h2v_ref, slots_hbm, gates_hbm, wg_ref, wu_ref, wd_ref,
                x1_ref, mod_ref, l2g_ref, l2b_ref, o_ref,
                slot_s, gate_s, tok_s, gat_s, acc_v, xt, yt, ct, sem):
    s = pl.program_id(0)
    e = pl.program_id(1)
    l = h2v_ref.shape[0] // 8
    nch = D_MODEL // LANES

    @pl.when(e == 0)
    def _():
        c_slots = pltpu.make_async_copy(slots_hbm.at[s], slot_s, sem.at[0])
        c_gates = pltpu.make_async_copy(gates_hbm.at[s], gate_s, sem.at[1])
        c_slots.start()
        c_gates.start()
        acc_v[...] = jnp.zeros_like(acc_v)
        xt[...] = jnp.zeros_like(xt)

        def clear(ex, c):
            n = meta_ref[s, 1, ex]
            last = meta_ref[s, 0, ex] + ((n + 7) // 8) * 8 - 8

            @pl.when(n > 0)
            def _():
                for j in range(8):
                    tok_s[last + j] = l
                    gat_s[last + j] = 0.0

            return c

        lax.fori_loop(0, N_EXPERTS, clear, 0)
        c_slots.wait()
        c_gates.wait()

        def invert(g, c):
            for j in range(8):
                t = g * 8 + j
                for k in range(2):
                    sl = slot_s[k, t]
                    tok_s[sl] = t
                    gat_s[sl] = gate_s[k, t]
            return c

        lax.fori_loop(0, l // 8, invert, 0)

    base = meta_ref[s, 0, e]
    ngrp = (meta_ref[s, 1, e] + 7) // 8
    grp_per_blk = MOE_ROWS // 8

    def block(blk, c):
        g_lo = blk * grp_per_blk
        ng = jnp.minimum(grp_per_blk, ngrp - g_lo)

        def gather(g, cc):
            r0 = base + (g_lo + g) * 8
            for j in range(8):
                t = jnp.minimum(tok_s[r0 + j], l - 1)
                row = h2v_ref[pl.ds(pl.multiple_of(t * 8, 8), 8), :]
                xt[pl.ds(g * 8 + j, nch, stride=ROW_STRIDE), :] = row
            return cc

        lax.fori_loop(0, ng, gather, 0)
        xb = jnp.concatenate([xt[j * ROW_STRIDE:j * ROW_STRIDE + MOE_ROWS, :] for j in range(nch)],
                             axis=-1).astype(BF16)
        mid = (_silu(_dot(xb, wg_ref[0])) * _dot(xb, wu_ref[0])).astype(BF16)
        y = _dot(mid, wd_ref[0])
        for j in range(nch):
            yt[j * ROW_STRIDE:j * ROW_STRIDE + MOE_ROWS, :] = y[:, j * LANES:(j + 1) * LANES]

        def combine(g, cc):
            r0 = base + (g_lo + g) * 8
            toks = [pl.multiple_of(tok_s[r0 + j] * 8, 8) for j in range(8)]
            vals = [acc_v[pl.ds(toks[j], 8), :]
                    + gat_s[r0 + j] * yt[pl.ds(g * 8 + j, nch, stride=ROW_STRIDE), :] for j in range(8)]
            for j in range(8):
                acc_v[pl.ds(toks[j], 8), :] = vals[j]
            return cc

        lax.fori_loop(0, ng, combine, 0)
        return c

    lax.fori_loop(0, (ngrp + grp_per_blk - 1) // grp_per_blk, block, 0)

    @pl.when(e >= N_EXPERTS)
    def _():
        tmo = x1_ref.shape[1]
        y = _tiles_to_rows(acc_v, (e - N_EXPERTS) * tmo, tmo, ct)
        m = mod_ref[0]
        o_ref[0] = _ln_rows(DN_ALPHA * x1_ref[0] + (1.0 + m[5:6]) * y, l2g_ref[...], l2b_ref[...])


def _moe(h2v, slots, gates, meta, x1, mod, lw, tm):
    s, l, d = x1.shape
    n_slots = _slot_rows(l)
    last = N_EXPERTS - 1
    wspec = lambda shape: pl.BlockSpec(shape, lambda si, e: (jnp.minimum(e, last), 0, 0))
    tok_blk = pl.BlockSpec((1, tm, d), lambda si, e: (si, jnp.maximum(e - N_EXPERTS, 0), 0))
    vec = pl.BlockSpec((1, d), lambda si, e: (0, 0))
    return pl.pallas_call(
        _moe_kernel,
        out_shape=jax.ShapeDtypeStruct((s, l, d), F32),
        grid=(s, N_EXPERTS + l // tm),
        in_specs=[pl.BlockSpec(memory_space=pltpu.SMEM),
                  pl.BlockSpec((l * 8, LANES), lambda si, e: (si, 0)),
                  pl.BlockSpec(memory_space=pl.ANY), pl.BlockSpec(memory_space=pl.ANY),
                  wspec((1, d, D_FF_EXPERT)), wspec((1, d, D_FF_EXPERT)), wspec((1, D_FF_EXPERT, d)),
                  tok_blk, pl.BlockSpec((1, 6, d), lambda si, e: (si, 0, 0)), vec, vec],
        out_specs=tok_blk,
        scratch_shapes=[pltpu.SMEM((2, l), jnp.int32), pltpu.SMEM((2, l), F32),
                        pltpu.SMEM((n_slots,), jnp.int32), pltpu.SMEM((n_slots,), F32),
                        pltpu.VMEM(((l + 1) * 8, LANES), F32),
                        pltpu.VMEM((8 * ROW_STRIDE, LANES), F32), pltpu.VMEM((8 * ROW_STRIDE, LANES), F32),
                        pltpu.VMEM((8 * _pitch(tm), LANES), F32),
                        pltpu.SemaphoreType.DMA((2,))],
        compiler_params=_params(("parallel", "arbitrary"), 56),
        name="moe",
    )(meta, h2v, slots, gates, lw["w_e_gate"], lw["w_e_up"], lw["w_e_down"], x1, mod,
      lw["ln2_g"], lw["ln2_b"])


def _rope_tables(seq_len):
    t = jnp.arange(seq_len, dtype=jnp.int32)
    row = (t // GRID_W).astype(F32)[:, None]
    col = (t % GRID_W).astype(F32)[:, None]

    def axis_tables(d_axis):
        inv = ROPE_THETA ** (-jnp.arange(0, d_axis, 2, dtype=F32) / d_axis)
        zero = jnp.zeros((seq_len, d_axis // 2), F32)
        cs, ps, ms = [], [], []
        for pos in (row, col):
            ang = pos * inv
            c, sn = jnp.cos(ang), jnp.sin(ang)
            cs += [c, c]
            ps += [zero, sn]
            ms += [-sn, zero]
        return [jnp.concatenate(v, axis=-1) for v in (cs, ps, ms)]

    ta = [jnp.tile(v, (1, 2)) for v in axis_tables(HEAD_DIM // 2)]
    one = jnp.ones((seq_len, MLA_NOPE), F32)
    zero = jnp.zeros((seq_len, MLA_NOPE), F32)
    pad1 = jnp.ones((seq_len, LANES - MLA_NOPE - MLA_ROPE), F32)
    pad0 = jnp.zeros((seq_len, LANES - MLA_NOPE - MLA_ROPE), F32)
    cb, pb, mb = axis_tables(MLA_ROPE // 2)
    tb = [jnp.concatenate([one, cb, pad1], -1), jnp.concatenate([zero, pb, pad0], -1),
          jnp.concatenate([zero, mb, pad0], -1)]
    return ta + tb


def _ssd_consts():
    r = np.arange(LANES)[:, None]
    c = np.arange(C_INNER)[None, :]
    exf = ((r < C_HEADS) & (c // C_HEAD_DIM == r)).astype(np.float32)
    exb = ((r >= C_HEADS) & (r < 2 * C_HEADS) & (c // C_HEAD_DIM == r - C_HEADS)).astype(np.float32)
    heads_per_group = C_HEADS // SSM_GROUPS
    bd = (r // D_STATE == c // (C_HEAD_DIM * heads_per_group)).astype(np.float32)
    return {"exf": jnp.asarray(exf, BF16), "exb": jnp.asarray(exb, BF16), "bd": jnp.asarray(bd, F32)}


def _pack_weights(p):
    w_in = p["w_in"]
    depth = w_in.shape[0]
    off = np.cumsum((0, 256, 128, 128, Q_LORA, KV_LORA, MLA_ROPE, C_INNER, CONV_DIM, 2 * C_HEADS))
    w_kr = w_in[:, :, off[5]:off[6]]
    zpad = lambda n: jnp.zeros((depth, D_MODEL, n), F32)
    kr_t = jnp.concatenate(
        [jnp.concatenate([zpad(MLA_NOPE), w_kr, zpad(LANES - MLA_NOPE - MLA_ROPE)], -1)] * B_HEADS, -1)
    w_in_p = jnp.concatenate([w_in[:, :, :off[5]], kr_t, w_in[:, :, off[6]:off[8]],
                              w_in[:, :, off[8]:], zpad(LANES - 2 * C_HEADS)], -1).astype(BF16)
    dq = MLA_NOPE + MLA_ROPE
    w_uq = p["w_uq"].reshape(depth, Q_LORA, B_HEADS, dq)
    w_uq_p = jnp.concatenate([w_uq, jnp.zeros((depth, Q_LORA, B_HEADS, LANES - dq), F32)], -1)
    w_ukv = p["w_ukv"].reshape(depth, KV_LORA, B_HEADS, MLA_NOPE + MLA_V)
    w_ukk = jnp.concatenate([w_ukv[..., :MLA_NOPE],
                             jnp.zeros((depth, KV_LORA, B_HEADS, LANES - MLA_NOPE), F32)], -1)
    pad_lanes = lambda v: jnp.concatenate([v, jnp.zeros(v.shape[:-1] + (LANES - v.shape[-1],), F32)], -1)
    w_r = pad_lanes(jnp.concatenate([p["w_re"], p["w_rg"]], -1))
    w_r_hi = w_r.astype(BF16)
    return {
        "w_ada": p["w_ada"].astype(BF16), "b_ada": p["b_ada"],
        "w_in": w_in_p,
        "w_uq": w_uq_p.reshape(depth, Q_LORA, B_HEADS * LANES).astype(BF16),
        "w_ukk": w_ukk.reshape(depth, KV_LORA, B_HEADS * LANES).astype(BF16),
        "w_ukv": w_ukv[..., MLA_NOPE:].reshape(depth, KV_LORA, B_HEADS * MLA_V).astype(BF16),
        "gqa": jnp.tile(p["qa_norm_g"], (1, 2))[:, None, :], "gka": jnp.tile(p["ka_norm_g"], (1, 2))[:, None, :],
        "gqb": p["qb_norm_g"][:, None, :], "gkvb": p["kvb_norm_g"][:, None, :],
        "goa": p["oa_norm_g"][:, None, :], "gob": p["ob_norm_g"][:, None, :],
        "conv_w": p["conv_w"], "conv_b": p["conv_b"][:, None, :],
        "dt_bias": pad_lanes(p["dt_bias"].reshape(depth, 2 * C_HEADS))[:, None, :],
        "a_log": pad_lanes(p["a_log"].reshape(depth, 2 * C_HEADS))[:, None, :],
        "d_skip": jnp.repeat(p["d_skip"], C_HEAD_DIM, axis=-1)[:, None, :],
        "ssm_g": p["ssm_norm_g"][:, None, :],
        "w_out": p["w_out"].astype(BF16),
        "ln1_g": p["ln1_g"][:, None, :], "ln1_b": p["ln1_b"][:, None, :],
        "w_r_hi": w_r_hi, "w_r_lo": (w_r - w_r_hi.astype(F32)).astype(BF16),
        "b_r": pad_lanes(jnp.concatenate([p["b_re"], p["b_rg"]], -1))[:, None, :],
        "w_e_gate": p["w_e_gate"].astype(BF16), "w_e_up": p["w_e_up"].astype(BF16),
        "w_e_down": p["w_e_down"].astype(BF16),
        "ln2_g": p["ln2_g"][:, None, :], "ln2_b": p["ln2_b"][:, None, :],
    }


def _trunk(x, c, p, tm=512, tq=256):
    s, l, d = x.shape
    tm = min(tm, l)
    tq = min(tq, l)
    pw = _pack_weights(p)
    tabs = _rope_tables(l)
    consts = _ssd_consts()
    mods = _ada(c, pw["w_ada"], pw["b_ada"]).reshape(DEPTH, s, 6, d)
    x = _ln0(x.reshape(s * l, d), p["ln0_g"][None, :], p["ln0_b"][None, :], tm).reshape(s, l, d)
    for layer in range(DEPTH):
        lw = {k: v[layer] for k, v in pw.items() if k not in ("w_ada", "b_ada")}
        mod = mods[layer]
        qa, ka, va, qb, kb, vb, z, xbc, dt = _inproj(x, mod, lw, tabs, tm)
        oa = _attention(qa, ka, va, tq, "attn_a")
        ob = _attention(qb, kb, vb, tq, "attn_b")
        oc = _ssd(xbc, dt, z, lw, consts)
        x1, h2v, route = _outproj(oa, ob, oc, x, mod, lw, tm)
        slots, gates, meta = _route_sort(route)
        x = _moe(h2v, slots, gates, meta, x1, mod, lw, tm)
    return x


def kernel(x_prompt, x_sample, c_prompt, c_sample, ln0_g, ln0_b, w_ada, b_ada, w_in, qa_norm_g, ka_norm_g, oa_norm_g, qb_norm_g, w_uq, kvb_norm_g, w_ukv, ob_norm_g, conv_w, conv_b, dt_bias, a_log, d_skip, ssm_norm_g, w_out, ln1_g, ln1_b, w_rg, b_rg, w_re, b_re, w_e_gate, w_e_up, w_e_down, ln2_g, ln2_b):
    p = dict(ln0_g=ln0_g, ln0_b=ln0_b, w_ada=w_ada, b_ada=b_ada, w_in=w_in,
             qa_norm_g=qa_norm_g, ka_norm_g=ka_norm_g, oa_norm_g=oa_norm_g,
             qb_norm_g=qb_norm_g, w_uq=w_uq, kvb_norm_g=kvb_norm_g, w_ukv=w_ukv, ob_norm_g=ob_norm_g,
             conv_w=conv_w, conv_b=conv_b, dt_bias=dt_bias, a_log=a_log, d_skip=d_skip,
             ssm_norm_g=ssm_norm_g, w_out=w_out, ln1_g=ln1_g, ln1_b=ln1_b,
             w_rg=w_rg, b_rg=b_rg, w_re=w_re, b_re=b_re,
             w_e_gate=w_e_gate, w_e_up=w_e_up, w_e_down=w_e_down, ln2_g=ln2_g, ln2_b=ln2_b)
    nb = x_prompt.shape[0]
    y = _trunk(jnp.concatenate([x_prompt, x_sample], 0), jnp.concatenate([c_prompt, c_sample], 0), p)
    return (y[:nb], y[nb:])
```

```python
import functools

import jax
import jax.numpy as jnp
import numpy as np
from jax import lax
from jax.experimental import pallas as pl
from jax.experimental.pallas import tpu as pltpu

F32 = jnp.float32
BF16 = jnp.bfloat16

D_MODEL = 1024
DEPTH = 4
GRID_W = 64
ROPE_THETA = 10000.0
HEAD_DIM = 64
A_HEADS = 4
A_KV_HEADS = 2
B_HEADS = 4
Q_LORA = 384
KV_LORA = 128
MLA_NOPE = 64
MLA_ROPE = 32
MLA_V = 64
C_HEADS = 8
C_HEAD_DIM = 64
SSM_GROUPS = 2
D_STATE = 64
D_CONV = 5
C_INNER = C_HEADS * C_HEAD_DIM
CONV_DIM = C_INNER + 2 * SSM_GROUPS * D_STATE
N_EGROUPS = 4
EXPERTS_PER_GROUP = 8
N_EXPERTS = N_EGROUPS * EXPERTS_PER_GROUP
D_FF_EXPERT = 256
DN_ALPHA = (2 * DEPTH) ** 0.25
EPS = 1e-6

LANES = 128
SSD_Q = 128
NEG_BIG = -1e30

P_QA, P_KA, P_VA, P_CQ, P_CKV, P_Z, P_XBC, P_DTKR, P_END = (
    0, 256, 384, 512, 896, 1024, 1536, 2304, 2432)
KR_LANE = MLA_NOPE


def _params(sem, vmem_mb):
    return pltpu.CompilerParams(dimension_semantics=sem, vmem_limit_bytes=vmem_mb * 1024 * 1024)


def _silu(x):
    return x * (1.0 / (1.0 + jnp.exp(-x)))


def _ln_rows(x, g, b):
    mu = jnp.mean(x, axis=-1, keepdims=True)
    xc = x - mu
    var = jnp.mean(xc * xc, axis=-1, keepdims=True)
    return xc * lax.rsqrt(var + EPS) * g + b


def _rms_rows(x, g):
    return x * lax.rsqrt(jnp.mean(x * x, axis=-1, keepdims=True) + EPS) * g


def _split_bf16(x):
    hi = x.astype(BF16)
    lo = (x - hi.astype(F32)).astype(BF16)
    return hi, lo


def _dot(a, b):
    return jnp.dot(a, b, preferred_element_type=F32)


def _dot_nt(a, b):
    return lax.dot_general(a, b, (((1,), (1,)), ((), ())), preferred_element_type=F32)


def _pitch(rows):
    return rows + 8


def _rows_to_tiles(x, ct, out_ref):
    m = x.shape[0]
    nch = x.shape[1] // LANES
    p = _pitch(m)
    for j in range(nch):
        ct[j * p:j * p + m, :] = x[:, j * LANES:(j + 1) * LANES]

    def grp(g, c):
        for j in range(8):
            r = g * 8 + j
            out_ref[pl.ds(pl.multiple_of(r * 8, 8), 8), :] = ct[pl.ds(r, nch, stride=p), :]
        return c

    lax.fori_loop(0, m // 8, grp, 0)


def _tiles_to_rows(src_ref, row0, m, ct):
    nch = D_MODEL // LANES
    p = _pitch(m)

    def grp(g, c):
        for j in range(8):
            r = g * 8 + j
            ct[pl.ds(r, nch, stride=p), :] = src_ref[pl.ds(pl.multiple_of((row0 + r) * 8, 8), 8), :]
        return c

    lax.fori_loop(0, m // 8, grp, 0)
    return jnp.concatenate([ct[j * p:j * p + m, :] for j in range(nch)], axis=-1)


def _ln0_kernel(x_ref, g_ref, b_ref, o_ref):
    o_ref[...] = _ln_rows(x_ref[...], g_ref[...], b_ref[...])


def _ln0(x2d, g, b, tm):
    t, d = x2d.shape
    return pl.pallas_call(
        _ln0_kernel,
        out_shape=jax.ShapeDtypeStruct((t, d), F32),
        grid=(t // tm,),
        in_specs=[pl.BlockSpec((tm, d), lambda i: (i, 0)),
                  pl.BlockSpec((1, d), lambda i: (0, 0)),
                  pl.BlockSpec((1, d), lambda i: (0, 0))],
        out_specs=pl.BlockSpec((tm, d), lambda i: (i, 0)),
        compiler_params=_params(("parallel",), 32),
        name="ln0",
    )(x2d, g, b)


def _ada_kernel(c_ref, w_ref, b_ref, o_ref):
    cs = _silu(c_ref[...]).astype(BF16)
    o_ref[0] = _dot(cs, w_ref[0]) + b_ref[0]


def _ada(c, w_bf, b, tn=2048):
    s, d = c.shape
    depth, _, n = w_bf.shape
    return pl.pallas_call(
        _ada_kernel,
        out_shape=jax.ShapeDtypeStruct((depth, s, n), F32),
        grid=(depth, n // tn),
        in_specs=[pl.BlockSpec((s, d), lambda l, j: (0, 0)),
                  pl.BlockSpec((1, d, tn), lambda l, j: (l, 0, j)),
                  pl.BlockSpec((1, 1, tn), lambda l, j: (l, 0, j))],
        out_specs=pl.BlockSpec((1, s, tn), lambda l, j: (l, 0, j)),
        compiler_params=_params(("parallel", "parallel"), 32),
        name="ada",
    )(c, w_bf, b.reshape(depth, 1, n))


def _rope(x, c, p, m, shift):
    return x * c + pltpu.roll(x, shift, 1) * p + pltpu.roll(x, LANES - shift, 1) * m


def _headnorm2(x, g):
    xx = x * x
    lo = lax.broadcasted_iota(jnp.int32, x.shape, 1) < HEAD_DIM
    s0 = jnp.sum(jnp.where(lo, xx, 0.0), axis=-1, keepdims=True)
    s1 = jnp.sum(jnp.where(lo, 0.0, xx), axis=-1, keepdims=True)
    inv = jnp.where(lo, lax.rsqrt(s0 * (1.0 / HEAD_DIM) + EPS), lax.rsqrt(s1 * (1.0 / HEAD_DIM) + EPS))
    return x * inv * g


def _store_values(v2, v_o, h0):
    lo = lax.broadcasted_iota(jnp.int32, v2.shape, 1) < ATT_DV
    v_o[0, h0] = jnp.where(lo, v2, 1.0).astype(BF16)
    v_o[0, h0 + 1] = jnp.where(lo, pltpu.roll(v2, ATT_DV, 1), 1.0).astype(BF16)


def _inproj_kernel(x_ref, mod_ref, w_ref, wuq_ref, wukk_ref, wukv_ref,
                   gqa_ref, gka_ref, gqb_ref, gkvb_ref,
                   ca_ref, pa_ref, ma_ref, cb_ref, pb_ref, mb_ref,
                   qa_o, ka_o, va_o, qb_o, kb_o, vb_o, z_o, xbc_o, dt_o):
    m = mod_ref[0]
    h = (x_ref[0] * (1.0 + m[1:2]) + m[0:1]).astype(BF16)

    def proj(a, b):
        return _dot(h, w_ref[:, a:b])

    ca, pa, ma = ca_ref[...], pa_ref[...], ma_ref[...]
    cb, pb, mb = cb_ref[...], pb_ref[...], mb_ref[...]
    half = HEAD_DIM // 2
    pq = proj(P_QA, P_KA)
    for ch in range(A_HEADS // 2):
        xa = _headnorm2(pq[:, ch * LANES:(ch + 1) * LANES], gqa_ref[...])
        xa = _rope(xa, ca, pa, ma, half // 2) * (HEAD_DIM ** -0.5)
        qa_o[0, 2 * ch] = xa[:, :HEAD_DIM].astype(BF16)
        qa_o[0, 2 * ch + 1] = xa[:, HEAD_DIM:].astype(BF16)
    pkv = proj(P_KA, P_CQ)
    xk = _rope(_headnorm2(pkv[:, :LANES], gka_ref[...]), ca, pa, ma, half // 2)
    ka_o[0, 0] = xk[:, :HEAD_DIM].astype(BF16)
    ka_o[0, 1] = xk[:, HEAD_DIM:].astype(BF16)
    _store_values(pkv[:, LANES:], va_o, 0)
    pc = proj(P_CQ, P_Z)
    cqn = _rms_rows(pc[:, :Q_LORA], gqb_ref[...]).astype(BF16)
    qb = _dot(cqn, wuq_ref[...])
    ckvn = _rms_rows(pc[:, Q_LORA:], gkvb_ref[...]).astype(BF16)
    kn = _dot(ckvn, wukk_ref[...])
    vv = _dot(ckvn, wukv_ref[...])
    for pr in range(B_HEADS // 2):
        _store_values(vv[:, pr * LANES:(pr + 1) * LANES], vb_o, 2 * pr)
    px = proj(P_XBC, P_END)
    dtkr = px[:, CONV_DIM:]
    lane = lax.broadcasted_iota(jnp.int32, dtkr.shape, 1)
    is_kr = (lane >= KR_LANE) & (lane < KR_LANE + MLA_ROPE)
    k_rope = _rope(jnp.where(is_kr, dtkr, 0.0), cb, pb, mb, MLA_ROPE // 4)
    scale_b = (MLA_NOPE + MLA_ROPE) ** -0.5
    for hh in range(B_HEADS):
        sl = slice(hh * LANES, (hh + 1) * LANES)
        qb_o[0, hh] = (_rope(qb[:, sl], cb, pb, mb, MLA_ROPE // 4) * scale_b).astype(BF16)
        kb_o[0, hh] = (kn[:, sl] + k_rope).astype(BF16)
    z_o[0] = proj(P_Z, P_XBC)
    xbc_o[0] = px[:, :CONV_DIM]
    dt_o[0] = jnp.where(lane < 2 * C_HEADS, dtkr, 0.0)


def _inproj(x, mod, lw, tabs, tm):
    s, l, d = x.shape
    nt = l // tm
    full = lambda shape: pl.BlockSpec(shape, lambda si, i: (0,) * len(shape))
    tab = pl.BlockSpec((tm, LANES), lambda si, i: (i, 0))
    head = lambda nh, dh: pl.BlockSpec((1, nh, tm, dh), lambda si, i: (si, 0, i, 0))
    row = lambda w: pl.BlockSpec((1, tm, w), lambda si, i: (si, i, 0))
    out_shape = (
        jax.ShapeDtypeStruct((s, A_HEADS, l, HEAD_DIM), BF16),
        jax.ShapeDtypeStruct((s, A_KV_HEADS, l, HEAD_DIM), BF16),
        jax.ShapeDtypeStruct((s, A_KV_HEADS, l, LANES), BF16),
        jax.ShapeDtypeStruct((s, B_HEADS, l, LANES), BF16),
        jax.ShapeDtypeStruct((s, B_HEADS, l, LANES), BF16),
        jax.ShapeDtypeStruct((s, B_HEADS, l, LANES), BF16),
        jax.ShapeDtypeStruct((s, l, C_INNER), F32),
        jax.ShapeDtypeStruct((s, l, CONV_DIM), F32),
        jax.ShapeDtypeStruct((s, l, LANES), F32),
    )
    out_specs = (head(A_HEADS, HEAD_DIM), head(A_KV_HEADS, HEAD_DIM), head(A_KV_HEADS, LANES),
                 head(B_HEADS, LANES), head(B_HEADS, LANES), head(B_HEADS, LANES),
                 row(C_INNER), row(CONV_DIM), row(LANES))
    return pl.pallas_call(
        _inproj_kernel,
        out_shape=out_shape,
        grid=(s, nt),
        in_specs=[row(d),
                  pl.BlockSpec((1, 6, d), lambda si, i: (si, 0, 0)),
                  full((d, P_END)), full((Q_LORA, B_HEADS * LANES)),
                  full((KV_LORA, B_HEADS * LANES)), full((KV_LORA, B_HEADS * MLA_V)),
                  full((1, LANES)), full((1, LANES)), full((1, Q_LORA)), full((1, KV_LORA)),
                  tab, tab, tab, tab, tab, tab],
        out_specs=out_specs,
        compiler_params=_params(("parallel", "parallel"), 48),
        name="inproj",
    )(x, mod, lw["w_in"], lw["w_uq"], lw["w_ukk"], lw["w_ukv"],
      lw["gqa"], lw["gka"], lw["gqb"], lw["gkvb"], *tabs)


ATT_DV = 64
ATT_Q_ROWS = 256


def _attn_kernel(q_ref, k_ref, v_ref, o_ref, *, shared_kv):
    tq = q_ref.shape[2]
    qr = min(ATT_Q_ROWS, tq)
    items = [(j, r) for j in range(2) for r in range(tq // qr)]

    def scores(j, r):
        kj = 0 if shared_kv else j
        return _dot_nt(q_ref[0, j, r * qr:(r + 1) * qr, :], k_ref[0, kj])

    s_next = scores(*items[0])
    outs = {}
    for i, (j, r) in enumerate(items):
        s = s_next
        if i + 1 < len(items):
            s_next = scores(*items[i + 1])
        kj = 0 if shared_kv else j
        p = jnp.exp(s - jnp.max(s, axis=-1, keepdims=True)).astype(BF16)
        o = _dot(p, v_ref[0, kj])
        outs[(j, r)] = o[:, :ATT_DV] / o[:, ATT_DV:ATT_DV + 1]
    for r in range(tq // qr):
        o_ref[0, r * qr:(r + 1) * qr, :] = jnp.concatenate([outs[(0, r)], outs[(1, r)]], axis=-1)


def _attention(q, k, v, tq, name):
    s, hq, l, dk = q.shape
    hk = k.shape[1]
    shared = hq // hk == 2
    kvb = 1 if shared else 2
    return pl.pallas_call(
        functools.partial(_attn_kernel, shared_kv=shared),
        out_shape=jax.ShapeDtypeStruct((s, l, hq * ATT_DV), F32),
        grid=(s, hq // 2, l // tq),
        in_specs=[pl.BlockSpec((1, 2, tq, dk), lambda si, hp, qi: (si, hp, qi, 0)),
                  pl.BlockSpec((1, kvb, l, dk), lambda si, hp, qi: (si, hp, 0, 0)),
                  pl.BlockSpec((1, kvb, l, LANES), lambda si, hp, qi: (si, hp, 0, 0))],
        out_specs=pl.BlockSpec((1, tq, 2 * ATT_DV), lambda si, hp, qi: (si, qi, hp)),
        compiler_params=_params(("parallel", "parallel", "arbitrary"), 48),
        name=name,
    )(q, k, v)


def _ssd_kernel(xbc_ref, dt_ref, z_ref, cw_ref, cb_ref, dtb_ref, alog_ref, dskip_ref, g_ref,
                exf_ref, exb_ref, bd_ref, o_ref, xpad, xact, ad_s, hf, hb):
    l = xbc_ref.shape[1]
    nc = l // SSD_Q
    q = SSD_Q
    pad = 8
    xpad[0:pad, :] = jnp.zeros((pad, CONV_DIM), F32)
    xpad[l + pad:l + 2 * pad, :] = jnp.zeros((pad, CONV_DIM), F32)
    xpad[pad:l + pad, :] = xbc_ref[0]
    def conv_chunk(c, carry):
        r0 = pl.multiple_of(c * q, q)
        for st in range(CONV_DIM // LANES):
            cols = slice(st * LANES, (st + 1) * LANES)
            win = xpad[pl.ds(r0, q + 2 * pad), cols]
            acc = jnp.zeros((q, LANES), F32) + cb_ref[:, cols]
            for k in range(D_CONV):
                lo = pad - D_CONV // 2 + k
                acc = acc + win[lo:lo + q] * cw_ref[k:k + 1, cols]
            xact[pl.ds(r0, q), cols] = _silu(acc)
        return carry

    lax.fori_loop(0, nc, conv_chunk, 0)
    dtr = dt_ref[0] + dtb_ref[...]
    dt = jnp.maximum(dtr, 0.0) + jnp.log1p(jnp.exp(-jnp.abs(dtr)))
    lane_l = lax.broadcasted_iota(jnp.int32, dt.shape, 1)
    a = dt * (-jnp.exp(alog_ref[...]))
    ad_s[...] = jnp.where(lane_l < 2 * C_HEADS, a, pltpu.roll(dt, 2 * C_HEADS, 1))

    ri = lax.broadcasted_iota(jnp.int32, (q, q), 0)
    ci = lax.broadcasted_iota(jnp.int32, (q, q), 1)
    lower = ri >= ci
    upper = ci >= ri
    tril = jnp.where(lower, 1.0, 0.0).astype(BF16)
    lane_q = lax.broadcasted_iota(jnp.int32, (q, LANES), 1)
    first8 = lane_q < C_HEADS
    mid8 = (lane_q >= C_HEADS) & (lane_q < 2 * C_HEADS)
    lo64 = lane_q < C_HEAD_DIM
    bd = bd_ref[...] > 0.5

    def chunk_decay(r0):
        ad = ad_s[pl.ds(r0, q), :]
        hi, lo = _split_bf16(ad)
        inc = _dot(tril, hi) + _dot(tril, lo)
        exc = inc - ad
        tot = inc[q - 1:q, :]
        dts = pltpu.roll(ad, LANES - 2 * C_HEADS, 1)
        return ad, inc, exc, tot, dts

    def expand(v, ex_ref):
        hi, lo = _split_bf16(v)
        return _dot(hi, ex_ref[...]) + _dot(lo, ex_ref[...])

    hf[...] = jnp.zeros_like(hf)
    hb[...] = jnp.zeros_like(hb)

    def fwd_body(c, carry):
        r0 = pl.multiple_of(c * q, q)
        ad, inc, exc, tot, dts = chunk_decay(r0)
        inc_t = inc.T
        ad_t = ad.T
        exc_t = inc_t - ad_t
        xq = xact[pl.ds(r0, q), :]
        xs = xq[:, :C_INNER]
        xs_bf = xs.astype(BF16)
        bm = xq[:, C_INNER:C_INNER + LANES]
        cm = xq[:, C_INNER + LANES:]
        bt_bf = bm.T.astype(BF16)
        cm_bf = cm.astype(BF16)
        gmat = []
        for g in range(SSM_GROUPS):
            cg = jnp.where(lo64 if g == 0 else jnp.logical_not(lo64), cm, 0.0).astype(BF16)
            gmat.append(_dot(cg, bt_bf))
        for pr in range(C_HEADS // 2):
            ys = []
            for hh in (2 * pr, 2 * pr + 1):
                icol = jnp.broadcast_to(inc[:, hh:hh + 1], (q, q))
                irow = jnp.broadcast_to(inc_t[hh:hh + 1, :], (q, q))
                lf = jnp.exp(jnp.where(lower, icol - irow, NEG_BIG)) * ad_t[16 + hh:17 + hh, :]
                ecol = jnp.broadcast_to(exc[:, 8 + hh:9 + hh], (q, q))
                erow = jnp.broadcast_to(exc_t[8 + hh:9 + hh, :], (q, q))
                lb = jnp.exp(jnp.where(upper, erow - ecol, NEG_BIG)) * ad_t[24 + hh:25 + hh, :]
                w = (gmat[hh // (C_HEADS // SSM_GROUPS)] * (lf + lb)).astype(BF16)
                ys.append(_dot(w, xs_bf[:, pr * LANES:(pr + 1) * LANES]))
            o_ref[0, pl.ds(r0, q), pr * LANES:(pr + 1) * LANES] = jnp.where(lo64, ys[0], ys[1])
        pf = expand(jnp.where(first8, jnp.exp(inc), 0.0), exf_ref)
        sf = expand(jnp.where(first8, jnp.exp(tot - inc) * dts, 0.0), exf_ref)
        y_off = _dot(cm_bf, hf[...].astype(BF16)) * pf
        o_ref[0, pl.ds(r0, q), :] = o_ref[0, pl.ds(r0, q), :] + y_off
        st = _dot(bt_bf, (xs * sf).astype(BF16))
        hf[...] = pf[q - 1:q, :] * hf[...] + jnp.where(bd, st, 0.0)
        return carry

    lax.fori_loop(0, nc // 2, lambda i, cr: fwd_body(2 * i + 1, fwd_body(2 * i, cr)), 0)

    def bwd_body(i, carry):
        c = nc - 1 - i
        r0 = pl.multiple_of(c * q, q)
        ad, inc, exc, tot, dts = chunk_decay(r0)
        xq = xact[pl.ds(r0, q), :]
        xs = xq[:, :C_INNER]
        bt_bf = xq[:, C_INNER:C_INNER + LANES].T.astype(BF16)
        cm_bf = xq[:, C_INNER + LANES:].astype(BF16)
        pbk = expand(jnp.where(mid8, jnp.exp(tot - exc), 0.0), exb_ref)
        sbk = expand(jnp.where(mid8, jnp.exp(exc) * dts, 0.0), exb_ref)
        y = o_ref[0, pl.ds(r0, q), :] + _dot(cm_bf, hb[...].astype(BF16)) * pbk
        st = _dot(bt_bf, (xs * sbk).astype(BF16))
        hb[...] = pbk[0:1, :] * hb[...] + jnp.where(bd, st, 0.0)
        y = (y + dskip_ref[...] * xs) * _silu(z_ref[0, pl.ds(r0, q), :])
        o_ref[0, pl.ds(r0, q), :] = _rms_rows(y, g_ref[...])
        return carry

    lax.fori_loop(0, nc // 2, lambda i, cr: bwd_body(2 * i + 1, bwd_body(2 * i, cr)), 0)


def _ssd(xbc, dt, z, lw, consts):
    s, l, _ = xbc.shape
    assert l % (2 * SSD_Q) == 0
    seq = lambda w: pl.BlockSpec((1, l, w), lambda si: (si, 0, 0))
    full = lambda shape: pl.BlockSpec(shape, lambda si: (0,) * len(shape))
    return pl.pallas_call(
        _ssd_kernel,
        out_shape=jax.ShapeDtypeStruct((s, l, C_INNER), F32),
        grid=(s,),
        in_specs=[seq(CONV_DIM), seq(LANES), seq(C_INNER),
                  full((D_CONV, CONV_DIM)), full((1, CONV_DIM)), full((1, LANES)), full((1, LANES)),
                  full((1, C_INNER)), full((1, C_INNER)),
                  full((LANES, C_INNER)), full((LANES, C_INNER)), full((LANES, C_INNER))],
        out_specs=seq(C_INNER),
        scratch_shapes=[pltpu.VMEM((l + 16, CONV_DIM), F32), pltpu.VMEM((l, CONV_DIM), F32),
                        pltpu.VMEM((l, LANES), F32),
                        pltpu.VMEM((LANES, C_INNER), F32), pltpu.VMEM((LANES, C_INNER), F32)],
        compiler_params=_params(("parallel",), 56),
        name="ssd",
    )(xbc, dt, z, lw["conv_w"], lw["conv_b"], lw["dt_bias"], lw["a_log"], lw["d_skip"], lw["ssm_g"],
      consts["exf"], consts["exb"], consts["bd"])


def _route(logits):
    lane = lax.broadcasted_iota(jnp.int32, logits.shape, 1)
    lane_f = lane.astype(F32)
    neg_inf = -jnp.inf
    isg = (lane >= N_EXPERTS) & (lane < N_EXPERTS + N_EGROUPS)
    gl = jnp.where(isg, logits, neg_inf)
    gmax = jnp.max(gl, axis=-1, keepdims=True)
    gsel = jnp.min(jnp.where(gl == gmax, lane_f - N_EXPERTS, 1e9), axis=-1, keepdims=True)
    gw = 1.0 / jnp.sum(jnp.where(isg, jnp.exp(gl - gmax), 0.0), axis=-1, keepdims=True)
    grp = jnp.floor(lane_f * (1.0 / EXPERTS_PER_GROUP))
    el = jnp.where((lane < N_EXPERTS) & (grp == gsel), logits, neg_inf)
    l1 = jnp.max(el, axis=-1, keepdims=True)
    i1 = jnp.min(jnp.where(el == l1, lane_f, 1e9), axis=-1, keepdims=True)
    el2 = jnp.where(lane_f == i1, neg_inf, el)
    l2 = jnp.max(el2, axis=-1, keepdims=True)
    i2 = jnp.min(jnp.where(el2 == l2, lane_f, 1e9), axis=-1, keepdims=True)
    r = jnp.exp(l2 - l1)
    g1 = gw / (1.0 + r)
    g2 = g1 * r
    return jnp.where(lane == 0, i1, jnp.where(lane == 1, i2, jnp.where(lane == 2, g1,
                                                                    jnp.where(lane == 3, g2, 0.0))))


def _outproj_kernel(oa_ref, ob_ref, oc_ref, x_ref, mod_ref, wout_ref, goa_ref, gob_ref,
                    l1g_ref, l1b_ref, wrc_ref, br_ref, x1_o, h2v_o, route_o, ct):
    oa = _rms_rows(oa_ref[0], goa_ref[...]).astype(BF16)
    ob = _rms_rows(ob_ref[0], gob_ref[...]).astype(BF16)
    oc = oc_ref[0].astype(BF16)
    wa = A_HEADS * HEAD_DIM
    wb = wa + B_HEADS * MLA_V
    y = _dot(oa, wout_ref[0:wa, :]) + _dot(ob, wout_ref[wa:wb, :]) + _dot(oc, wout_ref[wb:, :])
    m = mod_ref[0]
    x1 = _ln_rows(DN_ALPHA * x_ref[0] + (1.0 + m[2:3]) * y, l1g_ref[...], l1b_ref[...])
    x1_o[0] = x1
    h2 = x1 * (1.0 + m[4:5]) + m[3:4]
    _rows_to_tiles(h2, ct, h2v_o)
    hi, lo = _split_bf16(h2)
    hw = _dot(hi, wrc_ref[...])
    logits = hw[:, :LANES] + hw[:, LANES:] + _dot(lo, wrc_ref[:, :LANES]) + br_ref[...]
    route_o[0] = _route(logits)


def _outproj(oa, ob, oc, x, mod, lw, tm):
    s, l, d = x.shape
    nt = l // tm
    nch = d // LANES
    row = lambda w: pl.BlockSpec((1, tm, w), lambda si, i: (si, i, 0))
    full = lambda shape: pl.BlockSpec(shape, lambda si, i: (0,) * len(shape))
    return pl.pallas_call(
        _outproj_kernel,
        out_shape=(jax.ShapeDtypeStruct((s, l, d), F32), jax.ShapeDtypeStruct((s * l * nch, LANES), F32),
                   jax.ShapeDtypeStruct((s, l, LANES), F32)),
        grid=(s, l // tm),
        in_specs=[row(A_HEADS * HEAD_DIM), row(B_HEADS * MLA_V), row(C_INNER), row(d),
                  pl.BlockSpec((1, 6, d), lambda si, i: (si, 0, 0)),
                  full((d, d)), full((1, A_HEADS * HEAD_DIM)), full((1, B_HEADS * MLA_V)),
                  full((1, d)), full((1, d)), full((d, 2 * LANES)), full((1, LANES))],
        out_specs=(row(d), pl.BlockSpec((tm * nch, LANES), lambda si, i: (si * nt + i, 0)), row(LANES)),
        scratch_shapes=[pltpu.VMEM((nch * _pitch(tm), LANES), F32)],
        compiler_params=_params(("parallel", "parallel"), 48),
        name="outproj",
    )(oa, ob, oc, x, mod, lw["w_out"], lw["goa"], lw["gob"], lw["ln1_g"], lw["ln1_b"],
      lw["w_r"], lw["b_r"])


MOE_ROWS = 128
ROW_STRIDE = _pitch(MOE_ROWS)


def _slot_rows(l):
    return 2 * l + N_EXPERTS * 8


def _route_sort_kernel(route_ref, slots_o, gates_o, meta_o):
    r = route_ref[0]
    l = r.shape[0]
    lane = lax.broadcasted_iota(jnp.int32, (l, LANES), 1)
    lane_f = lane.astype(F32)
    oh1 = lane_f == jnp.broadcast_to(r[:, 0:1], (l, LANES))
    oh2 = lane_f == jnp.broadcast_to(r[:, 1:2], (l, LANES))
    member = jnp.where(oh1 | oh2, 1.0, 0.0).astype(BF16)
    tb = min(256, l)
    ri = lax.broadcasted_iota(jnp.int32, (tb, tb), 0)
    ci = lax.broadcasted_iota(jnp.int32, (tb, tb), 1)
    stril = jnp.where(ri > ci, 1.0, 0.0).astype(BF16)
    run = jnp.zeros((1, LANES), F32)
    ranks = []
    for i in range(l // tb):
        blk = member[i * tb:(i + 1) * tb]
        ranks.append(_dot(stril, blk) + run)
        run = run + jnp.sum(blk.astype(F32), axis=0, keepdims=True)
    rank_all = jnp.concatenate(ranks, axis=0)
    padded = jnp.floor((run + 7.0) * 0.125) * 8.0
    hi, lo = _split_bf16(jnp.broadcast_to(padded, (8, LANES)))
    rl = lax.broadcasted_iota(jnp.int32, (LANES, LANES), 0)
    cl = lax.broadcasted_iota(jnp.int32, (LANES, LANES), 1)
    upper = jnp.where(rl < cl, 1.0, 0.0).astype(BF16)
    base = (_dot(hi, upper) + _dot(lo, upper))[0:1]
    pos = base + rank_all
    slot1 = jnp.sum(jnp.where(oh1, pos, 0.0), axis=-1, keepdims=True)
    slot2 = jnp.sum(jnp.where(oh2, pos, 0.0), axis=-1, keepdims=True)
    packed = jnp.where(lane == 0, slot1, jnp.where(lane == 1, slot2, jnp.where(lane < 4, r, 0.0)))
    pt = packed.T
    slots_o[0] = pt[0:2].astype(jnp.int32)
    gates_o[0] = pt[2:4]
    meta_o[0] = jnp.concatenate([base, run, jnp.zeros((6, LANES), F32)], axis=0).astype(jnp.int32)


def _route_sort(route):
    s, l, _ = route.shape
    return pl.pallas_call(
        _route_sort_kernel,
        out_shape=(jax.ShapeDtypeStruct((s, 2, l), jnp.int32), jax.ShapeDtypeStruct((s, 2, l), F32),
                   jax.ShapeDtypeStruct((s, 8, LANES), jnp.int32)),
        grid=(s,),
        in_specs=[pl.BlockSpec((1, l, LANES), lambda si: (si, 0, 0))],
        out_specs=(pl.BlockSpec((1, 2, l), lambda si: (si, 0, 0)), pl.BlockSpec((1, 2, l), lambda si: (si, 0, 0)),
                   pl.BlockSpec((1, 8, LANES), lambda si: (si, 0, 0))),
        compiler_params=_params(("parallel",), 32),
        name="route_sort",
    )(route)


def _moe_kernel(meta_ref, h2v_ref, slots_hbm, gates_hbm, wg_ref, wu_ref, wd_ref,
                x1_ref, mod_ref, l2g_ref, l2b_ref, o_ref,
                slot_s, gate_s, tok_s, gat_s, acc_v, xt, yt, ct, sem):
    s = pl.program_id(0)
    e = pl.program_id(1)
    l = h2v_ref.shape[0] // 8
    nch = D_MODEL // LANES

    @pl.when(e == 0)
    def _():
        c_slots = pltpu.make_async_copy(slots_hbm.at[s], slot_s, sem.at[0])
        c_gates = pltpu.make_async_copy(gates_hbm.at[s], gate_s, sem.at[1])
        c_slots.start()
        c_gates.start()
        acc_v[...] = jnp.zeros_like(acc_v)
        xt[...] = jnp.zeros_like(xt)

        def clear(ex, c):
            n = meta_ref[s, 1, ex]
            last = meta_ref[s, 0, ex] + ((n + 7) // 8) * 8 - 8

            @pl.when(n > 0)
            def _():
                for j in range(8):
                    tok_s[last + j] = l
                    gat_s[last + j] = 0.0

            return c

        lax.fori_loop(0, N_EXPERTS, clear, 0)
        c_slots.wait()
        c_gates.wait()

        def invert(g, c):
            for j in range(8):
                t = g * 8 + j
                for k in range(2):
                    sl = slot_s[k, t]
                    tok_s[sl] = t
                    gat_s[sl] = gate_s[k, t]
            return c

        lax.fori_loop(0, l // 8, invert, 0)

    base = meta_ref[s, 0, e]
    ngrp = (meta_ref[s, 1, e] + 7) // 8
    grp_per_blk = MOE_ROWS // 8

    def block(blk, c):
        g_lo = blk * grp_per_blk
        ng = jnp.minimum(grp_per_blk, ngrp - g_lo)

        def gather(g, cc):
            r0 = base + (g_lo + g) * 8
            for j in range(8):
                t = jnp.minimum(tok_s[r0 + j], l - 1)
                row = h2v_ref[pl.ds(pl.multiple_of(t * 8, 8), 8), :]
                xt[pl.ds(g * 8 + j, nch, stride=ROW_STRIDE), :] = row
            return cc

        lax.fori_loop(0, ng, gather, 0)
        xb = jnp.concatenate([xt[j * ROW_STRIDE:j * ROW_STRIDE + MOE_ROWS, :] for j in range(nch)],
                             axis=-1).astype(BF16)
        mid = (_silu(_dot(xb, wg_ref[0])) * _dot(xb, wu_ref[0])).astype(BF16)
        y = _dot(mid, wd_ref[0])
        for j in range(nch):
            yt[j * ROW_STRIDE:j * ROW_STRIDE + MOE_ROWS, :] = y[:, j * LANES:(j + 1) * LANES]

        def combine(g, cc):
            r0 = base + (g_lo + g) * 8
            toks = [pl.multiple_of(tok_s[r0 + j] * 8, 8) for j in range(8)]
            vals = [acc_v[pl.ds(toks[j], 8), :]
                    + gat_s[r0 + j] * yt[pl.ds(g * 8 + j, nch, stride=ROW_STRIDE), :] for j in range(8)]
            for j in range(8):
                acc_v[pl.ds(toks[j], 8), :] = vals[j]
            return cc

        lax.fori_loop(0, ng, combine, 0)
        return c

    lax.fori_loop(0, (ngrp + grp_per_blk - 1) // grp_per_blk, block, 0)

    @pl.when(e >= N_EXPERTS)
    def _():
        tmo = x1_ref.shape[1]
        y = _tiles_to_rows(acc_v, (e - N_EXPERTS) * tmo, tmo, ct)
        m = mod_ref[0]
        o_ref[0] = _ln_rows(DN_ALPHA * x1_ref[0] + (1.0 + m[5:6]) * y, l2g_ref[...], l2b_ref[...])


def _moe(h2v, slots, gates, meta, x1, mod, lw, tm):
    s, l, d = x1.shape
    n_slots = _slot_rows(l)
    last = N_EXPERTS - 1
    wspec = lambda shape: pl.BlockSpec(shape, lambda si, e: (jnp.minimum(e, last), 0, 0))
    tok_blk = pl.BlockSpec((1, tm, d), lambda si, e: (si, jnp.maximum(e - N_EXPERTS, 0), 0))
    vec = pl.BlockSpec((1, d), lambda si, e: (0, 0))
    return pl.pallas_call(
        _moe_kernel,
        out_shape=jax.ShapeDtypeStruct((s, l, d), F32),
        grid=(s, N_EXPERTS + l // tm),
        in_specs=[pl.BlockSpec(memory_space=pltpu.SMEM),
                  pl.BlockSpec((l * 8, LANES), lambda si, e: (si, 0)),
                  pl.BlockSpec(memory_space=pl.ANY), pl.BlockSpec(memory_space=pl.ANY),
                  wspec((1, d, D_FF_EXPERT)), wspec((1, d, D_FF_EXPERT)), wspec((1, D_FF_EXPERT, d)),
                  tok_blk, pl.BlockSpec((1, 6, d), lambda si, e: (si, 0, 0)), vec, vec],
        out_specs=tok_blk,
        scratch_shapes=[pltpu.SMEM((2, l), jnp.int32), pltpu.SMEM((2, l), F32),
                        pltpu.SMEM((n_slots,), jnp.int32), pltpu.SMEM((n_slots,), F32),
                        pltpu.VMEM(((l + 1) * 8, LANES), F32),
                        pltpu.VMEM((8 * ROW_STRIDE, LANES), F32), pltpu.VMEM((8 * ROW_STRIDE, LANES), F32),
                        pltpu.VMEM((8 * _pitch(tm), LANES), F32),
                        pltpu.SemaphoreType.DMA((2,))],
        compiler_params=_params(("parallel", "arbitrary"), 56),
        name="moe",
    )(meta, h2v, slots, gates, lw["w_e_gate"], lw["w_e_up"], lw["w_e_down"], x1, mod,
      lw["ln2_g"], lw["ln2_b"])


def _rope_tables(seq_len):
    t = jnp.arange(seq_len, dtype=jnp.int32)
    row = (t // GRID_W).astype(F32)[:, None]
    col = (t % GRID_W).astype(F32)[:, None]

    def axis_tables(d_axis):
        inv = ROPE_THETA ** (-jnp.arange(0, d_axis, 2, dtype=F32) / d_axis)
        zero = jnp.zeros((seq_len, d_axis // 2), F32)
        cs, ps, ms = [], [], []
        for pos in (row, col):
            ang = pos * inv
            c, sn = jnp.cos(ang), jnp.sin(ang)
            cs += [c, c]
            ps += [zero, sn]
            ms += [-sn, zero]
        return [jnp.concatenate(v, axis=-1) for v in (cs, ps, ms)]

    ta = [jnp.tile(v, (1, 2)) for v in axis_tables(HEAD_DIM // 2)]
    one = jnp.ones((seq_len, MLA_NOPE), F32)
    zero = jnp.zeros((seq_len, MLA_NOPE), F32)
    pad1 = jnp.ones((seq_len, LANES - MLA_NOPE - MLA_ROPE), F32)
    pad0 = jnp.zeros((seq_len, LANES - MLA_NOPE - MLA_ROPE), F32)
    cb, pb, mb = axis_tables(MLA_ROPE // 2)
    tb = [jnp.concatenate([one, cb, pad1], -1), jnp.concatenate([zero, pb, pad0], -1),
          jnp.concatenate([zero, mb, pad0], -1)]
    return ta + tb


def _ssd_consts():
    r = np.arange(LANES)[:, None]
    c = np.arange(C_INNER)[None, :]
    exf = ((r < C_HEADS) & (c // C_HEAD_DIM == r)).astype(np.float32)
    exb = ((r >= C_HEADS) & (r < 2 * C_HEADS) & (c // C_HEAD_DIM == r - C_HEADS)).astype(np.float32)
    heads_per_group = C_HEADS // SSM_GROUPS
    bd = (r // D_STATE == c // (C_HEAD_DIM * heads_per_group)).astype(np.float32)
    return {"exf": jnp.asarray(exf, BF16), "exb": jnp.asarray(exb, BF16), "bd": jnp.asarray(bd, F32)}


def _pack_weights(p):
    w_in = p["w_in"]
    depth = w_in.shape[0]
    off = np.cumsum((0, 256, 128, 128, Q_LORA, KV_LORA, MLA_ROPE, C_INNER, CONV_DIM, 2 * C_HEADS))
    w_kr = w_in[:, :, off[5]:off[6]]
    zpad = lambda n: jnp.zeros((depth, D_MODEL, n), F32)
    dtkr = jnp.concatenate([w_in[:, :, off[8]:], zpad(KR_LANE - 2 * C_HEADS), w_kr,
                            zpad(LANES - KR_LANE - MLA_ROPE)], -1)
    w_in_p = jnp.concatenate([w_in[:, :, :off[5]], w_in[:, :, off[6]:off[8]], dtkr], -1).astype(BF16)
    dq = MLA_NOPE + MLA_ROPE
    w_uq = p["w_uq"].reshape(depth, Q_LORA, B_HEADS, dq)
    w_uq_p = jnp.concatenate([w_uq, jnp.zeros((depth, Q_LORA, B_HEADS, LANES - dq), F32)], -1)
    w_ukv = p["w_ukv"].reshape(depth, KV_LORA, B_HEADS, MLA_NOPE + MLA_V)
    w_ukk = jnp.concatenate([w_ukv[..., :MLA_NOPE],
                             jnp.zeros((depth, KV_LORA, B_HEADS, LANES - MLA_NOPE), F32)], -1)
    pad_lanes = lambda v: jnp.concatenate([v, jnp.zeros(v.shape[:-1] + (LANES - v.shape[-1],), F32)], -1)
    w_r = pad_lanes(jnp.concatenate([p["w_re"], p["w_rg"]], -1))
    w_r_hi = w_r.astype(BF16)
    return {
        "w_ada": p["w_ada"].astype(BF16), "b_ada": p["b_ada"],
        "w_in": w_in_p,
        "w_uq": w_uq_p.reshape(depth, Q_LORA, B_HEADS * LANES).astype(BF16),
        "w_ukk": w_ukk.reshape(depth, KV_LORA, B_HEADS * LANES).astype(BF16),
        "w_ukv": w_ukv[..., MLA_NOPE:].reshape(depth, KV_LORA, B_HEADS * MLA_V).astype(BF16),
        "gqa": jnp.tile(p["qa_norm_g"], (1, 2))[:, None, :], "gka": jnp.tile(p["ka_norm_g"], (1, 2))[:, None, :],
        "gqb": p["qb_norm_g"][:, None, :], "gkvb": p["kvb_norm_g"][:, None, :],
        "goa": p["oa_norm_g"][:, None, :], "gob": p["ob_norm_g"][:, None, :],
        "conv_w": p["conv_w"], "conv_b": p["conv_b"][:, None, :],
        "dt_bias": pad_lanes(p["dt_bias"].reshape(depth, 2 * C_HEADS))[:, None, :],
        "a_log": pad_lanes(p["a_log"].reshape(depth, 2 * C_HEADS))[:, None, :],
        "d_skip": jnp.repeat(p["d_skip"], C_HEAD_DIM, axis=-1)[:, None, :],
        "ssm_g": p["ssm_norm_g"][:, None, :],
        "w_out": p["w_out"].astype(BF16),
        "ln1_g": p["ln1_g"][:, None, :], "ln1_b": p["ln1_b"][:, None, :],
        "w_r": jnp.concatenate([w_r_hi, (w_r - w_r_hi.astype(F32)).astype(BF16)], -1),
        "b_r": pad_lanes(jnp.concatenate([p["b_re"], p["b_rg"]], -1))[:, None, :],
        "w_e_gate": p["w_e_gate"].astype(BF16), "w_e_up": p["w_e_up"].astype(BF16),
        "w_e_down": p["w_e_down"].astype(BF16),
        "ln2_g": p["ln2_g"][:, None, :], "ln2_b": p["ln2_b"][:, None, :],
    }


def _trunk(x, c, p, tm=512, tq=512):
    s, l, d = x.shape
    tm = min(tm, l)
    tq = min(tq, l)
    pw = _pack_weights(p)
    tabs = _rope_tables(l)
    consts = _ssd_consts()
    mods = _ada(c, pw["w_ada"], pw["b_ada"]).reshape(DEPTH, s, 6, d)
    x = _ln0(x.reshape(s * l, d), p["ln0_g"][None, :], p["ln0_b"][None, :], tm).reshape(s, l, d)
    for layer in range(DEPTH):
        lw = {k: v[layer] for k, v in pw.items() if k not in ("w_ada", "b_ada")}
        mod = mods[layer]
        qa, ka, va, qb, kb, vb, z, xbc, dt = _inproj(x, mod, lw, tabs, tm)
        oa = _attention(qa, ka, va, tq, "attn_a")
        ob = _attention(qb, kb, vb, tq, "attn_b")
        oc = _ssd(xbc, dt, z, lw, consts)
        x1, h2v, route = _outproj(oa, ob, oc, x, mod, lw, tm)
        slots, gates, meta = _route_sort(route)
        x = _moe(h2v, slots, gates, meta, x1, mod, lw, tm)
    return x


def kernel(x_prompt, x_sample, c_prompt, c_sample, ln0_g, ln0_b, w_ada, b_ada, w_in, qa_norm_g, ka_norm_g, oa_norm_g, qb_norm_g, w_uq, kvb_norm_g, w_ukv, ob_norm_g, conv_w, conv_b, dt_bias, a_log, d_skip, ssm_norm_g, w_out, ln1_g, ln1_b, w_rg, b_rg, w_re, b_re, w_e_gate, w_e_up, w_e_down, ln2_g, ln2_b):
    p = dict(ln0_g=ln0_g, ln0_b=ln0_b, w_ada=w_ada, b_ada=b_ada, w_in=w_in,
             qa_norm_g=qa_norm_g, ka_norm_g=ka_norm_g, oa_norm_g=oa_norm_g,
             qb_norm_g=qb_norm_g, w_uq=w_uq, kvb_norm_g=kvb_norm_g, w_ukv=w_ukv, ob_norm_g=ob_norm_g,
             conv_w=conv_w, conv_b=conv_b, dt_bias=dt_bias, a_log=a_log, d_skip=d_skip,
             ssm_norm_g=ssm_norm_g, w_out=w_out, ln1_g=ln1_g, ln1_b=ln1_b,
             w_rg=w_rg, b_rg=b_rg, w_re=w_re, b_re=b_re,
             w_e_gate=w_e_gate, w_e_up=w_e_up, w_e_down=w_e_down, ln2_g=ln2_g, ln2_b=ln2_b)
    nb = x_prompt.shape[0]
    y = _trunk(jnp.concatenate([x_prompt, x_sample], 0), jnp.concatenate([c_prompt, c_sample], 0), p)
    return (y[:nb], y[nb:])
```

```python
import functools

import jax
import jax.numpy as jnp
import numpy as np
from jax import lax
from jax.experimental import pallas as pl
from jax.experimental.pallas import tpu as pltpu

F32 = jnp.float32
BF16 = jnp.bfloat16

D_MODEL = 1024
DEPTH = 4
GRID_W = 64
ROPE_THETA = 10000.0
HEAD_DIM = 64
A_HEADS = 4
A_KV_HEADS = 2
B_HEADS = 4
Q_LORA = 384
KV_LORA = 128
MLA_NOPE = 64
MLA_ROPE = 32
MLA_V = 64
C_HEADS = 8
C_HEAD_DIM = 64
SSM_GROUPS = 2
D_STATE = 64
D_CONV = 5
C_INNER = C_HEADS * C_HEAD_DIM
CONV_DIM = C_INNER + 2 * SSM_GROUPS * D_STATE
N_EGROUPS = 4
EXPERTS_PER_GROUP = 8
N_EXPERTS = N_EGROUPS * EXPERTS_PER_GROUP
D_FF_EXPERT = 256
DN_ALPHA = (2 * DEPTH) ** 0.25
EPS = 1e-6

LANES = 128
SSD_Q = 128
NEG_BIG = -1e30

P_QA, P_KA, P_VA, P_CQ, P_CKV, P_Z, P_XBC, P_DTKR, P_END = (
    0, 256, 384, 512, 896, 1024, 1536, 2304, 2432)
KR_LANE = MLA_NOPE


def _params(sem, vmem_mb):
    return pltpu.CompilerParams(dimension_semantics=sem, vmem_limit_bytes=vmem_mb * 1024 * 1024)


def _silu(x):
    return x * (1.0 / (1.0 + jnp.exp(-x)))


def _ln_rows(x, g, b):
    mu = jnp.mean(x, axis=-1, keepdims=True)
    xc = x - mu
    var = jnp.mean(xc * xc, axis=-1, keepdims=True)
    return xc * lax.rsqrt(var + EPS) * g + b


def _rms_rows(x, g):
    return x * lax.rsqrt(jnp.mean(x * x, axis=-1, keepdims=True) + EPS) * g


def _split_bf16(x):
    hi = x.astype(BF16)
    lo = (x - hi.astype(F32)).astype(BF16)
    return hi, lo


def _dot(a, b):
    return jnp.dot(a, b, preferred_element_type=F32)


def _dot_nt(a, b):
    return lax.dot_general(a, b, (((1,), (1,)), ((), ())), preferred_element_type=F32)


def _pitch(rows):
    return rows + 8


def _rows_to_tiles(x, ct, out_ref):
    m = x.shape[0]
    nch = x.shape[1] // LANES
    p = _pitch(m)
    for j in range(nch):
        ct[j * p:j * p + m, :] = x[:, j * LANES:(j + 1) * LANES]

    def grp(g, c):
        for j in range(8):
            r = g * 8 + j
            out_ref[pl.ds(pl.multiple_of(r * 8, 8), 8), :] = ct[pl.ds(r, nch, stride=p), :]
        return c

    lax.fori_loop(0, m // 8, grp, 0)


def _tiles_to_rows(src_ref, row0, m, ct, step=1, offset=0):
    nch = D_MODEL // LANES
    p = _pitch(m)

    def grp(g, c):
        for j in range(8):
            r = g * 8 + j
            src = pl.multiple_of(((row0 + r) * step + offset) * 8, 8)
            ct[pl.ds(r, nch, stride=p), :] = src_ref[pl.ds(src, 8), :]
        return c

    lax.fori_loop(0, m // 8, grp, 0)
    return jnp.concatenate([ct[j * p:j * p + m, :] for j in range(nch)], axis=-1)


def _ln0_kernel(x_ref, g_ref, b_ref, o_ref):
    o_ref[...] = _ln_rows(x_ref[...], g_ref[...], b_ref[...])


def _ln0(x2d, g, b, tm):
    t, d = x2d.shape
    return pl.pallas_call(
        _ln0_kernel,
        out_shape=jax.ShapeDtypeStruct((t, d), F32),
        grid=(t // tm,),
        in_specs=[pl.BlockSpec((tm, d), lambda i: (i, 0)),
                  pl.BlockSpec((1, d), lambda i: (0, 0)),
                  pl.BlockSpec((1, d), lambda i: (0, 0))],
        out_specs=pl.BlockSpec((tm, d), lambda i: (i, 0)),
        compiler_params=_params(("parallel",), 32),
        name="ln0",
    )(x2d, g, b)


def _ada_kernel(c_ref, w_ref, b_ref, o_ref):
    cs = _silu(c_ref[...]).astype(BF16)
    o_ref[0] = _dot(cs, w_ref[0]) + b_ref[0]


def _ada(c, w_bf, b, tn=2048):
    s, d = c.shape
    depth, _, n = w_bf.shape
    return pl.pallas_call(
        _ada_kernel,
        out_shape=jax.ShapeDtypeStruct((depth, s, n), F32),
        grid=(depth, n // tn),
        in_specs=[pl.BlockSpec((s, d), lambda l, j: (0, 0)),
                  pl.BlockSpec((1, d, tn), lambda l, j: (l, 0, j)),
                  pl.BlockSpec((1, 1, tn), lambda l, j: (l, 0, j))],
        out_specs=pl.BlockSpec((1, s, tn), lambda l, j: (l, 0, j)),
        compiler_params=_params(("parallel", "parallel"), 32),
        name="ada",
    )(c, w_bf, b.reshape(depth, 1, n))


def _rope(x, c, p, m, shift):
    return x * c + pltpu.roll(x, shift, 1) * p + pltpu.roll(x, LANES - shift, 1) * m


def _headnorm2(x, g):
    xx = x * x
    lo = lax.broadcasted_iota(jnp.int32, x.shape, 1) < HEAD_DIM
    s0 = jnp.sum(jnp.where(lo, xx, 0.0), axis=-1, keepdims=True)
    s1 = jnp.sum(jnp.where(lo, 0.0, xx), axis=-1, keepdims=True)
    inv = jnp.where(lo, lax.rsqrt(s0 * (1.0 / HEAD_DIM) + EPS), lax.rsqrt(s1 * (1.0 / HEAD_DIM) + EPS))
    return x * inv * g


def _store_values(v2, v_o, h0):
    lo = lax.broadcasted_iota(jnp.int32, v2.shape, 1) < ATT_DV
    v_o[0, h0] = jnp.where(lo, v2, 1.0).astype(BF16)
    v_o[0, h0 + 1] = jnp.where(lo, pltpu.roll(v2, ATT_DV, 1), 1.0).astype(BF16)


def _inproj_kernel(x_ref, mod_ref, w_ref, wuq_ref, wukk_ref, wukv_ref,
                   gqa_ref, gka_ref, gqb_ref, gkvb_ref,
                   ca_ref, pa_ref, ma_ref, cb_ref, pb_ref, mb_ref,
                   qa_o, ka_o, va_o, qb_o, kb_o, vb_o, z_o, xbc_o, dt_o):
    m = mod_ref[0]
    h = (x_ref[0] * (1.0 + m[1:2]) + m[0:1]).astype(BF16)

    def proj(a, b):
        return _dot(h, w_ref[:, a:b])

    ca, pa, ma = ca_ref[...], pa_ref[...], ma_ref[...]
    cb, pb, mb = cb_ref[...], pb_ref[...], mb_ref[...]
    half = HEAD_DIM // 2
    pq = proj(P_QA, P_KA)
    for ch in range(A_HEADS // 2):
        xa = _headnorm2(pq[:, ch * LANES:(ch + 1) * LANES], gqa_ref[...])
        xa = _rope(xa, ca, pa, ma, half // 2) * (HEAD_DIM ** -0.5)
        qa_o[0, 2 * ch] = xa[:, :HEAD_DIM].astype(BF16)
        qa_o[0, 2 * ch + 1] = xa[:, HEAD_DIM:].astype(BF16)
    pkv = proj(P_KA, P_CQ)
    xk = _rope(_headnorm2(pkv[:, :LANES], gka_ref[...]), ca, pa, ma, half // 2)
    ka_o[0, 0] = xk[:, :HEAD_DIM].astype(BF16)
    ka_o[0, 1] = xk[:, HEAD_DIM:].astype(BF16)
    _store_values(pkv[:, LANES:], va_o, 0)
    pc = proj(P_CQ, P_Z)
    cqn = _rms_rows(pc[:, :Q_LORA], gqb_ref[...]).astype(BF16)
    qb = _dot(cqn, wuq_ref[...])
    ckvn = _rms_rows(pc[:, Q_LORA:], gkvb_ref[...]).astype(BF16)
    kn = _dot(ckvn, wukk_ref[...])
    vv = _dot(ckvn, wukv_ref[...])
    for pr in range(B_HEADS // 2):
        _store_values(vv[:, pr * LANES:(pr + 1) * LANES], vb_o, 2 * pr)
    px = proj(P_XBC, P_END)
    dtkr = px[:, CONV_DIM:]
    lane = lax.broadcasted_iota(jnp.int32, dtkr.shape, 1)
    is_kr = (lane >= KR_LANE) & (lane < KR_LANE + MLA_ROPE)
    k_rope = _rope(jnp.where(is_kr, dtkr, 0.0), cb, pb, mb, MLA_ROPE // 4)
    scale_b = (MLA_NOPE + MLA_ROPE) ** -0.5
    for hh in range(B_HEADS):
        sl = slice(hh * LANES, (hh + 1) * LANES)
        qb_o[0, hh] = (_rope(qb[:, sl], cb, pb, mb, MLA_ROPE // 4) * scale_b).astype(BF16)
        kb_o[0, hh] = (kn[:, sl] + k_rope).astype(BF16)
    z_o[0] = proj(P_Z, P_XBC)
    xbc_o[0] = px[:, :CONV_DIM]
    dt_o[0] = jnp.where(lane < 2 * C_HEADS, dtkr, 0.0)


def _inproj(x, mod, lw, tabs, tm):
    s, l, d = x.shape
    nt = l // tm
    full = lambda shape: pl.BlockSpec(shape, lambda si, i: (0,) * len(shape))
    tab = pl.BlockSpec((tm, LANES), lambda si, i: (i, 0))
    head = lambda nh, dh: pl.BlockSpec((1, nh, tm, dh), lambda si, i: (si, 0, i, 0))
    row = lambda w: pl.BlockSpec((1, tm, w), lambda si, i: (si, i, 0))
    out_shape = (
        jax.ShapeDtypeStruct((s, A_HEADS, l, HEAD_DIM), BF16),
        jax.ShapeDtypeStruct((s, A_KV_HEADS, l, HEAD_DIM), BF16),
        jax.ShapeDtypeStruct((s, A_KV_HEADS, l, LANES), BF16),
        jax.ShapeDtypeStruct((s, B_HEADS, l, LANES), BF16),
        jax.ShapeDtypeStruct((s, B_HEADS, l, LANES), BF16),
        jax.ShapeDtypeStruct((s, B_HEADS, l, LANES), BF16),
        jax.ShapeDtypeStruct((s, l, C_INNER), F32),
        jax.ShapeDtypeStruct((s, l, CONV_DIM), F32),
        jax.ShapeDtypeStruct((s, l, LANES), F32),
    )
    out_specs = (head(A_HEADS, HEAD_DIM), head(A_KV_HEADS, HEAD_DIM), head(A_KV_HEADS, LANES),
                 head(B_HEADS, LANES), head(B_HEADS, LANES), head(B_HEADS, LANES),
                 row(C_INNER), row(CONV_DIM), row(LANES))
    return pl.pallas_call(
        _inproj_kernel,
        out_shape=out_shape,
        grid=(s, nt),
        in_specs=[row(d),
                  pl.BlockSpec((1, 6, d), lambda si, i: (si, 0, 0)),
                  full((d, P_END)), full((Q_LORA, B_HEADS * LANES)),
                  full((KV_LORA, B_HEADS * LANES)), full((KV_LORA, B_HEADS * MLA_V)),
                  full((1, LANES)), full((1, LANES)), full((1, Q_LORA)), full((1, KV_LORA)),
                  tab, tab, tab, tab, tab, tab],
        out_specs=out_specs,
        compiler_params=_params(("parallel", "parallel"), 48),
        name="inproj",
    )(x, mod, lw["w_in"], lw["w_uq"], lw["w_ukk"], lw["w_ukv"],
      lw["gqa"], lw["gka"], lw["gqb"], lw["gkvb"], *tabs)


ATT_DV = 64
ATT_Q_ROWS = 256


def _attn_kernel(q_ref, k_ref, v_ref, o_ref, *, shared_kv):
    tq = q_ref.shape[2]
    qr = min(ATT_Q_ROWS, tq)
    items = [(j, r) for j in range(2) for r in range(tq // qr)]

    def scores(j, r):
        kj = 0 if shared_kv else j
        return _dot_nt(q_ref[0, j, r * qr:(r + 1) * qr, :], k_ref[0, kj])

    s_next = scores(*items[0])
    outs = {}
    for i, (j, r) in enumerate(items):
        s = s_next
        if i + 1 < len(items):
            s_next = scores(*items[i + 1])
        kj = 0 if shared_kv else j
        p = jnp.exp(s - jnp.max(s, axis=-1, keepdims=True)).astype(BF16)
        o = _dot(p, v_ref[0, kj])
        outs[(j, r)] = o[:, :ATT_DV] / o[:, ATT_DV:ATT_DV + 1]
    for r in range(tq // qr):
        o_ref[0, r * qr:(r + 1) * qr, :] = jnp.concatenate([outs[(0, r)], outs[(1, r)]], axis=-1)


def _attention(q, k, v, tq, name):
    s, hq, l, dk = q.shape
    hk = k.shape[1]
    shared = hq // hk == 2
    kvb = 1 if shared else 2
    return pl.pallas_call(
        functools.partial(_attn_kernel, shared_kv=shared),
        out_shape=jax.ShapeDtypeStruct((s, l, hq * ATT_DV), F32),
        grid=(s, hq // 2, l // tq),
        in_specs=[pl.BlockSpec((1, 2, tq, dk), lambda si, hp, qi: (si, hp, qi, 0)),
                  pl.BlockSpec((1, kvb, l, dk), lambda si, hp, qi: (si, hp, 0, 0)),
                  pl.BlockSpec((1, kvb, l, LANES), lambda si, hp, qi: (si, hp, 0, 0))],
        out_specs=pl.BlockSpec((1, tq, 2 * ATT_DV), lambda si, hp, qi: (si, qi, hp)),
        compiler_params=_params(("parallel", "parallel", "arbitrary"), 48),
        name=name,
    )(q, k, v)


def _ssd_kernel(xbc_ref, dt_ref, z_ref, cw_ref, cb_ref, dtb_ref, alog_ref, dskip_ref, g_ref,
                exf_ref, exb_ref, bd_ref, o_ref, xpad, xact, ad_s, hf, hb):
    l = xbc_ref.shape[1]
    nc = l // SSD_Q
    q = SSD_Q
    pad = 8
    xpad[0:pad, :] = jnp.zeros((pad, CONV_DIM), F32)
    xpad[l + pad:l + 2 * pad, :] = jnp.zeros((pad, CONV_DIM), F32)
    xpad[pad:l + pad, :] = xbc_ref[0]
    def conv_chunk(c, carry):
        r0 = pl.multiple_of(c * q, q)
        for st in range(CONV_DIM // LANES):
            cols = slice(st * LANES, (st + 1) * LANES)
            win = xpad[pl.ds(r0, q + 2 * pad), cols]
            acc = jnp.zeros((q, LANES), F32) + cb_ref[:, cols]
            for k in range(D_CONV):
                lo = pad - D_CONV // 2 + k
                acc = acc + win[lo:lo + q] * cw_ref[k:k + 1, cols]
            xact[pl.ds(r0, q), cols] = _silu(acc)
        return carry

    lax.fori_loop(0, nc, conv_chunk, 0)
    dtr = dt_ref[0] + dtb_ref[...]
    dt = jnp.maximum(dtr, 0.0) + jnp.log1p(jnp.exp(-jnp.abs(dtr)))
    lane_l = lax.broadcasted_iota(jnp.int32, dt.shape, 1)
    a = dt * (-jnp.exp(alog_ref[...]))
    ad_s[...] = jnp.where(lane_l < 2 * C_HEADS, a, pltpu.roll(dt, 2 * C_HEADS, 1))

    ri = lax.broadcasted_iota(jnp.int32, (q, q), 0)
    ci = lax.broadcasted_iota(jnp.int32, (q, q), 1)
    lower = ri >= ci
    upper = ci >= ri
    tril = jnp.where(lower, 1.0, 0.0).astype(BF16)
    lane_q = lax.broadcasted_iota(jnp.int32, (q, LANES), 1)
    first8 = lane_q < C_HEADS
    mid8 = (lane_q >= C_HEADS) & (lane_q < 2 * C_HEADS)
    lo64 = lane_q < C_HEAD_DIM
    bd = bd_ref[...] > 0.5

    def chunk_decay(r0):
        ad = ad_s[pl.ds(r0, q), :]
        hi, lo = _split_bf16(ad)
        inc = _dot(tril, hi) + _dot(tril, lo)
        exc = inc - ad
        tot = inc[q - 1:q, :]
        dts = pltpu.roll(ad, LANES - 2 * C_HEADS, 1)
        return ad, inc, exc, tot, dts

    def expand(v, ex_ref):
        hi, lo = _split_bf16(v)
        return _dot(hi, ex_ref[...]) + _dot(lo, ex_ref[...])

    hf[...] = jnp.zeros_like(hf)
    hb[...] = jnp.zeros_like(hb)

    def fwd_body(c, carry):
        r0 = pl.multiple_of(c * q, q)
        ad, inc, exc, tot, dts = chunk_decay(r0)
        inc_t = inc.T
        ad_t = ad.T
        exc_t = inc_t - ad_t
        xq = xact[pl.ds(r0, q), :]
        xs = xq[:, :C_INNER]
        xs_bf = xs.astype(BF16)
        bm = xq[:, C_INNER:C_INNER + LANES]
        cm = xq[:, C_INNER + LANES:]
        bt_bf = bm.T.astype(BF16)
        cm_bf = cm.astype(BF16)
        gmat = []
        for g in range(SSM_GROUPS):
            cg = jnp.where(lo64 if g == 0 else jnp.logical_not(lo64), cm, 0.0).astype(BF16)
            gmat.append(_dot(cg, bt_bf))
        for pr in range(C_HEADS // 2):
            ys = []
            for hh in (2 * pr, 2 * pr + 1):
                icol = jnp.broadcast_to(inc[:, hh:hh + 1], (q, q))
                irow = jnp.broadcast_to(inc_t[hh:hh + 1, :], (q, q))
                lf = jnp.exp(jnp.where(lower, icol - irow, NEG_BIG)) * ad_t[16 + hh:17 + hh, :]
                ecol = jnp.broadcast_to(exc[:, 8 + hh:9 + hh], (q, q))
                erow = jnp.broadcast_to(exc_t[8 + hh:9 + hh, :], (q, q))
                lb = jnp.exp(jnp.where(upper, erow - ecol, NEG_BIG)) * ad_t[24 + hh:25 + hh, :]
                w = (gmat[hh // (C_HEADS // SSM_GROUPS)] * (lf + lb)).astype(BF16)
                ys.append(_dot(w, xs_bf[:, pr * LANES:(pr + 1) * LANES]))
            o_ref[0, pl.ds(r0, q), pr * LANES:(pr + 1) * LANES] = jnp.where(lo64, ys[0], ys[1])
        pf = expand(jnp.where(first8, jnp.exp(inc), 0.0), exf_ref)
        sf = expand(jnp.where(first8, jnp.exp(tot - inc) * dts, 0.0), exf_ref)
        y_off = _dot(cm_bf, hf[...].astype(BF16)) * pf
        o_ref[0, pl.ds(r0, q), :] = o_ref[0, pl.ds(r0, q), :] + y_off
        st = _dot(bt_bf, (xs * sf).astype(BF16))
        hf[...] = pf[q - 1:q, :] * hf[...] + jnp.where(bd, st, 0.0)
        return carry

    lax.fori_loop(0, nc // 2, lambda i, cr: fwd_body(2 * i + 1, fwd_body(2 * i, cr)), 0)

    def bwd_body(i, carry):
        c = nc - 1 - i
        r0 = pl.multiple_of(c * q, q)
        ad, inc, exc, tot, dts = chunk_decay(r0)
        xq = xact[pl.ds(r0, q), :]
        xs = xq[:, :C_INNER]
        bt_bf = xq[:, C_INNER:C_INNER + LANES].T.astype(BF16)
        cm_bf = xq[:, C_INNER + LANES:].astype(BF16)
        pbk = expand(jnp.where(mid8, jnp.exp(tot - exc), 0.0), exb_ref)
        sbk = expand(jnp.where(mid8, jnp.exp(exc) * dts, 0.0), exb_ref)
        y = o_ref[0, pl.ds(r0, q), :] + _dot(cm_bf, hb[...].astype(BF16)) * pbk
        st = _dot(bt_bf, (xs * sbk).astype(BF16))
        hb[...] = pbk[0:1, :] * hb[...] + jnp.where(bd, st, 0.0)
        y = (y + dskip_ref[...] * xs) * _silu(z_ref[0, pl.ds(r0, q), :])
        o_ref[0, pl.ds(r0, q), :] = _rms_rows(y, g_ref[...])
        return carry

    lax.fori_loop(0, nc // 2, lambda i, cr: bwd_body(2 * i + 1, bwd_body(2 * i, cr)), 0)


def _ssd(xbc, dt, z, lw, consts):
    s, l, _ = xbc.shape
    assert l % (2 * SSD_Q) == 0
    seq = lambda w: pl.BlockSpec((1, l, w), lambda si: (si, 0, 0))
    full = lambda shape: pl.BlockSpec(shape, lambda si: (0,) * len(shape))
    return pl.pallas_call(
        _ssd_kernel,
        out_shape=jax.ShapeDtypeStruct((s, l, C_INNER), F32),
        grid=(s,),
        in_specs=[seq(CONV_DIM), seq(LANES), seq(C_INNER),
                  full((D_CONV, CONV_DIM)), full((1, CONV_DIM)), full((1, LANES)), full((1, LANES)),
                  full((1, C_INNER)), full((1, C_INNER)),
                  full((LANES, C_INNER)), full((LANES, C_INNER)), full((LANES, C_INNER))],
        out_specs=seq(C_INNER),
        scratch_shapes=[pltpu.VMEM((l + 16, CONV_DIM), F32), pltpu.VMEM((l, CONV_DIM), F32),
                        pltpu.VMEM((l, LANES), F32),
                        pltpu.VMEM((LANES, C_INNER), F32), pltpu.VMEM((LANES, C_INNER), F32)],
        compiler_params=_params(("parallel",), 56),
        name="ssd",
    )(xbc, dt, z, lw["conv_w"], lw["conv_b"], lw["dt_bias"], lw["a_log"], lw["d_skip"], lw["ssm_g"],
      consts["exf"], consts["exb"], consts["bd"])


def _route(logits):
    lane = lax.broadcasted_iota(jnp.int32, logits.shape, 1)
    lane_f = lane.astype(F32)
    neg_inf = -jnp.inf
    isg = (lane >= N_EXPERTS) & (lane < N_EXPERTS + N_EGROUPS)
    gl = jnp.where(isg, logits, neg_inf)
    gmax = jnp.max(gl, axis=-1, keepdims=True)
    gsel = jnp.min(jnp.where(gl == gmax, lane_f - N_EXPERTS, 1e9), axis=-1, keepdims=True)
    gw = 1.0 / jnp.sum(jnp.where(isg, jnp.exp(gl - gmax), 0.0), axis=-1, keepdims=True)
    grp = jnp.floor(lane_f * (1.0 / EXPERTS_PER_GROUP))
    el = jnp.where((lane < N_EXPERTS) & (grp == gsel), logits, neg_inf)
    l1 = jnp.max(el, axis=-1, keepdims=True)
    i1 = jnp.min(jnp.where(el == l1, lane_f, 1e9), axis=-1, keepdims=True)
    el2 = jnp.where(lane_f == i1, neg_inf, el)
    l2 = jnp.max(el2, axis=-1, keepdims=True)
    i2 = jnp.min(jnp.where(el2 == l2, lane_f, 1e9), axis=-1, keepdims=True)
    r = jnp.exp(l2 - l1)
    g1 = gw / (1.0 + r)
    g2 = g1 * r
    return jnp.where(lane == 0, i1, jnp.where(lane == 1, i2, jnp.where(lane == 2, g1,
                                                                    jnp.where(lane == 3, g2, 0.0))))


def _outproj_kernel(oa_ref, ob_ref, oc_ref, x_ref, mod_ref, wout_ref, goa_ref, gob_ref,
                    l1g_ref, l1b_ref, wrc_ref, br_ref, x1_o, h2v_o, route_o, ct):
    oa = _rms_rows(oa_ref[0], goa_ref[...]).astype(BF16)
    ob = _rms_rows(ob_ref[0], gob_ref[...]).astype(BF16)
    oc = oc_ref[0].astype(BF16)
    wa = A_HEADS * HEAD_DIM
    wb = wa + B_HEADS * MLA_V
    y = _dot(oa, wout_ref[0:wa, :]) + _dot(ob, wout_ref[wa:wb, :]) + _dot(oc, wout_ref[wb:, :])
    m = mod_ref[0]
    x1 = _ln_rows(DN_ALPHA * x_ref[0] + (1.0 + m[2:3]) * y, l1g_ref[...], l1b_ref[...])
    x1_o[0] = x1
    h2 = x1 * (1.0 + m[4:5]) + m[3:4]
    _rows_to_tiles(h2, ct, h2v_o)
    hi, lo = _split_bf16(h2)
    hw = _dot(hi, wrc_ref[...])
    logits = hw[:, :LANES] + hw[:, LANES:] + _dot(lo, wrc_ref[:, :LANES]) + br_ref[...]
    route_o[0] = _route(logits)


def _outproj(oa, ob, oc, x, mod, lw, tm):
    s, l, d = x.shape
    nt = l // tm
    nch = d // LANES
    row = lambda w: pl.BlockSpec((1, tm, w), lambda si, i: (si, i, 0))
    full = lambda shape: pl.BlockSpec(shape, lambda si, i: (0,) * len(shape))
    return pl.pallas_call(
        _outproj_kernel,
        out_shape=(jax.ShapeDtypeStruct((s, l, d), F32), jax.ShapeDtypeStruct((s * l * nch, LANES), F32),
                   jax.ShapeDtypeStruct((s, l, LANES), F32)),
        grid=(s, l // tm),
        in_specs=[row(A_HEADS * HEAD_DIM), row(B_HEADS * MLA_V), row(C_INNER), row(d),
                  pl.BlockSpec((1, 6, d), lambda si, i: (si, 0, 0)),
                  full((d, d)), full((1, A_HEADS * HEAD_DIM)), full((1, B_HEADS * MLA_V)),
                  full((1, d)), full((1, d)), full((d, 2 * LANES)), full((1, LANES))],
        out_specs=(row(d), pl.BlockSpec((tm * nch, LANES), lambda si, i: (si * nt + i, 0)), row(LANES)),
        scratch_shapes=[pltpu.VMEM((nch * _pitch(tm), LANES), F32)],
        compiler_params=_params(("parallel", "parallel"), 48),
        name="outproj",
    )(oa, ob, oc, x, mod, lw["w_out"], lw["goa"], lw["gob"], lw["ln1_g"], lw["ln1_b"],
      lw["w_r"], lw["b_r"])


MOE_ROWS = 128
ROW_STRIDE = _pitch(MOE_ROWS)


def _slot_rows(l):
    return 2 * l + N_EXPERTS * 8


def _route_sort_kernel(route_ref, slots_o, meta_o):
    r = route_ref[0]
    l = r.shape[0]
    lane = lax.broadcasted_iota(jnp.int32, (l, LANES), 1)
    lane_f = lane.astype(F32)
    oh1 = lane_f == jnp.broadcast_to(r[:, 0:1], (l, LANES))
    oh2 = lane_f == jnp.broadcast_to(r[:, 1:2], (l, LANES))
    member = jnp.where(oh1 | oh2, 1.0, 0.0).astype(BF16)
    tb = min(256, l)
    ri = lax.broadcasted_iota(jnp.int32, (tb, tb), 0)
    ci = lax.broadcasted_iota(jnp.int32, (tb, tb), 1)
    stril = jnp.where(ri > ci, 1.0, 0.0).astype(BF16)
    run = jnp.zeros((1, LANES), F32)
    ranks = []
    for i in range(l // tb):
        blk = member[i * tb:(i + 1) * tb]
        ranks.append(_dot(stril, blk) + run)
        run = run + jnp.sum(blk.astype(F32), axis=0, keepdims=True)
    rank_all = jnp.concatenate(ranks, axis=0)
    padded = jnp.floor((run + 7.0) * 0.125) * 8.0
    hi, lo = _split_bf16(jnp.broadcast_to(padded, (8, LANES)))
    rl = lax.broadcasted_iota(jnp.int32, (LANES, LANES), 0)
    cl = lax.broadcasted_iota(jnp.int32, (LANES, LANES), 1)
    upper = jnp.where(rl < cl, 1.0, 0.0).astype(BF16)
    base = (_dot(hi, upper) + _dot(lo, upper))[0:1]
    pos = base + rank_all
    slot1 = jnp.sum(jnp.where(oh1, pos, 0.0), axis=-1, keepdims=True)
    slot2 = jnp.sum(jnp.where(oh2, pos, 0.0), axis=-1, keepdims=True)
    packed = jnp.where(lane == 0, slot1, jnp.where(lane == 1, slot2, 0.0))
    slots_o[0] = packed.T[0:2].astype(jnp.int32)
    meta_o[0] = jnp.concatenate([base, run, jnp.zeros((6, LANES), F32)], axis=0).astype(jnp.int32)


def _route_sort(route):
    s, l, _ = route.shape
    return pl.pallas_call(
        _route_sort_kernel,
        out_shape=(jax.ShapeDtypeStruct((s, 2, l), jnp.int32), jax.ShapeDtypeStruct((s, 8, LANES), jnp.int32)),
        grid=(s,),
        in_specs=[pl.BlockSpec((1, l, LANES), lambda si: (si, 0, 0))],
        out_specs=(pl.BlockSpec((1, 2, l), lambda si: (si, 0, 0)), pl.BlockSpec((1, 8, LANES), lambda si: (si, 0, 0))),
        compiler_params=_params(("parallel",), 32),
        name="route_sort",
    )(route)


MOE_EXPERTS_PER_STEP = 2
MOE_EXPERT_STEPS = N_EXPERTS // MOE_EXPERTS_PER_STEP


def _moe_kernel(meta_ref, h2v_ref, slots_hbm, wg_ref, wu_ref, wd_ref,
                x1_ref, route_ref, mod_ref, l2g_ref, l2b_ref, o_ref,
                slot_s, tok_s, y_v, xt, yt, ct1, ct2, sem):
    s = pl.program_id(0)
    e = pl.program_id(1)
    l = h2v_ref.shape[0] // 8
    nch = D_MODEL // LANES

    @pl.when(e == 0)
    def _():
        c_slots = pltpu.make_async_copy(slots_hbm.at[s], slot_s, sem.at[0])
        c_slots.start()
        xt[...] = jnp.zeros_like(xt)

        def clear(ex, c):
            n = meta_ref[s, 1, ex]
            last = meta_ref[s, 0, ex] + ((n + 7) // 8) * 8 - 8

            @pl.when(n > 0)
            def _():
                for j in range(8):
                    tok_s[last + j] = 2 * l

            return c

        lax.fori_loop(0, N_EXPERTS, clear, 0)
        c_slots.wait()

        def invert(g, c):
            for j in range(8):
                t = g * 8 + j
                for k in range(2):
                    tok_s[slot_s[k, t]] = 2 * t + k
            return c

        lax.fori_loop(0, l // 8, invert, 0)

    grp_per_blk = MOE_ROWS // 8
    for sub in range(MOE_EXPERTS_PER_STEP):
        ex = e * MOE_EXPERTS_PER_STEP + sub
        base = meta_ref[s, 0, ex]
        ngrp = (meta_ref[s, 1, ex] + 7) // 8

        def block(blk, c, sub=sub, base=base, ngrp=ngrp):
            g_lo = blk * grp_per_blk
            ng = jnp.minimum(grp_per_blk, ngrp - g_lo)

            def gather(g, cc):
                r0 = base + (g_lo + g) * 8
                for j in range(8):
                    t = jnp.minimum(lax.shift_right_logical(tok_s[r0 + j], 1), l - 1)
                    row = h2v_ref[pl.ds(pl.multiple_of(t * 8, 8), 8), :]
                    xt[pl.ds(g * 8 + j, nch, stride=ROW_STRIDE), :] = row
                return cc

            lax.fori_loop(0, ng, gather, 0)
            xb = jnp.concatenate([xt[j * ROW_STRIDE:j * ROW_STRIDE + MOE_ROWS, :] for j in range(nch)],
                                 axis=-1).astype(BF16)
            mid = (_silu(_dot(xb, wg_ref[sub])) * _dot(xb, wu_ref[sub])).astype(BF16)
            y = _dot(mid, wd_ref[sub])
            for j in range(nch):
                yt[j * ROW_STRIDE:j * ROW_STRIDE + MOE_ROWS, :] = y[:, j * LANES:(j + 1) * LANES]

            def scatter(g, cc):
                r0 = base + (g_lo + g) * 8
                for j in range(8):
                    dst = pl.multiple_of(tok_s[r0 + j] * 8, 8)
                    y_v[pl.ds(dst, 8), :] = yt[pl.ds(g * 8 + j, nch, stride=ROW_STRIDE), :]
                return cc

            lax.fori_loop(0, ng, scatter, 0)
            return c

        lax.fori_loop(0, (ngrp + grp_per_blk - 1) // grp_per_blk, block, 0)

    @pl.when(e >= MOE_EXPERT_STEPS)
    def _():
        tmo = x1_ref.shape[1]
        t0 = (e - MOE_EXPERT_STEPS) * tmo
        r = route_ref[0]
        y = (r[:, 2:3] * _tiles_to_rows(y_v, t0, tmo, ct1, 2, 0)
             + r[:, 3:4] * _tiles_to_rows(y_v, t0, tmo, ct2, 2, 1))
        m = mod_ref[0]
        o_ref[0] = _ln_rows(DN_ALPHA * x1_ref[0] + (1.0 + m[5:6]) * y, l2g_ref[...], l2b_ref[...])


def _moe(h2v, slots, meta, x1, route, mod, lw, tm):
    s, l, d = x1.shape
    n_slots = _slot_rows(l)
    eps = MOE_EXPERTS_PER_STEP
    wspec = lambda shape: pl.BlockSpec(shape, lambda si, e: (jnp.minimum(e, MOE_EXPERT_STEPS - 1), 0, 0))
    blk_idx = lambda si, e: (si, jnp.maximum(e - MOE_EXPERT_STEPS, 0), 0)
    tok_blk = pl.BlockSpec((1, tm, d), blk_idx)
    vec = pl.BlockSpec((1, d), lambda si, e: (0, 0))
    return pl.pallas_call(
        _moe_kernel,
        out_shape=jax.ShapeDtypeStruct((s, l, d), F32),
        grid=(s, MOE_EXPERT_STEPS + l // tm),
        in_specs=[pl.BlockSpec(memory_space=pltpu.SMEM),
                  pl.BlockSpec((l * 8, LANES), lambda si, e: (si, 0)),
                  pl.BlockSpec(memory_space=pl.ANY),
                  wspec((eps, d, D_FF_EXPERT)), wspec((eps, d, D_FF_EXPERT)), wspec((eps, D_FF_EXPERT, d)),
                  tok_blk, pl.BlockSpec((1, tm, LANES), blk_idx),
                  pl.BlockSpec((1, 6, d), lambda si, e: (si, 0, 0)), vec, vec],
        out_specs=tok_blk,
        scratch_shapes=[pltpu.SMEM((2, l), jnp.int32), pltpu.SMEM((n_slots,), jnp.int32),
                        pltpu.VMEM(((2 * l + 1) * 8, LANES), F32),
                        pltpu.VMEM((8 * ROW_STRIDE, LANES), F32), pltpu.VMEM((8 * ROW_STRIDE, LANES), F32),
                        pltpu.VMEM((8 * _pitch(tm), LANES), F32), pltpu.VMEM((8 * _pitch(tm), LANES), F32),
                        pltpu.SemaphoreType.DMA((1,))],
        compiler_params=_params(("parallel", "arbitrary"), 60),
        name="moe",
    )(meta, h2v, slots, lw["w_e_gate"], lw["w_e_up"], lw["w_e_down"], x1, route, mod,
      lw["ln2_g"], lw["ln2_b"])


def _rope_tables(seq_len):
    t = jnp.arange(seq_len, dtype=jnp.int32)
    row = (t // GRID_W).astype(F32)[:, None]
    col = (t % GRID_W).astype(F32)[:, None]

    def axis_tables(d_axis):
        inv = ROPE_THETA ** (-jnp.arange(0, d_axis, 2, dtype=F32) / d_axis)
        zero = jnp.zeros((seq_len, d_axis // 2), F32)
        cs, ps, ms = [], [], []
        for pos in (row, col):
            ang = pos * inv
            c, sn = jnp.cos(ang), jnp.sin(ang)
            cs += [c, c]
            ps += [zero, sn]
            ms += [-sn, zero]
        return [jnp.concatenate(v, axis=-1) for v in (cs, ps, ms)]

    ta = [jnp.tile(v, (1, 2)) for v in axis_tables(HEAD_DIM // 2)]
    one = jnp.ones((seq_len, MLA_NOPE), F32)
    zero = jnp.zeros((seq_len, MLA_NOPE), F32)
    pad1 = jnp.ones((seq_len, LANES - MLA_NOPE - MLA_ROPE), F32)
    pad0 = jnp.zeros((seq_len, LANES - MLA_NOPE - MLA_ROPE), F32)
    cb, pb, mb = axis_tables(MLA_ROPE // 2)
    tb = [jnp.concatenate([one, cb, pad1], -1), jnp.concatenate([zero, pb, pad0], -1),
          jnp.concatenate([zero, mb, pad0], -1)]
    return ta + tb


def _ssd_consts():
    r = np.arange(LANES)[:, None]
    c = np.arange(C_INNER)[None, :]
    exf = ((r < C_HEADS) & (c // C_HEAD_DIM == r)).astype(np.float32)
    exb = ((r >= C_HEADS) & (r < 2 * C_HEADS) & (c // C_HEAD_DIM == r - C_HEADS)).astype(np.float32)
    heads_per_group = C_HEADS // SSM_GROUPS
    bd = (r // D_STATE == c // (C_HEAD_DIM * heads_per_group)).astype(np.float32)
    return {"exf": jnp.asarray(exf, BF16), "exb": jnp.asarray(exb, BF16), "bd": jnp.asarray(bd, F32)}


def _pack_weights(p):
    w_in = p["w_in"]
    depth = w_in.shape[0]
    off = np.cumsum((0, 256, 128, 128, Q_LORA, KV_LORA, MLA_ROPE, C_INNER, CONV_DIM, 2 * C_HEADS))
    w_kr = w_in[:, :, off[5]:off[6]]
    zpad = lambda n: jnp.zeros((depth, D_MODEL, n), F32)
    dtkr = jnp.concatenate([w_in[:, :, off[8]:], zpad(KR_LANE - 2 * C_HEADS), w_kr,
                            zpad(LANES - KR_LANE - MLA_ROPE)], -1)
    w_in_p = jnp.concatenate([w_in[:, :, :off[5]], w_in[:, :, off[6]:off[8]], dtkr], -1).astype(BF16)
    dq = MLA_NOPE + MLA_ROPE
    w_uq = p["w_uq"].reshape(depth, Q_LORA, B_HEADS, dq)
    w_uq_p = jnp.concatenate([w_uq, jnp.zeros((depth, Q_LORA, B_HEADS, LANES - dq), F32)], -1)
    w_ukv = p["w_ukv"].reshape(depth, KV_LORA, B_HEADS, MLA_NOPE + MLA_V)
    w_ukk = jnp.concatenate([w_ukv[..., :MLA_NOPE],
                             jnp.zeros((depth, KV_LORA, B_HEADS, LANES - MLA_NOPE), F32)], -1)
    pad_lanes = lambda v: jnp.concatenate([v, jnp.zeros(v.shape[:-1] + (LANES - v.shape[-1],), F32)], -1)
    w_r = pad_lanes(jnp.concatenate([p["w_re"], p["w_rg"]], -1))
    w_r_hi = w_r.astype(BF16)
    return {
        "w_ada": p["w_ada"].astype(BF16), "b_ada": p["b_ada"],
        "w_in": w_in_p,
        "w_uq": w_uq_p.reshape(depth, Q_LORA, B_HEADS * LANES).astype(BF16),
        "w_ukk": w_ukk.reshape(depth, KV_LORA, B_HEADS * LANES).astype(BF16),
        "w_ukv": w_ukv[..., MLA_NOPE:].reshape(depth, KV_LORA, B_HEADS * MLA_V).astype(BF16),
        "gqa": jnp.tile(p["qa_norm_g"], (1, 2))[:, None, :], "gka": jnp.tile(p["ka_norm_g"], (1, 2))[:, None, :],
        "gqb": p["qb_norm_g"][:, None, :], "gkvb": p["kvb_norm_g"][:, None, :],
        "goa": p["oa_norm_g"][:, None, :], "gob": p["ob_norm_g"][:, None, :],
        "conv_w": p["conv_w"], "conv_b": p["conv_b"][:, None, :],
        "dt_bias": pad_lanes(p["dt_bias"].reshape(depth, 2 * C_HEADS))[:, None, :],
        "a_log": pad_lanes(p["a_log"].reshape(depth, 2 * C_HEADS))[:, None, :],
        "d_skip": jnp.repeat(p["d_skip"], C_HEAD_DIM, axis=-1)[:, None, :],
        "ssm_g": p["ssm_norm_g"][:, None, :],
        "w_out": p["w_out"].astype(BF16),
        "ln1_g": p["ln1_g"][:, None, :], "ln1_b": p["ln1_b"][:, None, :],
        "w_r": jnp.concatenate([w_r_hi, (w_r - w_r_hi.astype(F32)).astype(BF16)], -1),
        "b_r": pad_lanes(jnp.concatenate([p["b_re"], p["b_rg"]], -1))[:, None, :],
        "w_e_gate": p["w_e_gate"].astype(BF16), "w_e_up": p["w_e_up"].astype(BF16),
        "w_e_down": p["w_e_down"].astype(BF16),
        "ln2_g": p["ln2_g"][:, None, :], "ln2_b": p["ln2_b"][:, None, :],
    }


def _trunk(x, c, p, tm=512, tq=512):
    s, l, d = x.shape
    tm = min(tm, l)
    tq = min(tq, l)
    pw = _pack_weights(p)
    tabs = _rope_tables(l)
    consts = _ssd_consts()
    mods = _ada(c, pw["w_ada"], pw["b_ada"]).reshape(DEPTH, s, 6, d)
    x = _ln0(x.reshape(s * l, d), p["ln0_g"][None, :], p["ln0_b"][None, :], tm).reshape(s, l, d)
    for layer in range(DEPTH):
        lw = {k: v[layer] for k, v in pw.items() if k not in ("w_ada", "b_ada")}
        mod = mods[layer]
        qa, ka, va, qb, kb, vb, z, xbc, dt = _inproj(x, mod, lw, tabs, tm)
        oa = _attention(qa, ka, va, tq, "attn_a")
        ob = _attention(qb, kb, vb, tq, "attn_b")
        oc = _ssd(xbc, dt, z, lw, consts)
        x1, h2v, route = _outproj(oa, ob, oc, x, mod, lw, tm)
        slots, meta = _route_sort(route)
        x = _moe(h2v, slots, meta, x1, route, mod, lw, tm)
    return x


def kernel(x_prompt, x_sample, c_prompt, c_sample, ln0_g, ln0_b, w_ada, b_ada, w_in, qa_norm_g, ka_norm_g, oa_norm_g, qb_norm_g, w_uq, kvb_norm_g, w_ukv, ob_norm_g, conv_w, conv_b, dt_bias, a_log, d_skip, ssm_norm_g, w_out, ln1_g, ln1_b, w_rg, b_rg, w_re, b_re, w_e_gate, w_e_up, w_e_down, ln2_g, ln2_b):
    p = dict(ln0_g=ln0_g, ln0_b=ln0_b, w_ada=w_ada, b_ada=b_ada, w_in=w_in,
             qa_norm_g=qa_norm_g, ka_norm_g=ka_norm_g, oa_norm_g=oa_norm_g,
             qb_norm_g=qb_norm_g, w_uq=w_uq, kvb_norm_g=kvb_norm_g, w_ukv=w_ukv, ob_norm_g=ob_norm_g,
             conv_w=conv_w, conv_b=conv_b, dt_bias=dt_bias, a_log=a_log, d_skip=d_skip,
             ssm_norm_g=ssm_norm_g, w_out=w_out, ln1_g=ln1_g, ln1_b=ln1_b,
             w_rg=w_rg, b_rg=b_rg, w_re=w_re, b_re=b_re,
             w_e_gate=w_e_gate, w_e_up=w_e_up, w_e_down=w_e_down, ln2_g=ln2_g, ln2_b=ln2_b)
    nb = x_prompt.shape[0]
    y = _trunk(jnp.concatenate([x_prompt, x_sample], 0), jnp.concatenate([c_prompt, c_sample], 0), p)
    return (y[:nb], y[nb:])
```

```python
import functools

import jax
import jax.numpy as jnp
import numpy as np
from jax import lax
from jax.experimental import pallas as pl
from jax.experimental.pallas import tpu as pltpu

F32 = jnp.float32
BF16 = jnp.bfloat16

D_MODEL = 1024
DEPTH = 4
GRID_W = 64
ROPE_THETA = 10000.0
HEAD_DIM = 64
A_HEADS = 4
A_KV_HEADS = 2
B_HEADS = 4
Q_LORA = 384
KV_LORA = 128
MLA_NOPE = 64
MLA_ROPE = 32
MLA_V = 64
C_HEADS = 8
C_HEAD_DIM = 64
SSM_GROUPS = 2
D_STATE = 64
D_CONV = 5
C_INNER = C_HEADS * C_HEAD_DIM
CONV_DIM = C_INNER + 2 * SSM_GROUPS * D_STATE
N_EGROUPS = 4
EXPERTS_PER_GROUP = 8
N_EXPERTS = N_EGROUPS * EXPERTS_PER_GROUP
D_FF_EXPERT = 256
DN_ALPHA = (2 * DEPTH) ** 0.25
EPS = 1e-6

LANES = 128
SSD_Q = 128
NEG_BIG = -1e30

P_QA, P_KA, P_VA, P_CQ, P_CKV, P_Z, P_XBC, P_DTKR, P_END = (
    0, 256, 384, 512, 896, 1024, 1536, 2304, 2432)
KR_LANE = MLA_NOPE


def _params(sem, vmem_mb):
    return pltpu.CompilerParams(dimension_semantics=sem, vmem_limit_bytes=vmem_mb * 1024 * 1024)


def _silu(x):
    return x * (1.0 / (1.0 + jnp.exp(-x)))


def _ln_rows(x, g, b):
    mu = jnp.mean(x, axis=-1, keepdims=True)
    xc = x - mu
    var = jnp.mean(xc * xc, axis=-1, keepdims=True)
    return xc * lax.rsqrt(var + EPS) * g + b


def _rms_rows(x, g):
    return x * lax.rsqrt(jnp.mean(x * x, axis=-1, keepdims=True) + EPS) * g


def _split_bf16(x):
    hi = x.astype(BF16)
    lo = (x - hi.astype(F32)).astype(BF16)
    return hi, lo


def _dot(a, b):
    return jnp.dot(a, b, preferred_element_type=F32)


def _dot_nt(a, b):
    return lax.dot_general(a, b, (((1,), (1,)), ((), ())), preferred_element_type=F32)


def _pitch(rows):
    return rows + 8


def _rows_to_tiles(x, ct, out_ref):
    m = x.shape[0]
    nch = x.shape[1] // LANES
    p = _pitch(m)
    for j in range(nch):
        ct[j * p:j * p + m, :] = x[:, j * LANES:(j + 1) * LANES]

    def grp(g, c):
        for j in range(8):
            r = g * 8 + j
            out_ref[pl.ds(pl.multiple_of(r * 8, 8), 8), :] = ct[pl.ds(r, nch, stride=p), :]
        return c

    lax.fori_loop(0, m // 8, grp, 0)


def _tiles_to_rows(src_ref, row0, m, ct, step=1, offset=0):
    nch = D_MODEL // LANES
    p = _pitch(m)

    def grp(g, c):
        for j in range(8):
            r = g * 8 + j
            src = pl.multiple_of(((row0 + r) * step + offset) * 8, 8)
            ct[pl.ds(r, nch, stride=p), :] = src_ref[pl.ds(src, 8), :]
        return c

    lax.fori_loop(0, m // 8, grp, 0)
    return jnp.concatenate([ct[j * p:j * p + m, :] for j in range(nch)], axis=-1)


def _ln0_kernel(x_ref, g_ref, b_ref, o_ref):
    o_ref[...] = _ln_rows(x_ref[...], g_ref[...], b_ref[...])


def _ln0(x2d, g, b, tm):
    t, d = x2d.shape
    return pl.pallas_call(
        _ln0_kernel,
        out_shape=jax.ShapeDtypeStruct((t, d), F32),
        grid=(t // tm,),
        in_specs=[pl.BlockSpec((tm, d), lambda i: (i, 0)),
                  pl.BlockSpec((1, d), lambda i: (0, 0)),
                  pl.BlockSpec((1, d), lambda i: (0, 0))],
        out_specs=pl.BlockSpec((tm, d), lambda i: (i, 0)),
        compiler_params=_params(("parallel",), 32),
        name="ln0",
    )(x2d, g, b)


def _ada_kernel(c_ref, w_ref, b_ref, o_ref):
    cs = _silu(c_ref[...]).astype(BF16)
    o_ref[0] = _dot(cs, w_ref[0]) + b_ref[0]


def _ada(c, w_bf, b, tn=2048):
    s, d = c.shape
    depth, _, n = w_bf.shape
    return pl.pallas_call(
        _ada_kernel,
        out_shape=jax.ShapeDtypeStruct((depth, s, n), F32),
        grid=(depth, n // tn),
        in_specs=[pl.BlockSpec((s, d), lambda l, j: (0, 0)),
                  pl.BlockSpec((1, d, tn), lambda l, j: (l, 0, j)),
                  pl.BlockSpec((1, 1, tn), lambda l, j: (l, 0, j))],
        out_specs=pl.BlockSpec((1, s, tn), lambda l, j: (l, 0, j)),
        compiler_params=_params(("parallel", "parallel"), 32),
        name="ada",
    )(c, w_bf, b.reshape(depth, 1, n))


def _rope(x, c, p, m, shift):
    return x * c + pltpu.roll(x, shift, 1) * p + pltpu.roll(x, LANES - shift, 1) * m


def _headnorm2(x, g):
    xx = x * x
    lo = lax.broadcasted_iota(jnp.int32, x.shape, 1) < HEAD_DIM
    s0 = jnp.sum(jnp.where(lo, xx, 0.0), axis=-1, keepdims=True)
    s1 = jnp.sum(jnp.where(lo, 0.0, xx), axis=-1, keepdims=True)
    inv = jnp.where(lo, lax.rsqrt(s0 * (1.0 / HEAD_DIM) + EPS), lax.rsqrt(s1 * (1.0 / HEAD_DIM) + EPS))
    return x * inv * g


def _store_values(v2, v_o, h0):
    lo = lax.broadcasted_iota(jnp.int32, v2.shape, 1) < ATT_DV
    v_o[0, h0] = jnp.where(lo, v2, 1.0).astype(BF16)
    v_o[0, h0 + 1] = jnp.where(lo, pltpu.roll(v2, ATT_DV, 1), 1.0).astype(BF16)


def _inproj_kernel(x_ref, mod_ref, w_ref, wuq_ref, wukk_ref, wukv_ref,
                   gqa_ref, gka_ref, gqb_ref, gkvb_ref,
                   ca_ref, pa_ref, ma_ref, cb_ref, pb_ref, mb_ref,
                   qa_o, ka_o, va_o, qb_o, kb_o, vb_o, z_o, xbc_o, dt_o):
    m = mod_ref[0]
    h = (x_ref[0] * (1.0 + m[1:2]) + m[0:1]).astype(BF16)

    def proj(a, b):
        return _dot(h, w_ref[:, a:b])

    ca, pa, ma = ca_ref[...], pa_ref[...], ma_ref[...]
    cb, pb, mb = cb_ref[...], pb_ref[...], mb_ref[...]
    half = HEAD_DIM // 2
    pq = proj(P_QA, P_KA)
    for ch in range(A_HEADS // 2):
        xa = _headnorm2(pq[:, ch * LANES:(ch + 1) * LANES], gqa_ref[...])
        xa = _rope(xa, ca, pa, ma, half // 2) * (HEAD_DIM ** -0.5)
        qa_o[0, 2 * ch] = xa[:, :HEAD_DIM].astype(BF16)
        qa_o[0, 2 * ch + 1] = xa[:, HEAD_DIM:].astype(BF16)
    pkv = proj(P_KA, P_CQ)
    xk = _rope(_headnorm2(pkv[:, :LANES], gka_ref[...]), ca, pa, ma, half // 2)
    ka_o[0, 0] = xk[:, :HEAD_DIM].astype(BF16)
    ka_o[0, 1] = xk[:, HEAD_DIM:].astype(BF16)
    _store_values(pkv[:, LANES:], va_o, 0)
    pc = proj(P_CQ, P_Z)
    cqn = _rms_rows(pc[:, :Q_LORA], gqb_ref[...]).astype(BF16)
    qb = _dot(cqn, wuq_ref[...])
    ckvn = _rms_rows(pc[:, Q_LORA:], gkvb_ref[...]).astype(BF16)
    kn = _dot(ckvn, wukk_ref[...])
    vv = _dot(ckvn, wukv_ref[...])
    for pr in range(B_HEADS // 2):
        _store_values(vv[:, pr * LANES:(pr + 1) * LANES], vb_o, 2 * pr)
    px = proj(P_XBC, P_END)
    dtkr = px[:, CONV_DIM:]
    lane = lax.broadcasted_iota(jnp.int32, dtkr.shape, 1)
    is_kr = (lane >= KR_LANE) & (lane < KR_LANE + MLA_ROPE)
    k_rope = _rope(jnp.where(is_kr, dtkr, 0.0), cb, pb, mb, MLA_ROPE // 4)
    scale_b = (MLA_NOPE + MLA_ROPE) ** -0.5
    for hh in range(B_HEADS):
        sl = slice(hh * LANES, (hh + 1) * LANES)
        qb_o[0, hh] = (_rope(qb[:, sl], cb, pb, mb, MLA_ROPE // 4) * scale_b).astype(BF16)
        kb_o[0, hh] = (kn[:, sl] + k_rope).astype(BF16)
    z_o[0] = proj(P_Z, P_XBC)
    xbc_o[0] = px[:, :CONV_DIM]
    dt_o[0] = jnp.where(lane < 2 * C_HEADS, dtkr, 0.0)


def _inproj(x, mod, lw, tabs, tm):
    s, l, d = x.shape
    nt = l // tm
    full = lambda shape: pl.BlockSpec(shape, lambda si, i: (0,) * len(shape))
    tab = pl.BlockSpec((tm, LANES), lambda si, i: (i, 0))
    head = lambda nh, dh: pl.BlockSpec((1, nh, tm, dh), lambda si, i: (si, 0, i, 0))
    row = lambda w: pl.BlockSpec((1, tm, w), lambda si, i: (si, i, 0))
    out_shape = (
        jax.ShapeDtypeStruct((s, A_HEADS, l, HEAD_DIM), BF16),
        jax.ShapeDtypeStruct((s, A_KV_HEADS, l, HEAD_DIM), BF16),
        jax.ShapeDtypeStruct((s, A_KV_HEADS, l, LANES), BF16),
        jax.ShapeDtypeStruct((s, B_HEADS, l, LANES), BF16),
        jax.ShapeDtypeStruct((s, B_HEADS, l, LANES), BF16),
        jax.ShapeDtypeStruct((s, B_HEADS, l, LANES), BF16),
        jax.ShapeDtypeStruct((s, l, C_INNER), F32),
        jax.ShapeDtypeStruct((s, l, CONV_DIM), F32),
        jax.ShapeDtypeStruct((s, l, LANES), F32),
    )
    out_specs = (head(A_HEADS, HEAD_DIM), head(A_KV_HEADS, HEAD_DIM), head(A_KV_HEADS, LANES),
                 head(B_HEADS, LANES), head(B_HEADS, LANES), head(B_HEADS, LANES),
                 row(C_INNER), row(CONV_DIM), row(LANES))
    return pl.pallas_call(
        _inproj_kernel,
        out_shape=out_shape,
        grid=(s, nt),
        in_specs=[row(d),
                  pl.BlockSpec((1, 6, d), lambda si, i: (si, 0, 0)),
                  full((d, P_END)), full((Q_LORA, B_HEADS * LANES)),
                  full((KV_LORA, B_HEADS * LANES)), full((KV_LORA, B_HEADS * MLA_V)),
                  full((1, LANES)), full((1, LANES)), full((1, Q_LORA)), full((1, KV_LORA)),
                  tab, tab, tab, tab, tab, tab],
        out_specs=out_specs,
        compiler_params=_params(("parallel", "parallel"), 48),
        name="inproj",
    )(x, mod, lw["w_in"], lw["w_uq"], lw["w_ukk"], lw["w_ukv"],
      lw["gqa"], lw["gka"], lw["gqb"], lw["gkvb"], *tabs)


ATT_DV = 64
ATT_Q_ROWS = 256


def _attn_kernel(q_ref, k_ref, v_ref, o_ref, *, shared_kv):
    tq = q_ref.shape[2]
    qr = min(ATT_Q_ROWS, tq)
    items = [(j, r) for j in range(2) for r in range(tq // qr)]

    def scores(j, r):
        kj = 0 if shared_kv else j
        return _dot_nt(q_ref[0, j, r * qr:(r + 1) * qr, :], k_ref[0, kj])

    s_next = scores(*items[0])
    outs = {}
    for i, (j, r) in enumerate(items):
        s = s_next
        if i + 1 < len(items):
            s_next = scores(*items[i + 1])
        kj = 0 if shared_kv else j
        p = jnp.exp(s - jnp.max(s, axis=-1, keepdims=True)).astype(BF16)
        o = _dot(p, v_ref[0, kj])
        outs[(j, r)] = o[:, :ATT_DV] / o[:, ATT_DV:ATT_DV + 1]
    for r in range(tq // qr):
        o_ref[0, r * qr:(r + 1) * qr, :] = jnp.concatenate([outs[(0, r)], outs[(1, r)]], axis=-1)


def _attention(q, k, v, tq, name):
    s, hq, l, dk = q.shape
    hk = k.shape[1]
    shared = hq // hk == 2
    kvb = 1 if shared else 2
    return pl.pallas_call(
        functools.partial(_attn_kernel, shared_kv=shared),
        out_shape=jax.ShapeDtypeStruct((s, l, hq * ATT_DV), F32),
        grid=(s, hq // 2, l // tq),
        in_specs=[pl.BlockSpec((1, 2, tq, dk), lambda si, hp, qi: (si, hp, qi, 0)),
                  pl.BlockSpec((1, kvb, l, dk), lambda si, hp, qi: (si, hp, 0, 0)),
                  pl.BlockSpec((1, kvb, l, LANES), lambda si, hp, qi: (si, hp, 0, 0))],
        out_specs=pl.BlockSpec((1, tq, 2 * ATT_DV), lambda si, hp, qi: (si, qi, hp)),
        compiler_params=_params(("parallel", "parallel", "arbitrary"), 48),
        name=name,
    )(q, k, v)


def _ssd_kernel(xbc_ref, dt_ref, z_ref, cw_ref, cb_ref, dtb_ref, alog_ref, dskip_ref, g_ref,
                exf_ref, exb_ref, bd_ref, o_ref, xpad, xact, ad_s, hf, hb):
    l = xbc_ref.shape[1]
    nc = l // SSD_Q
    q = SSD_Q
    pad = 8
    xpad[0:pad, :] = jnp.zeros((pad, CONV_DIM), F32)
    xpad[l + pad:l + 2 * pad, :] = jnp.zeros((pad, CONV_DIM), F32)
    xpad[pad:l + pad, :] = xbc_ref[0]
    def conv_chunk(c, carry):
        r0 = pl.multiple_of(c * q, q)
        for st in range(CONV_DIM // LANES):
            cols = slice(st * LANES, (st + 1) * LANES)
            win = xpad[pl.ds(r0, q + 2 * pad), cols]
            acc = jnp.zeros((q, LANES), F32) + cb_ref[:, cols]
            for k in range(D_CONV):
                lo = pad - D_CONV // 2 + k
                acc = acc + win[lo:lo + q] * cw_ref[k:k + 1, cols]
            xact[pl.ds(r0, q), cols] = _silu(acc)
        return carry

    lax.fori_loop(0, nc, conv_chunk, 0)
    dtr = dt_ref[0] + dtb_ref[...]
    dt = jnp.maximum(dtr, 0.0) + jnp.log1p(jnp.exp(-jnp.abs(dtr)))
    lane_l = lax.broadcasted_iota(jnp.int32, dt.shape, 1)
    a = dt * (-jnp.exp(alog_ref[...]))
    ad_s[...] = jnp.where(lane_l < 2 * C_HEADS, a, pltpu.roll(dt, 2 * C_HEADS, 1))

    ri = lax.broadcasted_iota(jnp.int32, (q, q), 0)
    ci = lax.broadcasted_iota(jnp.int32, (q, q), 1)
    lower = ri >= ci
    upper = ci >= ri
    tril = jnp.where(lower, 1.0, 0.0).astype(BF16)
    lane_q = lax.broadcasted_iota(jnp.int32, (q, LANES), 1)
    first8 = lane_q < C_HEADS
    mid8 = (lane_q >= C_HEADS) & (lane_q < 2 * C_HEADS)
    lo64 = lane_q < C_HEAD_DIM
    bd = bd_ref[...] > 0.5

    def chunk_decay(r0):
        ad = ad_s[pl.ds(r0, q), :]
        hi, lo = _split_bf16(ad)
        inc = _dot(tril, hi) + _dot(tril, lo)
        exc = inc - ad
        tot = inc[q - 1:q, :]
        dts = pltpu.roll(ad, LANES - 2 * C_HEADS, 1)
        return ad, inc, exc, tot, dts

    def expand(v, ex_ref):
        hi, lo = _split_bf16(v)
        return _dot(hi, ex_ref[...]) + _dot(lo, ex_ref[...])

    hf[...] = jnp.zeros_like(hf)
    hb[...] = jnp.zeros_like(hb)

    def fwd_body(c, carry):
        r0 = pl.multiple_of(c * q, q)
        ad, inc, exc, tot, dts = chunk_decay(r0)
        inc_t = inc.T
        ad_t = ad.T
        exc_t = inc_t - ad_t
        xq = xact[pl.ds(r0, q), :]
        xs = xq[:, :C_INNER]
        xs_bf = xs.astype(BF16)
        bm = xq[:, C_INNER:C_INNER + LANES]
        cm = xq[:, C_INNER + LANES:]
        bt_bf = bm.T.astype(BF16)
        cm_bf = cm.astype(BF16)
        gmat = []
        for g in range(SSM_GROUPS):
            cg = jnp.where(lo64 if g == 0 else jnp.logical_not(lo64), cm, 0.0).astype(BF16)
            gmat.append(_dot(cg, bt_bf))
        for pr in range(C_HEADS // 2):
            ys = []
            for hh in (2 * pr, 2 * pr + 1):
                icol = jnp.broadcast_to(inc[:, hh:hh + 1], (q, q))
                irow = jnp.broadcast_to(inc_t[hh:hh + 1, :], (q, q))
                lf = jnp.exp(jnp.where(lower, icol - irow, NEG_BIG)) * ad_t[16 + hh:17 + hh, :]
                ecol = jnp.broadcast_to(exc[:, 8 + hh:9 + hh], (q, q))
                erow = jnp.broadcast_to(exc_t[8 + hh:9 + hh, :], (q, q))
                lb = jnp.exp(jnp.where(upper, erow - ecol, NEG_BIG)) * ad_t[24 + hh:25 + hh, :]
                w = (gmat[hh // (C_HEADS // SSM_GROUPS)] * (lf + lb)).astype(BF16)
                ys.append(_dot(w, xs_bf[:, pr * LANES:(pr + 1) * LANES]))
            o_ref[0, pl.ds(r0, q), pr * LANES:(pr + 1) * LANES] = jnp.where(lo64, ys[0], ys[1])
        pf = expand(jnp.where(first8, jnp.exp(inc), 0.0), exf_ref)
        sf = expand(jnp.where(first8, jnp.exp(tot - inc) * dts, 0.0), exf_ref)
        y_off = _dot(cm_bf, hf[...].astype(BF16)) * pf
        o_ref[0, pl.ds(r0, q), :] = o_ref[0, pl.ds(r0, q), :] + y_off
        st = _dot(bt_bf, (xs * sf).astype(BF16))
        hf[...] = pf[q - 1:q, :] * hf[...] + jnp.where(bd, st, 0.0)
        return carry

    lax.fori_loop(0, nc // 2, lambda i, cr: fwd_body(2 * i + 1, fwd_body(2 * i, cr)), 0)

    def bwd_body(i, carry):
        c = nc - 1 - i
        r0 = pl.multiple_of(c * q, q)
        ad, inc, exc, tot, dts = chunk_decay(r0)
        xq = xact[pl.ds(r0, q), :]
        xs = xq[:, :C_INNER]
        bt_bf = xq[:, C_INNER:C_INNER + LANES].T.astype(BF16)
        cm_bf = xq[:, C_INNER + LANES:].astype(BF16)
        pbk = expand(jnp.where(mid8, jnp.exp(tot - exc), 0.0), exb_ref)
        sbk = expand(jnp.where(mid8, jnp.exp(exc) * dts, 0.0), exb_ref)
        y = o_ref[0, pl.ds(r0, q), :] + _dot(cm_bf, hb[...].astype(BF16)) * pbk
        st = _dot(bt_bf, (xs * sbk).astype(BF16))
        hb[...] = pbk[0:1, :] * hb[...] + jnp.where(bd, st, 0.0)
        y = (y + dskip_ref[...] * xs) * _silu(z_ref[0, pl.ds(r0, q), :])
        o_ref[0, pl.ds(r0, q), :] = _rms_rows(y, g_ref[...])
        return carry

    lax.fori_loop(0, nc // 2, lambda i, cr: bwd_body(2 * i + 1, bwd_body(2 * i, cr)), 0)


def _ssd(xbc, dt, z, lw, consts):
    s, l, _ = xbc.shape
    assert l % (2 * SSD_Q) == 0
    seq = lambda w: pl.BlockSpec((1, l, w), lambda si: (si, 0, 0))
    full = lambda shape: pl.BlockSpec(shape, lambda si: (0,) * len(shape))
    return pl.pallas_call(
        _ssd_kernel,
        out_shape=jax.ShapeDtypeStruct((s, l, C_INNER), F32),
        grid=(s,),
        in_specs=[seq(CONV_DIM), seq(LANES), seq(C_INNER),
                  full((D_CONV, CONV_DIM)), full((1, CONV_DIM)), full((1, LANES)), full((1, LANES)),
                  full((1, C_INNER)), full((1, C_INNER)),
                  full((LANES, C_INNER)), full((LANES, C_INNER)), full((LANES, C_INNER))],
        out_specs=seq(C_INNER),
        scratch_shapes=[pltpu.VMEM((l + 16, CONV_DIM), F32), pltpu.VMEM((l, CONV_DIM), F32),
                        pltpu.VMEM((l, LANES), F32),
                        pltpu.VMEM((LANES, C_INNER), F32), pltpu.VMEM((LANES, C_INNER), F32)],
        compiler_params=_params(("parallel",), 56),
        name="ssd",
    )(xbc, dt, z, lw["conv_w"], lw["conv_b"], lw["dt_bias"], lw["a_log"], lw["d_skip"], lw["ssm_g"],
      consts["exf"], consts["exb"], consts["bd"])


def _route(logits):
    lane = lax.broadcasted_iota(jnp.int32, logits.shape, 1)
    lane_f = lane.astype(F32)
    neg_inf = -jnp.inf
    isg = (lane >= N_EXPERTS) & (lane < N_EXPERTS + N_EGROUPS)
    gl = jnp.where(isg, logits, neg_inf)
    gmax = jnp.max(gl, axis=-1, keepdims=True)
    gsel = jnp.min(jnp.where(gl == gmax, lane_f - N_EXPERTS, 1e9), axis=-1, keepdims=True)
    gw = 1.0 / jnp.sum(jnp.where(isg, jnp.exp(gl - gmax), 0.0), axis=-1, keepdims=True)
    grp = jnp.floor(lane_f * (1.0 / EXPERTS_PER_GROUP))
    el = jnp.where((lane < N_EXPERTS) & (grp == gsel), logits, neg_inf)
    l1 = jnp.max(el, axis=-1, keepdims=True)
    i1 = jnp.min(jnp.where(el == l1, lane_f, 1e9), axis=-1, keepdims=True)
    el2 = jnp.where(lane_f == i1, neg_inf, el)
    l2 = jnp.max(el2, axis=-1, keepdims=True)
    i2 = jnp.min(jnp.where(el2 == l2, lane_f, 1e9), axis=-1, keepdims=True)
    r = jnp.exp(l2 - l1)
    g1 = gw / (1.0 + r)
    g2 = g1 * r
    return jnp.where(lane == 0, i1, jnp.where(lane == 1, i2, jnp.where(lane == 2, g1,
                                                                    jnp.where(lane == 3, g2, 0.0))))


def _outproj_kernel(oa_ref, ob_ref, oc_ref, x_ref, mod_ref, wout_ref, goa_ref, gob_ref,
                    l1g_ref, l1b_ref, wrc_ref, br_ref, x1_o, h2v_o, route_o, ct):
    oa = _rms_rows(oa_ref[0], goa_ref[...]).astype(BF16)
    ob = _rms_rows(ob_ref[0], gob_ref[...]).astype(BF16)
    oc = oc_ref[0].astype(BF16)
    wa = A_HEADS * HEAD_DIM
    wb = wa + B_HEADS * MLA_V
    y = _dot(oa, wout_ref[0:wa, :]) + _dot(ob, wout_ref[wa:wb, :]) + _dot(oc, wout_ref[wb:, :])
    m = mod_ref[0]
    x1 = _ln_rows(DN_ALPHA * x_ref[0] + (1.0 + m[2:3]) * y, l1g_ref[...], l1b_ref[...])
    x1_o[0] = x1
    h2 = x1 * (1.0 + m[4:5]) + m[3:4]
    _rows_to_tiles(h2, ct, h2v_o)
    hi, lo = _split_bf16(h2)
    hw = _dot(hi, wrc_ref[...])
    logits = hw[:, :LANES] + hw[:, LANES:] + _dot(lo, wrc_ref[:, :LANES]) + br_ref[...]
    route_o[0] = _route(logits)


def _outproj(oa, ob, oc, x, mod, lw, tm):
    s, l, d = x.shape
    nt = l // tm
    nch = d // LANES
    row = lambda w: pl.BlockSpec((1, tm, w), lambda si, i: (si, i, 0))
    full = lambda shape: pl.BlockSpec(shape, lambda si, i: (0,) * len(shape))
    return pl.pallas_call(
        _outproj_kernel,
        out_shape=(jax.ShapeDtypeStruct((s, l, d), F32), jax.ShapeDtypeStruct((s * l * nch, LANES), F32),
                   jax.ShapeDtypeStruct((s, l, LANES), F32)),
        grid=(s, l // tm),
        in_specs=[row(A_HEADS * HEAD_DIM), row(B_HEADS * MLA_V), row(C_INNER), row(d),
                  pl.BlockSpec((1, 6, d), lambda si, i: (si, 0, 0)),
                  full((d, d)), full((1, A_HEADS * HEAD_DIM)), full((1, B_HEADS * MLA_V)),
                  full((1, d)), full((1, d)), full((d, 2 * LANES)), full((1, LANES))],
        out_specs=(row(d), pl.BlockSpec((tm * nch, LANES), lambda si, i: (si * nt + i, 0)), row(LANES)),
        scratch_shapes=[pltpu.VMEM((nch * _pitch(tm), LANES), F32)],
        compiler_params=_params(("parallel", "parallel"), 48),
        name="outproj",
    )(oa, ob, oc, x, mod, lw["w_out"], lw["goa"], lw["gob"], lw["ln1_g"], lw["ln1_b"],
      lw["w_r"], lw["b_r"])


MOE_ROWS = 128
ROW_STRIDE = _pitch(MOE_ROWS)


def _slot_rows(l):
    return 2 * l + N_EXPERTS * 8 + MOE_ROWS


def _route_sort_kernel(route_ref, slots_o, meta_o):
    r = route_ref[0]
    l = r.shape[0]
    lane = lax.broadcasted_iota(jnp.int32, (l, LANES), 1)
    lane_f = lane.astype(F32)
    oh1 = lane_f == jnp.broadcast_to(r[:, 0:1], (l, LANES))
    oh2 = lane_f == jnp.broadcast_to(r[:, 1:2], (l, LANES))
    member = jnp.where(oh1 | oh2, 1.0, 0.0).astype(BF16)
    tb = min(256, l)
    ri = lax.broadcasted_iota(jnp.int32, (tb, tb), 0)
    ci = lax.broadcasted_iota(jnp.int32, (tb, tb), 1)
    stril = jnp.where(ri > ci, 1.0, 0.0).astype(BF16)
    run = jnp.zeros((1, LANES), F32)
    ranks = []
    for i in range(l // tb):
        blk = member[i * tb:(i + 1) * tb]
        ranks.append(_dot(stril, blk) + run)
        run = run + jnp.sum(blk.astype(F32), axis=0, keepdims=True)
    rank_all = jnp.concatenate(ranks, axis=0)
    padded = jnp.floor((run + 7.0) * 0.125) * 8.0
    hi, lo = _split_bf16(jnp.broadcast_to(padded, (8, LANES)))
    rl = lax.broadcasted_iota(jnp.int32, (LANES, LANES), 0)
    cl = lax.broadcasted_iota(jnp.int32, (LANES, LANES), 1)
    upper = jnp.where(rl < cl, 1.0, 0.0).astype(BF16)
    base = (_dot(hi, upper) + _dot(lo, upper))[0:1]
    pos = base + rank_all
    slot1 = jnp.sum(jnp.where(oh1, pos, 0.0), axis=-1, keepdims=True)
    slot2 = jnp.sum(jnp.where(oh2, pos, 0.0), axis=-1, keepdims=True)
    packed = jnp.where(lane == 0, slot1, jnp.where(lane == 1, slot2, 0.0))
    slots_o[0] = packed.T[0:2].astype(jnp.int32)
    meta_o[0] = jnp.concatenate([base, run, jnp.zeros((6, LANES), F32)], axis=0).astype(jnp.int32)


def _route_sort(route):
    s, l, _ = route.shape
    return pl.pallas_call(
        _route_sort_kernel,
        out_shape=(jax.ShapeDtypeStruct((s, 2, l), jnp.int32), jax.ShapeDtypeStruct((s, 8, LANES), jnp.int32)),
        grid=(s,),
        in_specs=[pl.BlockSpec((1, l, LANES), lambda si: (si, 0, 0))],
        out_specs=(pl.BlockSpec((1, 2, l), lambda si: (si, 0, 0)), pl.BlockSpec((1, 8, LANES), lambda si: (si, 0, 0))),
        compiler_params=_params(("parallel",), 32),
        name="route_sort",
    )(route)


MOE_EXPERTS_PER_STEP = 2
MOE_EXPERT_STEPS = N_EXPERTS // MOE_EXPERTS_PER_STEP


def _moe_kernel(meta_ref, h2v_ref, slots_hbm, wg_ref, wu_ref, wd_ref,
                x1_ref, route_ref, mod_ref, l2g_ref, l2b_ref, o_ref,
                slot_s, tok_s, y_v, xt, yt, ct1, ct2, sem):
    s = pl.program_id(0)
    e = pl.program_id(1)
    l = h2v_ref.shape[0] // 8
    nch = D_MODEL // LANES
    n_slots = tok_s.shape[0]
    spare = 8 * 2 * l

    @pl.when(e == 0)
    def _():
        c_slots = pltpu.make_async_copy(slots_hbm.at[s], slot_s, sem.at[0])
        c_slots.start()
        used = meta_ref[s, 0, N_EXPERTS]

        def clear_tail(g, c):
            for j in range(8):
                tok_s[used + g * 8 + j] = spare
            return c

        lax.fori_loop(0, (n_slots - used) // 8, clear_tail, 0)

        def clear(ex, c):
            n = meta_ref[s, 1, ex]
            last = meta_ref[s, 0, ex] + ((n + 7) // 8) * 8 - 8

            @pl.when(n > 0)
            def _():
                for j in range(8):
                    tok_s[last + j] = spare

            return c

        lax.fori_loop(0, N_EXPERTS, clear, 0)
        c_slots.wait()

        def invert(g, c):
            for j in range(8):
                t = g * 8 + j
                for k in range(2):
                    tok_s[slot_s[k, t]] = 8 * (2 * t + k)
            return c

        lax.fori_loop(0, l // 8, invert, 0)

    grp_per_blk = MOE_ROWS // 8

    def ffn(sub):
        xb = jnp.concatenate([xt[sub, j * ROW_STRIDE:j * ROW_STRIDE + MOE_ROWS, :] for j in range(nch)],
                             axis=-1).astype(BF16)
        mid = (_silu(_dot(xb, wg_ref[sub])) * _dot(xb, wu_ref[sub])).astype(BF16)
        y = _dot(mid, wd_ref[sub])
        for j in range(nch):
            yt[sub, j * ROW_STRIDE:j * ROW_STRIDE + MOE_ROWS, :] = y[:, j * LANES:(j + 1) * LANES]

    def gather_row(sub, slot, i):
        src = pl.multiple_of(lax.shift_right_logical(tok_s[slot], 1) & (8 * l - 8), 8)
        xt[sub, pl.ds(i, nch, stride=ROW_STRIDE), :] = h2v_ref[pl.ds(src, 8), :]

    def scatter_row(sub, slot, i):
        y_v[pl.ds(pl.multiple_of(tok_s[slot], 8), 8), :] = yt[sub, pl.ds(i, nch, stride=ROW_STRIDE), :]

    @pl.when(e < MOE_EXPERT_STEPS)
    def _():
        info = []
        for sub in range(MOE_EXPERTS_PER_STEP):
            ex = e * MOE_EXPERTS_PER_STEP + sub
            info.append((meta_ref[s, 0, ex], meta_ref[s, 1, ex]))
        for sub, (base, n) in enumerate(info):
            for i in range(MOE_ROWS):
                gather_row(sub, base + i, i)
        for sub in range(MOE_EXPERTS_PER_STEP):
            ffn(sub)
        for sub, (base, n) in enumerate(info):
            for i in range(MOE_ROWS):
                scatter_row(sub, base + i, i)
        for sub, (base, n) in enumerate(info):
            ngrp = (n + 7) // 8

            def block(blk, c, sub=sub, base=base, ngrp=ngrp):
                g_lo = blk * grp_per_blk
                ng = jnp.minimum(grp_per_blk, ngrp - g_lo)

                def gather(g, cc):
                    for j in range(8):
                        gather_row(sub, base + (g_lo + g) * 8 + j, g * 8 + j)
                    return cc

                lax.fori_loop(0, ng, gather, 0)
                ffn(sub)

                def scatter(g, cc):
                    for j in range(8):
                        scatter_row(sub, base + (g_lo + g) * 8 + j, g * 8 + j)
                    return cc

                lax.fori_loop(0, ng, scatter, 0)
                return c

            lax.fori_loop(1, (ngrp + grp_per_blk - 1) // grp_per_blk, block, 0)

    @pl.when(e >= MOE_EXPERT_STEPS)
    def _():
        tmo = x1_ref.shape[1]
        t0 = (e - MOE_EXPERT_STEPS) * tmo
        r = route_ref[0]
        y = (r[:, 2:3] * _tiles_to_rows(y_v, t0, tmo, ct1, 2, 0)
             + r[:, 3:4] * _tiles_to_rows(y_v, t0, tmo, ct2, 2, 1))
        m = mod_ref[0]
        o_ref[0] = _ln_rows(DN_ALPHA * x1_ref[0] + (1.0 + m[5:6]) * y, l2g_ref[...], l2b_ref[...])


def _moe(h2v, slots, meta, x1, route, mod, lw, tm):
    s, l, d = x1.shape
    assert l & (l - 1) == 0, "the gather address mask needs a power-of-two sequence length"
    n_slots = _slot_rows(l)
    eps = MOE_EXPERTS_PER_STEP
    wspec = lambda shape: pl.BlockSpec(shape, lambda si, e: (jnp.minimum(e, MOE_EXPERT_STEPS - 1), 0, 0))
    blk_idx = lambda si, e: (si, jnp.maximum(e - MOE_EXPERT_STEPS, 0), 0)
    tok_blk = pl.BlockSpec((1, tm, d), blk_idx)
    vec = pl.BlockSpec((1, d), lambda si, e: (0, 0))
    return pl.pallas_call(
        _moe_kernel,
        out_shape=jax.ShapeDtypeStruct((s, l, d), F32),
        grid=(s, MOE_EXPERT_STEPS + l // tm),
        in_specs=[pl.BlockSpec(memory_space=pltpu.SMEM),
                  pl.BlockSpec((l * 8, LANES), lambda si, e: (si, 0)),
                  pl.BlockSpec(memory_space=pl.ANY),
                  wspec((eps, d, D_FF_EXPERT)), wspec((eps, d, D_FF_EXPERT)), wspec((eps, D_FF_EXPERT, d)),
                  tok_blk, pl.BlockSpec((1, tm, LANES), blk_idx),
                  pl.BlockSpec((1, 6, d), lambda si, e: (si, 0, 0)), vec, vec],
        out_specs=tok_blk,
        scratch_shapes=[pltpu.SMEM((2, l), jnp.int32), pltpu.SMEM((n_slots,), jnp.int32),
                        pltpu.VMEM(((2 * l + 1) * 8, LANES), F32),
                        pltpu.VMEM((eps, 8 * ROW_STRIDE, LANES), F32), pltpu.VMEM((eps, 8 * ROW_STRIDE, LANES), F32),
                        pltpu.VMEM((8 * _pitch(tm), LANES), F32), pltpu.VMEM((8 * _pitch(tm), LANES), F32),
                        pltpu.SemaphoreType.DMA((1,))],
        compiler_params=_params(("parallel", "arbitrary"), 60),
        name="moe",
    )(meta, h2v, slots, lw["w_e_gate"], lw["w_e_up"], lw["w_e_down"], x1, route, mod,
      lw["ln2_g"], lw["ln2_b"])


def _rope_tables(seq_len):
    t = jnp.arange(seq_len, dtype=jnp.int32)
    row = (t // GRID_W).astype(F32)[:, None]
    col = (t % GRID_W).astype(F32)[:, None]

    def axis_tables(d_axis):
        inv = ROPE_THETA ** (-jnp.arange(0, d_axis, 2, dtype=F32) / d_axis)
        zero = jnp.zeros((seq_len, d_axis // 2), F32)
        cs, ps, ms = [], [], []
        for pos in (row, col):
            ang = pos * inv
            c, sn = jnp.cos(ang), jnp.sin(ang)
            cs += [c, c]
            ps += [zero, sn]
            ms += [-sn, zero]
        return [jnp.concatenate(v, axis=-1) for v in (cs, ps, ms)]

    ta = [jnp.tile(v, (1, 2)) for v in axis_tables(HEAD_DIM // 2)]
    one = jnp.ones((seq_len, MLA_NOPE), F32)
    zero = jnp.zeros((seq_len, MLA_NOPE), F32)
    pad1 = jnp.ones((seq_len, LANES - MLA_NOPE - MLA_ROPE), F32)
    pad0 = jnp.zeros((seq_len, LANES - MLA_NOPE - MLA_ROPE), F32)
    cb, pb, mb = axis_tables(MLA_ROPE // 2)
    tb = [jnp.concatenate([one, cb, pad1], -1), jnp.concatenate([zero, pb, pad0], -1),
          jnp.concatenate([zero, mb, pad0], -1)]
    return ta + tb


def _ssd_consts():
    r = np.arange(LANES)[:, None]
    c = np.arange(C_INNER)[None, :]
    exf = ((r < C_HEADS) & (c // C_HEAD_DIM == r)).astype(np.float32)
    exb = ((r >= C_HEADS) & (r < 2 * C_HEADS) & (c // C_HEAD_DIM == r - C_HEADS)).astype(np.float32)
    heads_per_group = C_HEADS // SSM_GROUPS
    bd = (r // D_STATE == c // (C_HEAD_DIM * heads_per_group)).astype(np.float32)
    return {"exf": jnp.asarray(exf, BF16), "exb": jnp.asarray(exb, BF16), "bd": jnp.asarray(bd, F32)}


def _pack_weights(p):
    w_in = p["w_in"]
    depth = w_in.shape[0]
    off = np.cumsum((0, 256, 128, 128, Q_LORA, KV_LORA, MLA_ROPE, C_INNER, CONV_DIM, 2 * C_HEADS))
    w_kr = w_in[:, :, off[5]:off[6]]
    zpad = lambda n: jnp.zeros((depth, D_MODEL, n), F32)
    dtkr = jnp.concatenate([w_in[:, :, off[8]:], zpad(KR_LANE - 2 * C_HEADS), w_kr,
                            zpad(LANES - KR_LANE - MLA_ROPE)], -1)
    w_in_p = jnp.concatenate([w_in[:, :, :off[5]], w_in[:, :, off[6]:off[8]], dtkr], -1).astype(BF16)
    dq = MLA_NOPE + MLA_ROPE
    w_uq = p["w_uq"].reshape(depth, Q_LORA, B_HEADS, dq)
    w_uq_p = jnp.concatenate([w_uq, jnp.zeros((depth, Q_LORA, B_HEADS, LANES - dq), F32)], -1)
    w_ukv = p["w_ukv"].reshape(depth, KV_LORA, B_HEADS, MLA_NOPE + MLA_V)
    w_ukk = jnp.concatenate([w_ukv[..., :MLA_NOPE],
                             jnp.zeros((depth, KV_LORA, B_HEADS, LANES - MLA_NOPE), F32)], -1)
    pad_lanes = lambda v: jnp.concatenate([v, jnp.zeros(v.shape[:-1] + (LANES - v.shape[-1],), F32)], -1)
    w_r = pad_lanes(jnp.concatenate([p["w_re"], p["w_rg"]], -1))
    w_r_hi = w_r.astype(BF16)
    return {
        "w_ada": p["w_ada"].astype(BF16), "b_ada": p["b_ada"],
        "w_in": w_in_p,
        "w_uq": w_uq_p.reshape(depth, Q_LORA, B_HEADS * LANES).astype(BF16),
        "w_ukk": w_ukk.reshape(depth, KV_LORA, B_HEADS * LANES).astype(BF16),
        "w_ukv": w_ukv[..., MLA_NOPE:].reshape(depth, KV_LORA, B_HEADS * MLA_V).astype(BF16),
        "gqa": jnp.tile(p["qa_norm_g"], (1, 2))[:, None, :], "gka": jnp.tile(p["ka_norm_g"], (1, 2))[:, None, :],
        "gqb": p["qb_norm_g"][:, None, :], "gkvb": p["kvb_norm_g"][:, None, :],
        "goa": p["oa_norm_g"][:, None, :], "gob": p["ob_norm_g"][:, None, :],
        "conv_w": p["conv_w"], "conv_b": p["conv_b"][:, None, :],
        "dt_bias": pad_lanes(p["dt_bias"].reshape(depth, 2 * C_HEADS))[:, None, :],
        "a_log": pad_lanes(p["a_log"].reshape(depth, 2 * C_HEADS))[:, None, :],
        "d_skip": jnp.repeat(p["d_skip"], C_HEAD_DIM, axis=-1)[:, None, :],
        "ssm_g": p["ssm_norm_g"][:, None, :],
        "w_out": p["w_out"].astype(BF16),
        "ln1_g": p["ln1_g"][:, None, :], "ln1_b": p["ln1_b"][:, None, :],
        "w_r": jnp.concatenate([w_r_hi, (w_r - w_r_hi.astype(F32)).astype(BF16)], -1),
        "b_r": pad_lanes(jnp.concatenate([p["b_re"], p["b_rg"]], -1))[:, None, :],
        "w_e_gate": p["w_e_gate"].astype(BF16), "w_e_up": p["w_e_up"].astype(BF16),
        "w_e_down": p["w_e_down"].astype(BF16),
        "ln2_g": p["ln2_g"][:, None, :], "ln2_b": p["ln2_b"][:, None, :],
    }


def _trunk(x, c, p, pw=None, tm=512, tq=512):
    s, l, d = x.shape
    tm = min(tm, l)
    tq = min(tq, l)
    pw = _pack_weights(p) if pw is None else pw
    tabs = _rope_tables(l)
    consts = _ssd_consts()
    mods = _ada(c, pw["w_ada"], pw["b_ada"]).reshape(DEPTH, s, 6, d)
    x = _ln0(x.reshape(s * l, d), p["ln0_g"][None, :], p["ln0_b"][None, :], tm).reshape(s, l, d)
    for layer in range(DEPTH):
        lw = {k: v[layer] for k, v in pw.items() if k not in ("w_ada", "b_ada")}
        mod = mods[layer]
        qa, ka, va, qb, kb, vb, z, xbc, dt = _inproj(x, mod, lw, tabs, tm)
        oa = _attention(qa, ka, va, tq, "attn_a")
        ob = _attention(qb, kb, vb, tq, "attn_b")
        oc = _ssd(xbc, dt, z, lw, consts)
        x1, h2v, route = _outproj(oa, ob, oc, x, mod, lw, tm)
        slots, meta = _route_sort(route)
        x = _moe(h2v, slots, meta, x1, route, mod, lw, tm)
    return x


def kernel(x_prompt, x_sample, c_prompt, c_sample, ln0_g, ln0_b, w_ada, b_ada, w_in, qa_norm_g, ka_norm_g, oa_norm_g, qb_norm_g, w_uq, kvb_norm_g, w_ukv, ob_norm_g, conv_w, conv_b, dt_bias, a_log, d_skip, ssm_norm_g, w_out, ln1_g, ln1_b, w_rg, b_rg, w_re, b_re, w_e_gate, w_e_up, w_e_down, ln2_g, ln2_b):
    p = dict(ln0_g=ln0_g, ln0_b=ln0_b, w_ada=w_ada, b_ada=b_ada, w_in=w_in,
             qa_norm_g=qa_norm_g, ka_norm_g=ka_norm_g, oa_norm_g=oa_norm_g,
             qb_norm_g=qb_norm_g, w_uq=w_uq, kvb_norm_g=kvb_norm_g, w_ukv=w_ukv, ob_norm_g=ob_norm_g,
             conv_w=conv_w, conv_b=conv_b, dt_bias=dt_bias, a_log=a_log, d_skip=d_skip,
             ssm_norm_g=ssm_norm_g, w_out=w_out, ln1_g=ln1_g, ln1_b=ln1_b,
             w_rg=w_rg, b_rg=b_rg, w_re=w_re, b_re=b_re,
             w_e_gate=w_e_gate, w_e_up=w_e_up, w_e_down=w_e_down, ln2_g=ln2_g, ln2_b=ln2_b)
    pw = _pack_weights(p)
    return (_trunk(x_prompt, c_prompt, p, pw), _trunk(x_sample, c_sample, p, pw))
```

```python
import functools

import jax
import jax.numpy as jnp
import numpy as np
from jax import lax
from jax.experimental import pallas as pl
from jax.experimental.pallas import tpu as pltpu

F32 = jnp.float32
BF16 = jnp.bfloat16

D_MODEL = 1024
DEPTH = 4
GRID_W = 64
ROPE_THETA = 10000.0
HEAD_DIM = 64
A_HEADS = 4
A_KV_HEADS = 2
B_HEADS = 4
Q_LORA = 384
KV_LORA = 128
MLA_NOPE = 64
MLA_ROPE = 32
MLA_V = 64
C_HEADS = 8
C_HEAD_DIM = 64
SSM_GROUPS = 2
D_STATE = 64
D_CONV = 5
C_INNER = C_HEADS * C_HEAD_DIM
CONV_DIM = C_INNER + 2 * SSM_GROUPS * D_STATE
N_EGROUPS = 4
EXPERTS_PER_GROUP = 8
N_EXPERTS = N_EGROUPS * EXPERTS_PER_GROUP
D_FF_EXPERT = 256
DN_ALPHA = (2 * DEPTH) ** 0.25
EPS = 1e-6

LANES = 128
SSD_Q = 128
NEG_BIG = -1e30

P_QA, P_KA, P_VA, P_CQ, P_CKV, P_Z, P_XBC, P_DTKR, P_END = (
    0, 256, 384, 512, 896, 1024, 1536, 2304, 2432)
KR_LANE = MLA_NOPE


def _params(sem, vmem_mb):
    return pltpu.CompilerParams(dimension_semantics=sem, vmem_limit_bytes=vmem_mb * 1024 * 1024)


def _silu(x):
    return x * (1.0 / (1.0 + jnp.exp(-x)))


def _ln_rows(x, g, b):
    mu = jnp.mean(x, axis=-1, keepdims=True)
    xc = x - mu
    var = jnp.mean(xc * xc, axis=-1, keepdims=True)
    return xc * lax.rsqrt(var + EPS) * g + b


def _rms_rows(x, g):
    return x * lax.rsqrt(jnp.mean(x * x, axis=-1, keepdims=True) + EPS) * g


def _split_bf16(x):
    hi = x.astype(BF16)
    lo = (x - hi.astype(F32)).astype(BF16)
    return hi, lo


def _dot(a, b):
    return jnp.dot(a, b, preferred_element_type=F32)


def _dot_nt(a, b):
    return lax.dot_general(a, b, (((1,), (1,)), ((), ())), preferred_element_type=F32)


def _pitch(rows):
    return rows + 8


def _rows_to_tiles(x, ct, out_ref):
    m = x.shape[0]
    nch = x.shape[1] // LANES
    p = _pitch(m)
    for j in range(nch):
        ct[j * p:j * p + m, :] = x[:, j * LANES:(j + 1) * LANES]

    def grp(g, c):
        for j in range(8):
            r = g * 8 + j
            out_ref[pl.ds(pl.multiple_of(r * 8, 8), 8), :] = ct[pl.ds(r, nch, stride=p), :]
        return c

    lax.fori_loop(0, m // 8, grp, 0)


def _tiles_to_rows(src_ref, row0, m, ct, step=1, offset=0):
    nch = D_MODEL // LANES
    p = _pitch(m)

    def grp(g, c):
        for j in range(8):
            r = g * 8 + j
            src = pl.multiple_of(((row0 + r) * step + offset) * 8, 8)
            ct[pl.ds(r, nch, stride=p), :] = src_ref[pl.ds(src, 8), :]
        return c

    lax.fori_loop(0, m // 8, grp, 0)
    return jnp.concatenate([ct[j * p:j * p + m, :] for j in range(nch)], axis=-1)


def _ln0_kernel(x_ref, g_ref, b_ref, o_ref):
    o_ref[...] = _ln_rows(x_ref[...], g_ref[...], b_ref[...])


def _ln0(x2d, g, b, tm):
    t, d = x2d.shape
    return pl.pallas_call(
        _ln0_kernel,
        out_shape=jax.ShapeDtypeStruct((t, d), F32),
        grid=(t // tm,),
        in_specs=[pl.BlockSpec((tm, d), lambda i: (i, 0)),
                  pl.BlockSpec((1, d), lambda i: (0, 0)),
                  pl.BlockSpec((1, d), lambda i: (0, 0))],
        out_specs=pl.BlockSpec((tm, d), lambda i: (i, 0)),
        compiler_params=_params(("parallel",), 32),
        name="ln0",
    )(x2d, g, b)


def _ada_kernel(c_ref, w_ref, b_ref, o_ref):
    cs = _silu(c_ref[...]).astype(BF16)
    o_ref[0] = _dot(cs, w_ref[0]) + b_ref[0]


def _ada(c, w_bf, b, tn=2048):
    s, d = c.shape
    depth, _, n = w_bf.shape
    return pl.pallas_call(
        _ada_kernel,
        out_shape=jax.ShapeDtypeStruct((depth, s, n), F32),
        grid=(depth, n // tn),
        in_specs=[pl.BlockSpec((s, d), lambda l, j: (0, 0)),
                  pl.BlockSpec((1, d, tn), lambda l, j: (l, 0, j)),
                  pl.BlockSpec((1, 1, tn), lambda l, j: (l, 0, j))],
        out_specs=pl.BlockSpec((1, s, tn), lambda l, j: (l, 0, j)),
        compiler_params=_params(("parallel", "parallel"), 32),
        name="ada",
    )(c, w_bf, b.reshape(depth, 1, n))


def _rope(x, c, p, m, shift):
    return x * c + pltpu.roll(x, shift, 1) * p + pltpu.roll(x, LANES - shift, 1) * m


def _headnorm2(x, g):
    xx = x * x
    lo = lax.broadcasted_iota(jnp.int32, x.shape, 1) < HEAD_DIM
    s0 = jnp.sum(jnp.where(lo, xx, 0.0), axis=-1, keepdims=True)
    s1 = jnp.sum(jnp.where(lo, 0.0, xx), axis=-1, keepdims=True)
    inv = jnp.where(lo, lax.rsqrt(s0 * (1.0 / HEAD_DIM) + EPS), lax.rsqrt(s1 * (1.0 / HEAD_DIM) + EPS))
    return x * inv * g


def _store_values(v2, v_o, h0):
    lo = lax.broadcasted_iota(jnp.int32, v2.shape, 1) < ATT_DV
    v_o[0, h0] = jnp.where(lo, v2, 1.0).astype(BF16)
    v_o[0, h0 + 1] = jnp.where(lo, pltpu.roll(v2, ATT_DV, 1), 1.0).astype(BF16)


def _inproj_kernel(x_ref, mod_ref, w_ref, wuq_ref, wukk_ref, wukv_ref,
                   gqa_ref, gka_ref, gqb_ref, gkvb_ref,
                   ca_ref, pa_ref, ma_ref, cb_ref, pb_ref, mb_ref,
                   qa_o, ka_o, va_o, qb_o, kb_o, vb_o, z_o, xbc_o, dt_o):
    m = mod_ref[0]
    h = (x_ref[0] * (1.0 + m[1:2]) + m[0:1]).astype(BF16)

    def proj(a, b):
        return _dot(h, w_ref[:, a:b])

    ca, pa, ma = ca_ref[...], pa_ref[...], ma_ref[...]
    cb, pb, mb = cb_ref[...], pb_ref[...], mb_ref[...]
    half = HEAD_DIM // 2
    pq = proj(P_QA, P_KA)
    for ch in range(A_HEADS // 2):
        xa = _headnorm2(pq[:, ch * LANES:(ch + 1) * LANES], gqa_ref[...])
        xa = _rope(xa, ca, pa, ma, half // 2) * (HEAD_DIM ** -0.5)
        qa_o[0, 2 * ch] = xa[:, :HEAD_DIM].astype(BF16)
        qa_o[0, 2 * ch + 1] = xa[:, HEAD_DIM:].astype(BF16)
    pkv = proj(P_KA, P_CQ)
    xk = _rope(_headnorm2(pkv[:, :LANES], gka_ref[...]), ca, pa, ma, half // 2)
    ka_o[0, 0] = xk[:, :HEAD_DIM].astype(BF16)
    ka_o[0, 1] = xk[:, HEAD_DIM:].astype(BF16)
    _store_values(pkv[:, LANES:], va_o, 0)
    pc = proj(P_CQ, P_Z)
    cqn = _rms_rows(pc[:, :Q_LORA], gqb_ref[...]).astype(BF16)
    qb = _dot(cqn, wuq_ref[...])
    ckvn = _rms_rows(pc[:, Q_LORA:], gkvb_ref[...]).astype(BF16)
    kn = _dot(ckvn, wukk_ref[...])
    vv = _dot(ckvn, wukv_ref[...])
    for pr in range(B_HEADS // 2):
        _store_values(vv[:, pr * LANES:(pr + 1) * LANES], vb_o, 2 * pr)
    px = proj(P_XBC, P_END)
    dtkr = px[:, CONV_DIM:]
    lane = lax.broadcasted_iota(jnp.int32, dtkr.shape, 1)
    is_kr = (lane >= KR_LANE) & (lane < KR_LANE + MLA_ROPE)
    k_rope = _rope(jnp.where(is_kr, dtkr, 0.0), cb, pb, mb, MLA_ROPE // 4)
    scale_b = (MLA_NOPE + MLA_ROPE) ** -0.5
    for hh in range(B_HEADS):
        sl = slice(hh * LANES, (hh + 1) * LANES)
        qb_o[0, hh] = (_rope(qb[:, sl], cb, pb, mb, MLA_ROPE // 4) * scale_b).astype(BF16)
        kb_o[0, hh] = (kn[:, sl] + k_rope).astype(BF16)
    z_o[0] = proj(P_Z, P_XBC)
    xbc_o[0] = px[:, :CONV_DIM]
    dt_o[0] = jnp.where(lane < 2 * C_HEADS, dtkr, 0.0)


def _inproj(x, mod, lw, tabs, tm):
    s, l, d = x.shape
    nt = l // tm
    full = lambda shape: pl.BlockSpec(shape, lambda si, i: (0,) * len(shape))
    tab = pl.BlockSpec((tm, LANES), lambda si, i: (i, 0))
    head = lambda nh, dh: pl.BlockSpec((1, nh, tm, dh), lambda si, i: (si, 0, i, 0))
    row = lambda w: pl.BlockSpec((1, tm, w), lambda si, i: (si, i, 0))
    out_shape = (
        jax.ShapeDtypeStruct((s, A_HEADS, l, HEAD_DIM), BF16),
        jax.ShapeDtypeStruct((s, A_KV_HEADS, l, HEAD_DIM), BF16),
        jax.ShapeDtypeStruct((s, A_KV_HEADS, l, LANES), BF16),
        jax.ShapeDtypeStruct((s, B_HEADS, l, LANES), BF16),
        jax.ShapeDtypeStruct((s, B_HEADS, l, LANES), BF16),
        jax.ShapeDtypeStruct((s, B_HEADS, l, LANES), BF16),
        jax.ShapeDtypeStruct((s, l, C_INNER), F32),
        jax.ShapeDtypeStruct((s, l, CONV_DIM), F32),
        jax.ShapeDtypeStruct((s, l, LANES), F32),
    )
    out_specs = (head(A_HEADS, HEAD_DIM), head(A_KV_HEADS, HEAD_DIM), head(A_KV_HEADS, LANES),
                 head(B_HEADS, LANES), head(B_HEADS, LANES), head(B_HEADS, LANES),
                 row(C_INNER), row(CONV_DIM), row(LANES))
    return pl.pallas_call(
        _inproj_kernel,
        out_shape=out_shape,
        grid=(s, nt),
        in_specs=[row(d),
                  pl.BlockSpec((1, 6, d), lambda si, i: (si, 0, 0)),
                  full((d, P_END)), full((Q_LORA, B_HEADS * LANES)),
                  full((KV_LORA, B_HEADS * LANES)), full((KV_LORA, B_HEADS * MLA_V)),
                  full((1, LANES)), full((1, LANES)), full((1, Q_LORA)), full((1, KV_LORA)),
                  tab, tab, tab, tab, tab, tab],
        out_specs=out_specs,
        compiler_params=_params(("parallel", "parallel"), 48),
        name="inproj",
    )(x, mod, lw["w_in"], lw["w_uq"], lw["w_ukk"], lw["w_ukv"],
      lw["gqa"], lw["gka"], lw["gqb"], lw["gkvb"], *tabs)


ATT_DV = 64
ATT_Q_ROWS = 256


def _attn_kernel(q_ref, k_ref, v_ref, o_ref, *, shared_kv):
    tq = q_ref.shape[2]
    qr = min(ATT_Q_ROWS, tq)
    items = [(j, r) for j in range(2) for r in range(tq // qr)]

    def scores(j, r):
        kj = 0 if shared_kv else j
        return _dot_nt(q_ref[0, j, r * qr:(r + 1) * qr, :], k_ref[0, kj])

    s_next = scores(*items[0])
    outs = {}
    for i, (j, r) in enumerate(items):
        s = s_next
        if i + 1 < len(items):
            s_next = scores(*items[i + 1])
        kj = 0 if shared_kv else j
        p = jnp.exp(s - jnp.max(s, axis=-1, keepdims=True)).astype(BF16)
        o = _dot(p, v_ref[0, kj])
        outs[(j, r)] = o[:, :ATT_DV] / o[:, ATT_DV:ATT_DV + 1]
    for r in range(tq // qr):
        o_ref[0, r * qr:(r + 1) * qr, :] = jnp.concatenate([outs[(0, r)], outs[(1, r)]], axis=-1)


def _attention(q, k, v, tq, name):
    s, hq, l, dk = q.shape
    hk = k.shape[1]
    shared = hq // hk == 2
    kvb = 1 if shared else 2
    return pl.pallas_call(
        functools.partial(_attn_kernel, shared_kv=shared),
        out_shape=jax.ShapeDtypeStruct((s, l, hq * ATT_DV), F32),
        grid=(s, hq // 2, l // tq),
        in_specs=[pl.BlockSpec((1, 2, tq, dk), lambda si, hp, qi: (si, hp, qi, 0)),
                  pl.BlockSpec((1, kvb, l, dk), lambda si, hp, qi: (si, hp, 0, 0)),
                  pl.BlockSpec((1, kvb, l, LANES), lambda si, hp, qi: (si, hp, 0, 0))],
        out_specs=pl.BlockSpec((1, tq, 2 * ATT_DV), lambda si, hp, qi: (si, qi, hp)),
        compiler_params=_params(("parallel", "parallel", "arbitrary"), 48),
        name=name,
    )(q, k, v)


def _ssd_kernel(xbc_ref, dt_ref, z_ref, cw_ref, cb_ref, dtb_ref, alog_ref, dskip_ref, g_ref,
                exf_ref, exb_ref, bd_ref, o_ref, xpad, xact, ad_s, hf, hb):
    l = xbc_ref.shape[1]
    nc = l // SSD_Q
    q = SSD_Q
    pad = 8
    xpad[0:pad, :] = jnp.zeros((pad, CONV_DIM), F32)
    xpad[l + pad:l + 2 * pad, :] = jnp.zeros((pad, CONV_DIM), F32)
    xpad[pad:l + pad, :] = xbc_ref[0]
    def conv_chunk(c, carry):
        r0 = pl.multiple_of(c * q, q)
        for st in range(CONV_DIM // LANES):
            cols = slice(st * LANES, (st + 1) * LANES)
            win = xpad[pl.ds(r0, q + 2 * pad), cols]
            acc = jnp.zeros((q, LANES), F32) + cb_ref[:, cols]
            for k in range(D_CONV):
                lo = pad - D_CONV // 2 + k
                acc = acc + win[lo:lo + q] * cw_ref[k:k + 1, cols]
            xact[pl.ds(r0, q), cols] = _silu(acc)
        return carry

    lax.fori_loop(0, nc, conv_chunk, 0)
    dtr = dt_ref[0] + dtb_ref[...]
    dt = jnp.maximum(dtr, 0.0) + jnp.log1p(jnp.exp(-jnp.abs(dtr)))
    lane_l = lax.broadcasted_iota(jnp.int32, dt.shape, 1)
    a = dt * (-jnp.exp(alog_ref[...]))
    ad_s[...] = jnp.where(lane_l < 2 * C_HEADS, a, pltpu.roll(dt, 2 * C_HEADS, 1))

    ri = lax.broadcasted_iota(jnp.int32, (q, q), 0)
    ci = lax.broadcasted_iota(jnp.int32, (q, q), 1)
    lower = ri >= ci
    upper = ci >= ri
    tril = jnp.where(lower, 1.0, 0.0).astype(BF16)
    lane_q = lax.broadcasted_iota(jnp.int32, (q, LANES), 1)
    first8 = lane_q < C_HEADS
    mid8 = (lane_q >= C_HEADS) & (lane_q < 2 * C_HEADS)
    lo64 = lane_q < C_HEAD_DIM
    bd = bd_ref[...] > 0.5

    def chunk_decay(r0):
        ad = ad_s[pl.ds(r0, q), :]
        hi, lo = _split_bf16(ad)
        inc = _dot(tril, hi) + _dot(tril, lo)
        exc = inc - ad
        tot = inc[q - 1:q, :]
        dts = pltpu.roll(ad, LANES - 2 * C_HEADS, 1)
        return ad, inc, exc, tot, dts

    def expand(v, ex_ref):
        hi, lo = _split_bf16(v)
        return _dot(hi, ex_ref[...]) + _dot(lo, ex_ref[...])

    hf[...] = jnp.zeros_like(hf)
    hb[...] = jnp.zeros_like(hb)

    def fwd_pre(c):
        r0 = pl.multiple_of(c * q, q)
        ad, inc, exc, tot, dts = chunk_decay(r0)
        xq = xact[pl.ds(r0, q), :]
        xs = xq[:, :C_INNER]
        bm = xq[:, C_INNER:C_INNER + LANES]
        cm = xq[:, C_INNER + LANES:]
        bt_bf = bm.T.astype(BF16)
        gmat = []
        for g in range(SSM_GROUPS):
            cg = jnp.where(lo64 if g == 0 else jnp.logical_not(lo64), cm, 0.0).astype(BF16)
            gmat.append(_dot(cg, bt_bf))
        pf = expand(jnp.where(first8, jnp.exp(inc), 0.0), exf_ref)
        sf = expand(jnp.where(first8, jnp.exp(tot - inc) * dts, 0.0), exf_ref)
        return dict(r0=r0, ad=ad, inc=inc, exc=exc, xs=xs, bt_bf=bt_bf, cm_bf=cm.astype(BF16),
                    gmat=gmat, pf=pf, xsd=(xs * sf).astype(BF16))

    def fwd_state(v):
        y_off = _dot(v["cm_bf"], hf[...].astype(BF16)) * v["pf"]
        st = _dot(v["bt_bf"], v["xsd"])
        hf[...] = v["pf"][q - 1:q, :] * hf[...] + jnp.where(bd, st, 0.0)
        return y_off

    def fwd_heads(v, y_off):
        inc, exc, ad, r0 = v["inc"], v["exc"], v["ad"], v["r0"]
        inc_t = inc.T
        ad_t = ad.T
        exc_t = inc_t - ad_t
        xs_bf = v["xs"].astype(BF16)
        for pr in range(C_HEADS // 2):
            ys = []
            for hh in (2 * pr, 2 * pr + 1):
                icol = jnp.broadcast_to(inc[:, hh:hh + 1], (q, q))
                irow = jnp.broadcast_to(inc_t[hh:hh + 1, :], (q, q))
                lf = jnp.exp(jnp.where(lower, icol - irow, NEG_BIG)) * ad_t[16 + hh:17 + hh, :]
                ecol = jnp.broadcast_to(exc[:, 8 + hh:9 + hh], (q, q))
                erow = jnp.broadcast_to(exc_t[8 + hh:9 + hh, :], (q, q))
                lb = jnp.exp(jnp.where(upper, erow - ecol, NEG_BIG)) * ad_t[24 + hh:25 + hh, :]
                w = (v["gmat"][hh // (C_HEADS // SSM_GROUPS)] * (lf + lb)).astype(BF16)
                ys.append(_dot(w, xs_bf[:, pr * LANES:(pr + 1) * LANES]))
            cols = slice(pr * LANES, (pr + 1) * LANES)
            o_ref[0, pl.ds(r0, q), cols] = jnp.where(lo64, ys[0], ys[1]) + y_off[:, cols]

    def fwd_body(i, carry):
        va = fwd_pre(2 * i)
        vb = fwd_pre(2 * i + 1)
        ya = fwd_state(va)
        yb = fwd_state(vb)
        fwd_heads(va, ya)
        fwd_heads(vb, yb)
        return carry

    lax.fori_loop(0, nc // 2, fwd_body, 0)

    def bwd_body(i, carry):
        c = nc - 1 - i
        r0 = pl.multiple_of(c * q, q)
        ad, inc, exc, tot, dts = chunk_decay(r0)
        xq = xact[pl.ds(r0, q), :]
        xs = xq[:, :C_INNER]
        bt_bf = xq[:, C_INNER:C_INNER + LANES].T.astype(BF16)
        cm_bf = xq[:, C_INNER + LANES:].astype(BF16)
        pbk = expand(jnp.where(mid8, jnp.exp(tot - exc), 0.0), exb_ref)
        sbk = expand(jnp.where(mid8, jnp.exp(exc) * dts, 0.0), exb_ref)
        y = o_ref[0, pl.ds(r0, q), :] + _dot(cm_bf, hb[...].astype(BF16)) * pbk
        st = _dot(bt_bf, (xs * sbk).astype(BF16))
        hb[...] = pbk[0:1, :] * hb[...] + jnp.where(bd, st, 0.0)
        y = (y + dskip_ref[...] * xs) * _silu(z_ref[0, pl.ds(r0, q), :])
        o_ref[0, pl.ds(r0, q), :] = _rms_rows(y, g_ref[...])
        return carry

    lax.fori_loop(0, nc // 2, lambda i, cr: bwd_body(2 * i + 1, bwd_body(2 * i, cr)), 0)


def _ssd(xbc, dt, z, lw, consts):
    s, l, _ = xbc.shape
    assert l % (2 * SSD_Q) == 0
    seq = lambda w: pl.BlockSpec((1, l, w), lambda si: (si, 0, 0))
    full = lambda shape: pl.BlockSpec(shape, lambda si: (0,) * len(shape))
    return pl.pallas_call(
        _ssd_kernel,
        out_shape=jax.ShapeDtypeStruct((s, l, C_INNER), F32),
        grid=(s,),
        in_specs=[seq(CONV_DIM), seq(LANES), seq(C_INNER),
                  full((D_CONV, CONV_DIM)), full((1, CONV_DIM)), full((1, LANES)), full((1, LANES)),
                  full((1, C_INNER)), full((1, C_INNER)),
                  full((LANES, C_INNER)), full((LANES, C_INNER)), full((LANES, C_INNER))],
        out_specs=seq(C_INNER),
        scratch_shapes=[pltpu.VMEM((l + 16, CONV_DIM), F32), pltpu.VMEM((l, CONV_DIM), F32),
                        pltpu.VMEM((l, LANES), F32),
                        pltpu.VMEM((LANES, C_INNER), F32), pltpu.VMEM((LANES, C_INNER), F32)],
        compiler_params=_params(("parallel",), 56),
        name="ssd",
    )(xbc, dt, z, lw["conv_w"], lw["conv_b"], lw["dt_bias"], lw["a_log"], lw["d_skip"], lw["ssm_g"],
      consts["exf"], consts["exb"], consts["bd"])


def _route(logits):
    lane = lax.broadcasted_iota(jnp.int32, logits.shape, 1)
    lane_f = lane.astype(F32)
    neg_inf = -jnp.inf
    isg = (lane >= N_EXPERTS) & (lane < N_EXPERTS + N_EGROUPS)
    gl = jnp.where(isg, logits, neg_inf)
    gmax = jnp.max(gl, axis=-1, keepdims=True)
    gsel = jnp.min(jnp.where(gl == gmax, lane_f - N_EXPERTS, 1e9), axis=-1, keepdims=True)
    gw = 1.0 / jnp.sum(jnp.where(isg, jnp.exp(gl - gmax), 0.0), axis=-1, keepdims=True)
    grp = jnp.floor(lane_f * (1.0 / EXPERTS_PER_GROUP))
    el = jnp.where((lane < N_EXPERTS) & (grp == gsel), logits, neg_inf)
    l1 = jnp.max(el, axis=-1, keepdims=True)
    i1 = jnp.min(jnp.where(el == l1, lane_f, 1e9), axis=-1, keepdims=True)
    el2 = jnp.where(lane_f == i1, neg_inf, el)
    l2 = jnp.max(el2, axis=-1, keepdims=True)
    i2 = jnp.min(jnp.where(el2 == l2, lane_f, 1e9), axis=-1, keepdims=True)
    r = jnp.exp(l2 - l1)
    g1 = gw / (1.0 + r)
    g2 = g1 * r
    return jnp.where(lane == 0, i1, jnp.where(lane == 1, i2, jnp.where(lane == 2, g1,
                                                                    jnp.where(lane == 3, g2, 0.0))))


def _outproj_kernel(oa_ref, ob_ref, oc_ref, x_ref, mod_ref, wout_ref, goa_ref, gob_ref,
                    l1g_ref, l1b_ref, wrc_ref, br_ref, x1_o, h2v_o, route_o, ct):
    hm = x_ref.shape[1] // 2
    wa = A_HEADS * HEAD_DIM
    wb = wa + B_HEADS * MLA_V
    m = mod_ref[0]

    def mix(rows):
        oa = _rms_rows(oa_ref[0, rows, :], goa_ref[...]).astype(BF16)
        ob = _rms_rows(ob_ref[0, rows, :], gob_ref[...]).astype(BF16)
        oc = oc_ref[0, rows, :].astype(BF16)
        return _dot(oa, wout_ref[0:wa, :]) + _dot(ob, wout_ref[wa:wb, :]) + _dot(oc, wout_ref[wb:, :])

    halves = [pl.ds(0, hm), pl.ds(hm, hm)]
    ys = [mix(rows) for rows in halves]
    h2s = []
    for rows, y in zip(halves, ys):
        x1 = _ln_rows(DN_ALPHA * x_ref[0, rows, :] + (1.0 + m[2:3]) * y, l1g_ref[...], l1b_ref[...])
        x1_o[0, rows, :] = x1
        h2 = x1 * (1.0 + m[4:5]) + m[3:4]
        h2s.append(h2)
        hi, lo = _split_bf16(h2)
        hw = _dot(hi, wrc_ref[...])
        logits = hw[:, :LANES] + hw[:, LANES:] + _dot(lo, wrc_ref[:, :LANES]) + br_ref[...]
        route_o[0, rows, :] = _route(logits)
    _rows_to_tiles(jnp.concatenate(h2s, axis=0), ct, h2v_o)


def _outproj(oa, ob, oc, x, mod, lw, tm):
    s, l, d = x.shape
    nt = l // tm
    nch = d // LANES
    row = lambda w: pl.BlockSpec((1, tm, w), lambda si, i: (si, i, 0))
    full = lambda shape: pl.BlockSpec(shape, lambda si, i: (0,) * len(shape))
    return pl.pallas_call(
        _outproj_kernel,
        out_shape=(jax.ShapeDtypeStruct((s, l, d), F32), jax.ShapeDtypeStruct((s * l * nch, LANES), F32),
                   jax.ShapeDtypeStruct((s, l, LANES), F32)),
        grid=(s, l // tm),
        in_specs=[row(A_HEADS * HEAD_DIM), row(B_HEADS * MLA_V), row(C_INNER), row(d),
                  pl.BlockSpec((1, 6, d), lambda si, i: (si, 0, 0)),
                  full((d, d)), full((1, A_HEADS * HEAD_DIM)), full((1, B_HEADS * MLA_V)),
                  full((1, d)), full((1, d)), full((d, 2 * LANES)), full((1, LANES))],
        out_specs=(row(d), pl.BlockSpec((tm * nch, LANES), lambda si, i: (si * nt + i, 0)), row(LANES)),
        scratch_shapes=[pltpu.VMEM((nch * _pitch(tm), LANES), F32)],
        compiler_params=_params(("parallel", "parallel"), 48),
        name="outproj",
    )(oa, ob, oc, x, mod, lw["w_out"], lw["goa"], lw["gob"], lw["ln1_g"], lw["ln1_b"],
      lw["w_r"], lw["b_r"])


MOE_ROWS = 128
ROW_STRIDE = _pitch(MOE_ROWS)


def _slot_rows(l):
    return 2 * l + N_EXPERTS * 8 + MOE_ROWS


def _route_sort_kernel(route_ref, slots_o, meta_o):
    r = route_ref[0]
    l = r.shape[0]
    lane = lax.broadcasted_iota(jnp.int32, (l, LANES), 1)
    lane_f = lane.astype(F32)
    oh1 = lane_f == jnp.broadcast_to(r[:, 0:1], (l, LANES))
    oh2 = lane_f == jnp.broadcast_to(r[:, 1:2], (l, LANES))
    member = jnp.where(oh1 | oh2, 1.0, 0.0).astype(BF16)
    tb = min(256, l)
    ri = lax.broadcasted_iota(jnp.int32, (tb, tb), 0)
    ci = lax.broadcasted_iota(jnp.int32, (tb, tb), 1)
    stril = jnp.where(ri > ci, 1.0, 0.0).astype(BF16)
    run = jnp.zeros((1, LANES), F32)
    ranks = []
    for i in range(l // tb):
        blk = member[i * tb:(i + 1) * tb]
        ranks.append(_dot(stril, blk) + run)
        run = run + jnp.sum(blk.astype(F32), axis=0, keepdims=True)
    rank_all = jnp.concatenate(ranks, axis=0)
    padded = jnp.floor((run + 7.0) * 0.125) * 8.0
    hi, lo = _split_bf16(jnp.broadcast_to(padded, (8, LANES)))
    rl = lax.broadcasted_iota(jnp.int32, (LANES, LANES), 0)
    cl = lax.broadcasted_iota(jnp.int32, (LANES, LANES), 1)
    upper = jnp.where(rl < cl, 1.0, 0.0).astype(BF16)
    base = (_dot(hi, upper) + _dot(lo, upper))[0:1]
    pos = base + rank_all
    slot1 = jnp.sum(jnp.where(oh1, pos, 0.0), axis=-1, keepdims=True)
    slot2 = jnp.sum(jnp.where(oh2, pos, 0.0), axis=-1, keepdims=True)
    packed = jnp.where(lane == 0, slot1, jnp.where(lane == 1, slot2, 0.0))
    slots_o[0] = packed.T[0:2].astype(jnp.int32)
    meta_o[0] = jnp.concatenate([base, run, jnp.zeros((6, LANES), F32)], axis=0).astype(jnp.int32)


def _route_sort(route):
    s, l, _ = route.shape
    return pl.pallas_call(
        _route_sort_kernel,
        out_shape=(jax.ShapeDtypeStruct((s, 2, l), jnp.int32), jax.ShapeDtypeStruct((s, 8, LANES), jnp.int32)),
        grid=(s,),
        in_specs=[pl.BlockSpec((1, l, LANES), lambda si: (si, 0, 0))],
        out_specs=(pl.BlockSpec((1, 2, l), lambda si: (si, 0, 0)), pl.BlockSpec((1, 8, LANES), lambda si: (si, 0, 0))),
        compiler_params=_params(("parallel",), 32),
        name="route_sort",
    )(route)


MOE_EXPERTS_PER_STEP = 2
MOE_EXPERT_STEPS = N_EXPERTS // MOE_EXPERTS_PER_STEP


def _moe_kernel(meta_ref, h2v_ref, slots_hbm, wg_ref, wu_ref, wd_ref,
                x1_ref, route_ref, mod_ref, l2g_ref, l2b_ref, o_ref,
                slot_s, tok_s, y_v, xt, yt, ct1, ct2, sem):
    s = pl.program_id(0)
    e = pl.program_id(1)
    l = h2v_ref.shape[0] // 8
    nch = D_MODEL // LANES
    n_slots = tok_s.shape[0]
    spare = 8 * 2 * l

    @pl.when(e == 0)
    def _():
        c_slots = pltpu.make_async_copy(slots_hbm.at[s], slot_s, sem.at[0])
        c_slots.start()
        used = meta_ref[s, 0, N_EXPERTS]

        def clear_tail(g, c):
            for j in range(8):
                tok_s[used + g * 8 + j] = spare
            return c

        lax.fori_loop(0, (n_slots - used) // 8, clear_tail, 0)

        def clear(ex, c):
            n = meta_ref[s, 1, ex]
            last = meta_ref[s, 0, ex] + ((n + 7) // 8) * 8 - 8

            @pl.when(n > 0)
            def _():
                for j in range(8):
                    tok_s[last + j] = spare

            return c

        lax.fori_loop(0, N_EXPERTS, clear, 0)
        c_slots.wait()

        def invert(g, c):
            for j in range(8):
                t = g * 8 + j
                for k in range(2):
                    tok_s[slot_s[k, t]] = 8 * (2 * t + k)
            return c

        lax.fori_loop(0, l // 8, invert, 0)

    grp_per_blk = MOE_ROWS // 8

    def ffn(sub):
        xb = jnp.concatenate([xt[sub, j * ROW_STRIDE:j * ROW_STRIDE + MOE_ROWS, :] for j in range(nch)],
                             axis=-1).astype(BF16)
        mid = (_silu(_dot(xb, wg_ref[sub])) * _dot(xb, wu_ref[sub])).astype(BF16)
        y = _dot(mid, wd_ref[sub])
        for j in range(nch):
            yt[sub, j * ROW_STRIDE:j * ROW_STRIDE + MOE_ROWS, :] = y[:, j * LANES:(j + 1) * LANES]

    def gather_row(sub, slot, i):
        src = pl.multiple_of(lax.shift_right_logical(tok_s[slot], 1) & (8 * l - 8), 8)
        xt[sub, pl.ds(i, nch, stride=ROW_STRIDE), :] = h2v_ref[pl.ds(src, 8), :]

    def scatter_row(sub, slot, i):
        y_v[pl.ds(pl.multiple_of(tok_s[slot], 8), 8), :] = yt[sub, pl.ds(i, nch, stride=ROW_STRIDE), :]

    @pl.when(e < MOE_EXPERT_STEPS)
    def _():
        info = []
        for sub in range(MOE_EXPERTS_PER_STEP):
            ex = e * MOE_EXPERTS_PER_STEP + sub
            info.append((meta_ref[s, 0, ex], meta_ref[s, 1, ex]))
        for sub, (base, n) in enumerate(info):
            for i in range(MOE_ROWS):
                gather_row(sub, base + i, i)
        for sub in range(MOE_EXPERTS_PER_STEP):
            ffn(sub)
        for sub, (base, n) in enumerate(info):
            for i in range(MOE_ROWS):
                scatter_row(sub, base + i, i)
        for sub, (base, n) in enumerate(info):
            ngrp = (n + 7) // 8

            def block(blk, c, sub=sub, base=base, ngrp=ngrp):
                g_lo = blk * grp_per_blk
                ng = jnp.minimum(grp_per_blk, ngrp - g_lo)

                def gather(g, cc):
                    for j in range(8):
                        gather_row(sub, base + (g_lo + g) * 8 + j, g * 8 + j)
                    return cc

                lax.fori_loop(0, ng, gather, 0)
                ffn(sub)

                def scatter(g, cc):
                    for j in range(8):
                        scatter_row(sub, base + (g_lo + g) * 8 + j, g * 8 + j)
                    return cc

                lax.fori_loop(0, ng, scatter, 0)
                return c

            lax.fori_loop(1, (ngrp + grp_per_blk - 1) // grp_per_blk, block, 0)

    @pl.when(e >= MOE_EXPERT_STEPS)
    def _():
        tmo = x1_ref.shape[1]
        t0 = (e - MOE_EXPERT_STEPS) * tmo
        r = route_ref[0]
        y = (r[:, 2:3] * _tiles_to_rows(y_v, t0, tmo, ct1, 2, 0)
             + r[:, 3:4] * _tiles_to_rows(y_v, t0, tmo, ct2, 2, 1))
        m = mod_ref[0]
        o_ref[0] = _ln_rows(DN_ALPHA * x1_ref[0] + (1.0 + m[5:6]) * y, l2g_ref[...], l2b_ref[...])


def _moe(h2v, slots, meta, x1, route, mod, lw, tm):
    s, l, d = x1.shape
    assert l & (l - 1) == 0, "the gather address mask needs a power-of-two sequence length"
    n_slots = _slot_rows(l)
    eps = MOE_EXPERTS_PER_STEP
    wspec = lambda shape: pl.BlockSpec(shape, lambda si, e: (jnp.minimum(e, MOE_EXPERT_STEPS - 1), 0, 0))
    blk_idx = lambda si, e: (si, jnp.maximum(e - MOE_EXPERT_STEPS, 0), 0)
    tok_blk = pl.BlockSpec((1, tm, d), blk_idx)
    vec = pl.BlockSpec((1, d), lambda si, e: (0, 0))
    return pl.pallas_call(
        _moe_kernel,
        out_shape=jax.ShapeDtypeStruct((s, l, d), F32),
        grid=(s, MOE_EXPERT_STEPS + l // tm),
        in_specs=[pl.BlockSpec(memory_space=pltpu.SMEM),
                  pl.BlockSpec((l * 8, LANES), lambda si, e: (si, 0)),
                  pl.BlockSpec(memory_space=pl.ANY),
                  wspec((eps, d, D_FF_EXPERT)), wspec((eps, d, D_FF_EXPERT)), wspec((eps, D_FF_EXPERT, d)),
                  tok_blk, pl.BlockSpec((1, tm, LANES), blk_idx),
                  pl.BlockSpec((1, 6, d), lambda si, e: (si, 0, 0)), vec, vec],
        out_specs=tok_blk,
        scratch_shapes=[pltpu.SMEM((2, l), jnp.int32), pltpu.SMEM((n_slots,), jnp.int32),
                        pltpu.VMEM(((2 * l + 1) * 8, LANES), F32),
                        pltpu.VMEM((eps, 8 * ROW_STRIDE, LANES), F32), pltpu.VMEM((eps, 8 * ROW_STRIDE, LANES), F32),
                        pltpu.VMEM((8 * _pitch(tm), LANES), F32), pltpu.VMEM((8 * _pitch(tm), LANES), F32),
                        pltpu.SemaphoreType.DMA((1,))],
        compiler_params=_params(("parallel", "arbitrary"), 60),
        name="moe",
    )(meta, h2v, slots, lw["w_e_gate"], lw["w_e_up"], lw["w_e_down"], x1, route, mod,
      lw["ln2_g"], lw["ln2_b"])


def _rope_tables(seq_len):
    t = jnp.arange(seq_len, dtype=jnp.int32)
    row = (t // GRID_W).astype(F32)[:, None]
    col = (t % GRID_W).astype(F32)[:, None]

    def axis_tables(d_axis):
        inv = ROPE_THETA ** (-jnp.arange(0, d_axis, 2, dtype=F32) / d_axis)
        zero = jnp.zeros((seq_len, d_axis // 2), F32)
        cs, ps, ms = [], [], []
        for pos in (row, col):
            ang = pos * inv
            c, sn = jnp.cos(ang), jnp.sin(ang)
            cs += [c, c]
            ps += [zero, sn]
            ms += [-sn, zero]
        return [jnp.concatenate(v, axis=-1) for v in (cs, ps, ms)]

    ta = [jnp.tile(v, (1, 2)) for v in axis_tables(HEAD_DIM // 2)]
    one = jnp.ones((seq_len, MLA_NOPE), F32)
    zero = jnp.zeros((seq_len, MLA_NOPE), F32)
    pad1 = jnp.ones((seq_len, LANES - MLA_NOPE - MLA_ROPE), F32)
    pad0 = jnp.zeros((seq_len, LANES - MLA_NOPE - MLA_ROPE), F32)
    cb, pb, mb = axis_tables(MLA_ROPE // 2)
    tb = [jnp.concatenate([one, cb, pad1], -1), jnp.concatenate([zero, pb, pad0], -1),
          jnp.concatenate([zero, mb, pad0], -1)]
    return ta + tb


def _ssd_consts():
    r = np.arange(LANES)[:, None]
    c = np.arange(C_INNER)[None, :]
    exf = ((r < C_HEADS) & (c // C_HEAD_DIM == r)).astype(np.float32)
    exb = ((r >= C_HEADS) & (r < 2 * C_HEADS) & (c // C_HEAD_DIM == r - C_HEADS)).astype(np.float32)
    heads_per_group = C_HEADS // SSM_GROUPS
    bd = (r // D_STATE == c // (C_HEAD_DIM * heads_per_group)).astype(np.float32)
    return {"exf": jnp.asarray(exf, BF16), "exb": jnp.asarray(exb, BF16), "bd": jnp.asarray(bd, F32)}


def _pack_weights(p):
    w_in = p["w_in"]
    depth = w_in.shape[0]
    off = np.cumsum((0, 256, 128, 128, Q_LORA, KV_LORA, MLA_ROPE, C_INNER, CONV_DIM, 2 * C_HEADS))
    w_kr = w_in[:, :, off[5]:off[6]]
    zpad = lambda n: jnp.zeros((depth, D_MODEL, n), F32)
    dtkr = jnp.concatenate([w_in[:, :, off[8]:], zpad(KR_LANE - 2 * C_HEADS), w_kr,
                            zpad(LANES - KR_LANE - MLA_ROPE)], -1)
    w_in_p = jnp.concatenate([w_in[:, :, :off[5]], w_in[:, :, off[6]:off[8]], dtkr], -1).astype(BF16)
    dq = MLA_NOPE + MLA_ROPE
    w_uq = p["w_uq"].reshape(depth, Q_LORA, B_HEADS, dq)
    w_uq_p = jnp.concatenate([w_uq, jnp.zeros((depth, Q_LORA, B_HEADS, LANES - dq), F32)], -1)
    w_ukv = p["w_ukv"].reshape(depth, KV_LORA, B_HEADS, MLA_NOPE + MLA_V)
    w_ukk = jnp.concatenate([w_ukv[..., :MLA_NOPE],
                             jnp.zeros((depth, KV_LORA, B_HEADS, LANES - MLA_NOPE), F32)], -1)
    pad_lanes = lambda v: jnp.concatenate([v, jnp.zeros(v.shape[:-1] + (LANES - v.shape[-1],), F32)], -1)
    w_r = pad_lanes(jnp.concatenate([p["w_re"], p["w_rg"]], -1))
    w_r_hi = w_r.astype(BF16)
    return {
        "w_ada": p["w_ada"].astype(BF16), "b_ada": p["b_ada"],
        "w_in": w_in_p,
        "w_uq": w_uq_p.reshape(depth, Q_LORA, B_HEADS * LANES).astype(BF16),
        "w_ukk": w_ukk.reshape(depth, KV_LORA, B_HEADS * LANES).astype(BF16),
        "w_ukv": w_ukv[..., MLA_NOPE:].reshape(depth, KV_LORA, B_HEADS * MLA_V).astype(BF16),
        "gqa": jnp.tile(p["qa_norm_g"], (1, 2))[:, None, :], "gka": jnp.tile(p["ka_norm_g"], (1, 2))[:, None, :],
        "gqb": p["qb_norm_g"][:, None, :], "gkvb": p["kvb_norm_g"][:, None, :],
        "goa": p["oa_norm_g"][:, None, :], "gob": p["ob_norm_g"][:, None, :],
        "conv_w": p["conv_w"], "conv_b": p["conv_b"][:, None, :],
        "dt_bias": pad_lanes(p["dt_bias"].reshape(depth, 2 * C_HEADS))[:, None, :],
        "a_log": pad_lanes(p["a_log"].reshape(depth, 2 * C_HEADS))[:, None, :],
        "d_skip": jnp.repeat(p["d_skip"], C_HEAD_DIM, axis=-1)[:, None, :],
        "ssm_g": p["ssm_norm_g"][:, None, :],
        "w_out": p["w_out"].astype(BF16),
        "ln1_g": p["ln1_g"][:, None, :], "ln1_b": p["ln1_b"][:, None, :],
        "w_r": jnp.concatenate([w_r_hi, (w_r - w_r_hi.astype(F32)).astype(BF16)], -1),
        "b_r": pad_lanes(jnp.concatenate([p["b_re"], p["b_rg"]], -1))[:, None, :],
        "w_e_gate": p["w_e_gate"].astype(BF16), "w_e_up": p["w_e_up"].astype(BF16),
        "w_e_down": p["w_e_down"].astype(BF16),
        "ln2_g": p["ln2_g"][:, None, :], "ln2_b": p["ln2_b"][:, None, :],
    }


def _trunk(x, c, p, pw=None, tm=512, tq=1024):
    s, l, d = x.shape
    tm = min(tm, l)
    tq = min(tq, l)
    pw = _pack_weights(p) if pw is None else pw
    tabs = _rope_tables(l)
    consts = _ssd_consts()
    mods = _ada(c, pw["w_ada"], pw["b_ada"]).reshape(DEPTH, s, 6, d)
    x = _ln0(x.reshape(s * l, d), p["ln0_g"][None, :], p["ln0_b"][None, :], tm).reshape(s, l, d)
    for layer in range(DEPTH):
        lw = {k: v[layer] for k, v in pw.items() if k not in ("w_ada", "b_ada")}
        mod = mods[layer]
        qa, ka, va, qb, kb, vb, z, xbc, dt = _inproj(x, mod, lw, tabs, tm)
        oa = _attention(qa, ka, va, tq, "attn_a")
        ob = _attention(qb, kb, vb, tq, "attn_b")
        oc = _ssd(xbc, dt, z, lw, consts)
        x1, h2v, route = _outproj(oa, ob, oc, x, mod, lw, tm)
        slots, meta = _route_sort(route)
        x = _moe(h2v, slots, meta, x1, route, mod, lw, tm)
    return x


def kernel(x_prompt, x_sample, c_prompt, c_sample, ln0_g, ln0_b, w_ada, b_ada, w_in, qa_norm_g, ka_norm_g, oa_norm_g, qb_norm_g, w_uq, kvb_norm_g, w_ukv, ob_norm_g, conv_w, conv_b, dt_bias, a_log, d_skip, ssm_norm_g, w_out, ln1_g, ln1_b, w_rg, b_rg, w_re, b_re, w_e_gate, w_e_up, w_e_down, ln2_g, ln2_b):
    p = dict(ln0_g=ln0_g, ln0_b=ln0_b, w_ada=w_ada, b_ada=b_ada, w_in=w_in,
             qa_norm_g=qa_norm_g, ka_norm_g=ka_norm_g, oa_norm_g=oa_norm_g,
             qb_norm_g=qb_norm_g, w_uq=w_uq, kvb_norm_g=kvb_norm_g, w_ukv=w_ukv, ob_norm_g=ob_norm_g,
             conv_w=conv_w, conv_b=conv_b, dt_bias=dt_bias, a_log=a_log, d_skip=d_skip,
             ssm_norm_g=ssm_norm_g, w_out=w_out, ln1_g=ln1_g, ln1_b=ln1_b,
             w_rg=w_rg, b_rg=b_rg, w_re=w_re, b_re=b_re,
             w_e_gate=w_e_gate, w_e_up=w_e_up, w_e_down=w_e_down, ln2_g=ln2_g, ln2_b=ln2_b)
    pw = _pack_weights(p)
    return (_trunk(x_prompt, c_prompt, p, pw), _trunk(x_sample, c_sample, p, pw))
```

```python
import functools

import jax
import jax.numpy as jnp
import numpy as np
from jax import lax
from jax.experimental import pallas as pl
from jax.experimental.pallas import tpu as pltpu

F32 = jnp.float32
BF16 = jnp.bfloat16

D_MODEL = 1024
DEPTH = 4
GRID_W = 64
ROPE_THETA = 10000.0
HEAD_DIM = 64
A_HEADS = 4
A_KV_HEADS = 2
B_HEADS = 4
Q_LORA = 384
KV_LORA = 128
MLA_NOPE = 64
MLA_ROPE = 32
MLA_V = 64
C_HEADS = 8
C_HEAD_DIM = 64
SSM_GROUPS = 2
D_STATE = 64
D_CONV = 5
C_INNER = C_HEADS * C_HEAD_DIM
CONV_DIM = C_INNER + 2 * SSM_GROUPS * D_STATE
N_EGROUPS = 4
EXPERTS_PER_GROUP = 8
N_EXPERTS = N_EGROUPS * EXPERTS_PER_GROUP
D_FF_EXPERT = 256
DN_ALPHA = (2 * DEPTH) ** 0.25
EPS = 1e-6

LANES = 128
SSD_Q = 128
NEG_BIG = -1e30

P_QA, P_KA, P_VA, P_CQ, P_CKV, P_Z, P_XBC, P_DTKR, P_END = (
    0, 256, 384, 512, 896, 1024, 1536, 2304, 2432)
KR_LANE = MLA_NOPE


def _params(sem, vmem_mb):
    return pltpu.CompilerParams(dimension_semantics=sem, vmem_limit_bytes=vmem_mb * 1024 * 1024)


def _silu(x):
    return x * (1.0 / (1.0 + jnp.exp(-x)))


def _ln_rows(x, g, b):
    mu = jnp.mean(x, axis=-1, keepdims=True)
    xc = x - mu
    var = jnp.mean(xc * xc, axis=-1, keepdims=True)
    return xc * lax.rsqrt(var + EPS) * g + b


def _rms_rows(x, g):
    return x * lax.rsqrt(jnp.mean(x * x, axis=-1, keepdims=True) + EPS) * g


def _split_bf16(x):
    hi = x.astype(BF16)
    lo = (x - hi.astype(F32)).astype(BF16)
    return hi, lo


def _dot(a, b):
    return jnp.dot(a, b, preferred_element_type=F32)


def _dot_nt(a, b):
    return lax.dot_general(a, b, (((1,), (1,)), ((), ())), preferred_element_type=F32)


def _pitch(rows):
    return rows + 8


def _rows_to_tiles(x, ct, out_ref):
    m = x.shape[0]
    nch = x.shape[1] // LANES
    p = _pitch(m)
    for j in range(nch):
        ct[j * p:j * p + m, :] = x[:, j * LANES:(j + 1) * LANES]

    def grp(g, c):
        for j in range(8):
            r = g * 8 + j
            out_ref[pl.ds(pl.multiple_of(r * 8, 8), 8), :] = ct[pl.ds(r, nch, stride=p), :]
        return c

    lax.fori_loop(0, m // 8, grp, 0)


def _tiles_to_rows(src_ref, row0, m, ct, step=1, offset=0):
    nch = D_MODEL // LANES
    p = _pitch(m)

    def grp(g, c):
        for j in range(8):
            r = g * 8 + j
            src = pl.multiple_of(((row0 + r) * step + offset) * 8, 8)
            ct[pl.ds(r, nch, stride=p), :] = src_ref[pl.ds(src, 8), :]
        return c

    lax.fori_loop(0, m // 8, grp, 0)
    return jnp.concatenate([ct[j * p:j * p + m, :] for j in range(nch)], axis=-1)


def _ada_kernel(c_ref, w_ref, b_ref, o_ref):
    cs = _silu(c_ref[...]).astype(BF16)
    o_ref[0] = _dot(cs, w_ref[0]) + b_ref[0]


def _ada(c, w_bf, b, tn=2048):
    s, d = c.shape
    depth, _, n = w_bf.shape
    return pl.pallas_call(
        _ada_kernel,
        out_shape=jax.ShapeDtypeStruct((depth, s, n), F32),
        grid=(depth, n // tn),
        in_specs=[pl.BlockSpec((s, d), lambda l, j: (0, 0)),
                  pl.BlockSpec((1, d, tn), lambda l, j: (l, 0, j)),
                  pl.BlockSpec((1, 1, tn), lambda l, j: (l, 0, j))],
        out_specs=pl.BlockSpec((1, s, tn), lambda l, j: (l, 0, j)),
        compiler_params=_params(("parallel", "parallel"), 32),
        name="ada",
    )(c, w_bf, b.reshape(depth, 1, n))


def _rope(x, c, p, m, shift):
    return x * c + pltpu.roll(x, shift, 1) * p + pltpu.roll(x, LANES - shift, 1) * m


def _headnorm2(x, g):
    xx = x * x
    lo = lax.broadcasted_iota(jnp.int32, x.shape, 1) < HEAD_DIM
    s0 = jnp.sum(jnp.where(lo, xx, 0.0), axis=-1, keepdims=True)
    s1 = jnp.sum(jnp.where(lo, 0.0, xx), axis=-1, keepdims=True)
    inv = jnp.where(lo, lax.rsqrt(s0 * (1.0 / HEAD_DIM) + EPS), lax.rsqrt(s1 * (1.0 / HEAD_DIM) + EPS))
    return x * inv * g


def _store_values(v2, v_o, h0):
    lo = lax.broadcasted_iota(jnp.int32, v2.shape, 1) < ATT_DV
    v_o[0, h0] = jnp.where(lo, v2, 1.0).astype(BF16)
    v_o[0, h0 + 1] = jnp.where(lo, pltpu.roll(v2, ATT_DV, 1), 1.0).astype(BF16)


def _inproj_kernel(x_ref, mod_ref, w_ref, wuq_ref, wukk_ref, wukv_ref,
                   gqa_ref, gka_ref, gqb_ref, gkvb_ref,
                   ca_ref, pa_ref, ma_ref, cb_ref, pb_ref, mb_ref, g0_ref, b0_ref,
                   qa_o, ka_o, va_o, qb_o, kb_o, vb_o, z_o, xbc_o, dt_o, *xln_o):
    m = mod_ref[0]
    x = x_ref[0]
    if xln_o:
        x = _ln_rows(x, g0_ref[...], b0_ref[...])
        xln_o[0][0] = x
    h = (x * (1.0 + m[1:2]) + m[0:1]).astype(BF16)

    def proj(a, b):
        return _dot(h, w_ref[:, a:b])

    ca, pa, ma = ca_ref[...], pa_ref[...], ma_ref[...]
    cb, pb, mb = cb_ref[...], pb_ref[...], mb_ref[...]
    half = HEAD_DIM // 2
    pq = proj(P_QA, P_KA)
    for ch in range(A_HEADS // 2):
        xa = _headnorm2(pq[:, ch * LANES:(ch + 1) * LANES], gqa_ref[...])
        xa = _rope(xa, ca, pa, ma, half // 2) * (HEAD_DIM ** -0.5)
        qa_o[0, 2 * ch] = xa[:, :HEAD_DIM].astype(BF16)
        qa_o[0, 2 * ch + 1] = xa[:, HEAD_DIM:].astype(BF16)
    pkv = proj(P_KA, P_CQ)
    xk = _rope(_headnorm2(pkv[:, :LANES], gka_ref[...]), ca, pa, ma, half // 2)
    ka_o[0, 0] = xk[:, :HEAD_DIM].astype(BF16)
    ka_o[0, 1] = xk[:, HEAD_DIM:].astype(BF16)
    _store_values(pkv[:, LANES:], va_o, 0)
    pc = proj(P_CQ, P_Z)
    cqn = _rms_rows(pc[:, :Q_LORA], gqb_ref[...]).astype(BF16)
    qb = _dot(cqn, wuq_ref[...])
    ckvn = _rms_rows(pc[:, Q_LORA:], gkvb_ref[...]).astype(BF16)
    kn = _dot(ckvn, wukk_ref[...])
    vv = _dot(ckvn, wukv_ref[...])
    for pr in range(B_HEADS // 2):
        _store_values(vv[:, pr * LANES:(pr + 1) * LANES], vb_o, 2 * pr)
    px = proj(P_XBC, P_END)
    dtkr = px[:, CONV_DIM:]
    lane = lax.broadcasted_iota(jnp.int32, dtkr.shape, 1)
    is_kr = (lane >= KR_LANE) & (lane < KR_LANE + MLA_ROPE)
    k_rope = _rope(jnp.where(is_kr, dtkr, 0.0), cb, pb, mb, MLA_ROPE // 4)
    scale_b = (MLA_NOPE + MLA_ROPE) ** -0.5
    for hh in range(B_HEADS):
        sl = slice(hh * LANES, (hh + 1) * LANES)
        qb_o[0, hh] = (_rope(qb[:, sl], cb, pb, mb, MLA_ROPE // 4) * scale_b).astype(BF16)
        kb_o[0, hh] = (kn[:, sl] + k_rope).astype(BF16)
    z_o[0] = proj(P_Z, P_XBC)
    xbc_o[0] = px[:, :CONV_DIM]
    dt_o[0] = jnp.where(lane < 2 * C_HEADS, dtkr, 0.0)


def _inproj(x, mod, lw, tabs, tm, ln0=None):
    s, l, d = x.shape
    nt = l // tm
    full = lambda shape: pl.BlockSpec(shape, lambda si, i: (0,) * len(shape))
    tab = pl.BlockSpec((tm, LANES), lambda si, i: (i, 0))
    head = lambda nh, dh: pl.BlockSpec((1, nh, tm, dh), lambda si, i: (si, 0, i, 0))
    row = lambda w: pl.BlockSpec((1, tm, w), lambda si, i: (si, i, 0))
    out_shape = (
        jax.ShapeDtypeStruct((s, A_HEADS, l, HEAD_DIM), BF16),
        jax.ShapeDtypeStruct((s, A_KV_HEADS, l, HEAD_DIM), BF16),
        jax.ShapeDtypeStruct((s, A_KV_HEADS, l, LANES), BF16),
        jax.ShapeDtypeStruct((s, B_HEADS, l, LANES), BF16),
        jax.ShapeDtypeStruct((s, B_HEADS, l, LANES), BF16),
        jax.ShapeDtypeStruct((s, B_HEADS, l, LANES), BF16),
        jax.ShapeDtypeStruct((s, l, C_INNER), F32),
        jax.ShapeDtypeStruct((s, l, CONV_DIM), F32),
        jax.ShapeDtypeStruct((s, l, LANES), F32),
    )
    out_specs = (head(A_HEADS, HEAD_DIM), head(A_KV_HEADS, HEAD_DIM), head(A_KV_HEADS, LANES),
                 head(B_HEADS, LANES), head(B_HEADS, LANES), head(B_HEADS, LANES),
                 row(C_INNER), row(CONV_DIM), row(LANES))
    if ln0 is None:
        ln0 = (jnp.ones((1, d), F32), jnp.zeros((1, d), F32))
    else:
        out_shape += (jax.ShapeDtypeStruct((s, l, d), F32),)
        out_specs += (row(d),)
    return pl.pallas_call(
        _inproj_kernel,
        out_shape=out_shape,
        grid=(s, nt),
        in_specs=[row(d),
                  pl.BlockSpec((1, 6, d), lambda si, i: (si, 0, 0)),
                  full((d, P_END)), full((Q_LORA, B_HEADS * LANES)),
                  full((KV_LORA, B_HEADS * LANES)), full((KV_LORA, B_HEADS * MLA_V)),
                  full((1, LANES)), full((1, LANES)), full((1, Q_LORA)), full((1, KV_LORA)),
                  tab, tab, tab, tab, tab, tab, full((1, d)), full((1, d))],
        out_specs=out_specs,
        compiler_params=_params(("parallel", "parallel"), 48),
        name="inproj",
    )(x, mod, lw["w_in"], lw["w_uq"], lw["w_ukk"], lw["w_ukv"],
      lw["gqa"], lw["gka"], lw["gqb"], lw["gkvb"], *tabs, *ln0)


ATT_DV = 64
ATT_Q_ROWS = 256


def _attn_kernel(q_ref, k_ref, v_ref, o_ref, *, shared_kv):
    tq = q_ref.shape[2]
    qr = min(ATT_Q_ROWS, tq)
    items = [(j, r) for j in range(2) for r in range(tq // qr)]

    def scores(j, r):
        kj = 0 if shared_kv else j
        return _dot_nt(q_ref[0, j, r * qr:(r + 1) * qr, :], k_ref[0, kj])

    s_next = scores(*items[0])
    outs = {}
    for i, (j, r) in enumerate(items):
        s = s_next
        if i + 1 < len(items):
            s_next = scores(*items[i + 1])
        kj = 0 if shared_kv else j
        p = jnp.exp(s - jnp.max(s, axis=-1, keepdims=True)).astype(BF16)
        o = _dot(p, v_ref[0, kj])
        outs[(j, r)] = o[:, :ATT_DV] / o[:, ATT_DV:ATT_DV + 1]
    for r in range(tq // qr):
        o_ref[0, r * qr:(r + 1) * qr, :] = jnp.concatenate([outs[(0, r)], outs[(1, r)]], axis=-1)


def _attention(q, k, v, tq, name):
    s, hq, l, dk = q.shape
    hk = k.shape[1]
    shared = hq // hk == 2
    kvb = 1 if shared else 2
    return pl.pallas_call(
        functools.partial(_attn_kernel, shared_kv=shared),
        out_shape=jax.ShapeDtypeStruct((s, l, hq * ATT_DV), F32),
        grid=(s, hq // 2, l // tq),
        in_specs=[pl.BlockSpec((1, 2, tq, dk), lambda si, hp, qi: (si, hp, qi, 0)),
                  pl.BlockSpec((1, kvb, l, dk), lambda si, hp, qi: (si, hp, 0, 0)),
                  pl.BlockSpec((1, kvb, l, LANES), lambda si, hp, qi: (si, hp, 0, 0))],
        out_specs=pl.BlockSpec((1, tq, 2 * ATT_DV), lambda si, hp, qi: (si, qi, hp)),
        compiler_params=_params(("parallel", "parallel", "arbitrary"), 48),
        name=name,
    )(q, k, v)


def _ssd_kernel(xbc_ref, dt_ref, z_ref, cw_ref, cb_ref, dtb_ref, alog_ref, dskip_ref, g_ref,
                exf_ref, exb_ref, bd_ref, o_ref, xpad, xact, ad_s, hf, hb):
    l = xbc_ref.shape[1]
    nc = l // SSD_Q
    q = SSD_Q
    pad = 8
    xpad[0:pad, :] = jnp.zeros((pad, CONV_DIM), F32)
    xpad[l + pad:l + 2 * pad, :] = jnp.zeros((pad, CONV_DIM), F32)
    xpad[pad:l + pad, :] = xbc_ref[0]
    def conv_chunk(c, carry):
        r0 = pl.multiple_of(c * q, q)
        for st in range(CONV_DIM // LANES):
            cols = slice(st * LANES, (st + 1) * LANES)
            win = xpad[pl.ds(r0, q + 2 * pad), cols]
            acc = jnp.zeros((q, LANES), F32) + cb_ref[:, cols]
            for k in range(D_CONV):
                lo = pad - D_CONV // 2 + k
                acc = acc + win[lo:lo + q] * cw_ref[k:k + 1, cols]
            xact[pl.ds(r0, q), cols] = _silu(acc)
        return carry

    lax.fori_loop(0, nc, conv_chunk, 0)
    dtr = dt_ref[0] + dtb_ref[...]
    dt = jnp.maximum(dtr, 0.0) + jnp.log1p(jnp.exp(-jnp.abs(dtr)))
    lane_l = lax.broadcasted_iota(jnp.int32, dt.shape, 1)
    a = dt * (-jnp.exp(alog_ref[...]))
    ad_s[...] = jnp.where(lane_l < 2 * C_HEADS, a, pltpu.roll(dt, 2 * C_HEADS, 1))

    ri = lax.broadcasted_iota(jnp.int32, (q, q), 0)
    ci = lax.broadcasted_iota(jnp.int32, (q, q), 1)
    lower = ri >= ci
    upper = ci >= ri
    tril = jnp.where(lower, 1.0, 0.0).astype(BF16)
    lane_q = lax.broadcasted_iota(jnp.int32, (q, LANES), 1)
    first8 = lane_q < C_HEADS
    mid8 = (lane_q >= C_HEADS) & (lane_q < 2 * C_HEADS)
    lo64 = lane_q < C_HEAD_DIM
    bd = bd_ref[...] > 0.5

    def chunk_decay(r0):
        ad = ad_s[pl.ds(r0, q), :]
        hi, lo = _split_bf16(ad)
        inc = _dot(tril, hi) + _dot(tril, lo)
        exc = inc - ad
        tot = inc[q - 1:q, :]
        dts = pltpu.roll(ad, LANES - 2 * C_HEADS, 1)
        return ad, inc, exc, tot, dts

    def expand(v, ex_ref):
        hi, lo = _split_bf16(v)
        return _dot(hi, ex_ref[...]) + _dot(lo, ex_ref[...])

    hf[...] = jnp.zeros_like(hf)
    hb[...] = jnp.zeros_like(hb)

    def fwd_pre(c):
        r0 = pl.multiple_of(c * q, q)
        ad, inc, exc, tot, dts = chunk_decay(r0)
        xq = xact[pl.ds(r0, q), :]
        xs = xq[:, :C_INNER]
        bm = xq[:, C_INNER:C_INNER + LANES]
        cm = xq[:, C_INNER + LANES:]
        bt_bf = bm.T.astype(BF16)
        gmat = []
        for g in range(SSM_GROUPS):
            cg = jnp.where(lo64 if g == 0 else jnp.logical_not(lo64), cm, 0.0).astype(BF16)
            gmat.append(_dot(cg, bt_bf))
        pf = expand(jnp.where(first8, jnp.exp(inc), 0.0), exf_ref)
        sf = expand(jnp.where(first8, jnp.exp(tot - inc) * dts, 0.0), exf_ref)
        return dict(r0=r0, ad=ad, inc=inc, exc=exc, xs=xs, bt_bf=bt_bf, cm_bf=cm.astype(BF16),
                    gmat=gmat, pf=pf, xsd=(xs * sf).astype(BF16))

    def fwd_state(v):
        y_off = _dot(v["cm_bf"], hf[...].astype(BF16)) * v["pf"]
        st = _dot(v["bt_bf"], v["xsd"])
        hf[...] = v["pf"][q - 1:q, :] * hf[...] + jnp.where(bd, st, 0.0)
        return y_off

    def fwd_heads(v, y_off):
        inc, exc, ad, r0 = v["inc"], v["exc"], v["ad"], v["r0"]
        inc_t = inc.T
        ad_t = ad.T
        exc_t = inc_t - ad_t
        xs_bf = v["xs"].astype(BF16)
        for pr in range(C_HEADS // 2):
            ys = []
            for hh in (2 * pr, 2 * pr + 1):
                icol = jnp.broadcast_to(inc[:, hh:hh + 1], (q, q))
                irow = jnp.broadcast_to(inc_t[hh:hh + 1, :], (q, q))
                lf = jnp.exp(jnp.where(lower, icol - irow, NEG_BIG)) * ad_t[16 + hh:17 + hh, :]
                ecol = jnp.broadcast_to(exc[:, 8 + hh:9 + hh], (q, q))
                erow = jnp.broadcast_to(exc_t[8 + hh:9 + hh, :], (q, q))
                lb = jnp.exp(jnp.where(upper, erow - ecol, NEG_BIG)) * ad_t[24 + hh:25 + hh, :]
                w = (v["gmat"][hh // (C_HEADS // SSM_GROUPS)] * (lf + lb)).astype(BF16)
                ys.append(_dot(w, xs_bf[:, pr * LANES:(pr + 1) * LANES]))
            cols = slice(pr * LANES, (pr + 1) * LANES)
            o_ref[0, pl.ds(r0, q), cols] = jnp.where(lo64, ys[0], ys[1]) + y_off[:, cols]

    def fwd_body(i, carry):
        va = fwd_pre(2 * i)
        vb = fwd_pre(2 * i + 1)
        ya = fwd_state(va)
        yb = fwd_state(vb)
        fwd_heads(va, ya)
        fwd_heads(vb, yb)
        return carry

    lax.fori_loop(0, nc // 2, fwd_body, 0)

    def bwd_body(i, carry):
        c = nc - 1 - i
        r0 = pl.multiple_of(c * q, q)
        ad, inc, exc, tot, dts = chunk_decay(r0)
        xq = xact[pl.ds(r0, q), :]
        xs = xq[:, :C_INNER]
        bt_bf = xq[:, C_INNER:C_INNER + LANES].T.astype(BF16)
        cm_bf = xq[:, C_INNER + LANES:].astype(BF16)
        pbk = expand(jnp.where(mid8, jnp.exp(tot - exc), 0.0), exb_ref)
        sbk = expand(jnp.where(mid8, jnp.exp(exc) * dts, 0.0), exb_ref)
        y = o_ref[0, pl.ds(r0, q), :] + _dot(cm_bf, hb[...].astype(BF16)) * pbk
        st = _dot(bt_bf, (xs * sbk).astype(BF16))
        hb[...] = pbk[0:1, :] * hb[...] + jnp.where(bd, st, 0.0)
        y = (y + dskip_ref[...] * xs) * _silu(z_ref[0, pl.ds(r0, q), :])
        o_ref[0, pl.ds(r0, q), :] = _rms_rows(y, g_ref[...])
        return carry

    lax.fori_loop(0, nc // 2, lambda i, cr: bwd_body(2 * i + 1, bwd_body(2 * i, cr)), 0)


def _ssd(xbc, dt, z, lw, consts):
    s, l, _ = xbc.shape
    assert l % (2 * SSD_Q) == 0
    seq = lambda w: pl.BlockSpec((1, l, w), lambda si: (si, 0, 0))
    full = lambda shape: pl.BlockSpec(shape, lambda si: (0,) * len(shape))
    return pl.pallas_call(
        _ssd_kernel,
        out_shape=jax.ShapeDtypeStruct((s, l, C_INNER), F32),
        grid=(s,),
        in_specs=[seq(CONV_DIM), seq(LANES), seq(C_INNER),
                  full((D_CONV, CONV_DIM)), full((1, CONV_DIM)), full((1, LANES)), full((1, LANES)),
                  full((1, C_INNER)), full((1, C_INNER)),
                  full((LANES, C_INNER)), full((LANES, C_INNER)), full((LANES, C_INNER))],
        out_specs=seq(C_INNER),
        scratch_shapes=[pltpu.VMEM((l + 16, CONV_DIM), F32), pltpu.VMEM((l, CONV_DIM), F32),
                        pltpu.VMEM((l, LANES), F32),
                        pltpu.VMEM((LANES, C_INNER), F32), pltpu.VMEM((LANES, C_INNER), F32)],
        compiler_params=_params(("parallel",), 56),
        name="ssd",
    )(xbc, dt, z, lw["conv_w"], lw["conv_b"], lw["dt_bias"], lw["a_log"], lw["d_skip"], lw["ssm_g"],
      consts["exf"], consts["exb"], consts["bd"])


def _route(logits):
    lane = lax.broadcasted_iota(jnp.int32, logits.shape, 1)
    lane_f = lane.astype(F32)
    neg_inf = -jnp.inf
    isg = (lane >= N_EXPERTS) & (lane < N_EXPERTS + N_EGROUPS)
    gl = jnp.where(isg, logits, neg_inf)
    gmax = jnp.max(gl, axis=-1, keepdims=True)
    gsel = jnp.min(jnp.where(gl == gmax, lane_f - N_EXPERTS, 1e9), axis=-1, keepdims=True)
    gw = 1.0 / jnp.sum(jnp.where(isg, jnp.exp(gl - gmax), 0.0), axis=-1, keepdims=True)
    grp = jnp.floor(lane_f * (1.0 / EXPERTS_PER_GROUP))
    el = jnp.where((lane < N_EXPERTS) & (grp == gsel), logits, neg_inf)
    l1 = jnp.max(el, axis=-1, keepdims=True)
    i1 = jnp.min(jnp.where(el == l1, lane_f, 1e9), axis=-1, keepdims=True)
    el2 = jnp.where(lane_f == i1, neg_inf, el)
    l2 = jnp.max(el2, axis=-1, keepdims=True)
    i2 = jnp.min(jnp.where(el2 == l2, lane_f, 1e9), axis=-1, keepdims=True)
    r = jnp.exp(l2 - l1)
    g1 = gw / (1.0 + r)
    g2 = g1 * r
    return jnp.where(lane == 0, i1, jnp.where(lane == 1, i2, jnp.where(lane == 2, g1,
                                                                    jnp.where(lane == 3, g2, 0.0))))


def _outproj_kernel(oa_ref, ob_ref, oc_ref, x_ref, mod_ref, wout_ref, goa_ref, gob_ref,
                    l1g_ref, l1b_ref, wrc_ref, br_ref, x1_o, h2v_o, route_o, ct):
    hm = x_ref.shape[1] // 2
    wa = A_HEADS * HEAD_DIM
    wb = wa + B_HEADS * MLA_V
    m = mod_ref[0]

    def mix(rows):
        oa = _rms_rows(oa_ref[0, rows, :], goa_ref[...]).astype(BF16)
        ob = _rms_rows(ob_ref[0, rows, :], gob_ref[...]).astype(BF16)
        oc = oc_ref[0, rows, :].astype(BF16)
        return _dot(oa, wout_ref[0:wa, :]) + _dot(ob, wout_ref[wa:wb, :]) + _dot(oc, wout_ref[wb:, :])

    halves = [pl.ds(0, hm), pl.ds(hm, hm)]
    ys = [mix(rows) for rows in halves]
    h2s = []
    for rows, y in zip(halves, ys):
        x1 = _ln_rows(DN_ALPHA * x_ref[0, rows, :] + (1.0 + m[2:3]) * y, l1g_ref[...], l1b_ref[...])
        x1_o[0, rows, :] = x1
        h2 = x1 * (1.0 + m[4:5]) + m[3:4]
        h2s.append(h2)
        hi, lo = _split_bf16(h2)
        hw = _dot(hi, wrc_ref[...])
        logits = hw[:, :LANES] + hw[:, LANES:] + _dot(lo, wrc_ref[:, :LANES]) + br_ref[...]
        route_o[0, rows, :] = _route(logits)
    _rows_to_tiles(jnp.concatenate(h2s, axis=0), ct, h2v_o)


def _outproj(oa, ob, oc, x, mod, lw, tm):
    s, l, d = x.shape
    nt = l // tm
    nch = d // LANES
    row = lambda w: pl.BlockSpec((1, tm, w), lambda si, i: (si, i, 0))
    full = lambda shape: pl.BlockSpec(shape, lambda si, i: (0,) * len(shape))
    return pl.pallas_call(
        _outproj_kernel,
        out_shape=(jax.ShapeDtypeStruct((s, l, d), F32), jax.ShapeDtypeStruct((s * l * nch, LANES), F32),
                   jax.ShapeDtypeStruct((s, l, LANES), F32)),
        grid=(s, l // tm),
        in_specs=[row(A_HEADS * HEAD_DIM), row(B_HEADS * MLA_V), row(C_INNER), row(d),
                  pl.BlockSpec((1, 6, d), lambda si, i: (si, 0, 0)),
                  full((d, d)), full((1, A_HEADS * HEAD_DIM)), full((1, B_HEADS * MLA_V)),
                  full((1, d)), full((1, d)), full((d, 2 * LANES)), full((1, LANES))],
        out_specs=(row(d), pl.BlockSpec((tm * nch, LANES), lambda si, i: (si * nt + i, 0)), row(LANES)),
        scratch_shapes=[pltpu.VMEM((nch * _pitch(tm), LANES), F32)],
        compiler_params=_params(("parallel", "parallel"), 48),
        name="outproj",
    )(oa, ob, oc, x, mod, lw["w_out"], lw["goa"], lw["gob"], lw["ln1_g"], lw["ln1_b"],
      lw["w_r"], lw["b_r"])


MOE_ROWS = 144
ROW_STRIDE = _pitch(MOE_ROWS)


def _slot_rows(l):
    return 2 * l + N_EXPERTS * 8 + MOE_ROWS


def _route_sort_kernel(route_ref, slots_o, meta_o):
    r = route_ref[0]
    l = r.shape[0]
    lane = lax.broadcasted_iota(jnp.int32, (l, LANES), 1)
    lane_f = lane.astype(F32)
    oh1 = lane_f == jnp.broadcast_to(r[:, 0:1], (l, LANES))
    oh2 = lane_f == jnp.broadcast_to(r[:, 1:2], (l, LANES))
    member = jnp.where(oh1 | oh2, 1.0, 0.0).astype(BF16)
    tb = min(256, l)
    ri = lax.broadcasted_iota(jnp.int32, (tb, tb), 0)
    ci = lax.broadcasted_iota(jnp.int32, (tb, tb), 1)
    stril = jnp.where(ri > ci, 1.0, 0.0).astype(BF16)
    run = jnp.zeros((1, LANES), F32)
    ranks = []
    for i in range(l // tb):
        blk = member[i * tb:(i + 1) * tb]
        ranks.append(_dot(stril, blk) + run)
        run = run + jnp.sum(blk.astype(F32), axis=0, keepdims=True)
    rank_all = jnp.concatenate(ranks, axis=0)
    padded = jnp.floor((run + 7.0) * 0.125) * 8.0
    hi, lo = _split_bf16(jnp.broadcast_to(padded, (8, LANES)))
    rl = lax.broadcasted_iota(jnp.int32, (LANES, LANES), 0)
    cl = lax.broadcasted_iota(jnp.int32, (LANES, LANES), 1)
    upper = jnp.where(rl < cl, 1.0, 0.0).astype(BF16)
    base = (_dot(hi, upper) + _dot(lo, upper))[0:1]
    pos = base + rank_all
    slot1 = jnp.sum(jnp.where(oh1, pos, 0.0), axis=-1, keepdims=True)
    slot2 = jnp.sum(jnp.where(oh2, pos, 0.0), axis=-1, keepdims=True)
    packed = jnp.where(lane == 0, slot1, jnp.where(lane == 1, slot2, 0.0))
    slots_o[0] = packed.T[0:2].astype(jnp.int32)
    meta_o[0] = jnp.concatenate([base, run, jnp.zeros((6, LANES), F32)], axis=0).astype(jnp.int32)


def _route_sort(route):
    s, l, _ = route.shape
    return pl.pallas_call(
        _route_sort_kernel,
        out_shape=(jax.ShapeDtypeStruct((s, 2, l), jnp.int32), jax.ShapeDtypeStruct((s, 8, LANES), jnp.int32)),
        grid=(s,),
        in_specs=[pl.BlockSpec((1, l, LANES), lambda si: (si, 0, 0))],
        out_specs=(pl.BlockSpec((1, 2, l), lambda si: (si, 0, 0)), pl.BlockSpec((1, 8, LANES), lambda si: (si, 0, 0))),
        compiler_params=_params(("parallel",), 32),
        name="route_sort",
    )(route)


MOE_EXPERTS_PER_STEP = 2
MOE_EXPERT_STEPS = N_EXPERTS // MOE_EXPERTS_PER_STEP


def _moe_kernel(meta_ref, h2v_ref, slots_hbm, wg_ref, wu_ref, wd_ref,
                x1_ref, route_ref, mod_ref, l2g_ref, l2b_ref, o_ref,
                slot_s, tok_s, y_v, xt, yt, ct1, ct2, sem):
    s = pl.program_id(0)
    e = pl.program_id(1)
    l = h2v_ref.shape[0] // 8
    nch = D_MODEL // LANES
    n_slots = tok_s.shape[0]
    spare = 8 * 2 * l

    @pl.when(e == 0)
    def _():
        c_slots = pltpu.make_async_copy(slots_hbm.at[s], slot_s, sem.at[0])
        c_slots.start()
        used = meta_ref[s, 0, N_EXPERTS]

        def clear_tail(g, c):
            for j in range(8):
                tok_s[used + g * 8 + j] = spare
            return c

        lax.fori_loop(0, (n_slots - used) // 8, clear_tail, 0)

        def clear(ex, c):
            n = meta_ref[s, 1, ex]
            last = meta_ref[s, 0, ex] + ((n + 7) // 8) * 8 - 8

            @pl.when(n > 0)
            def _():
                for j in range(8):
                    tok_s[last + j] = spare

            return c

        lax.fori_loop(0, N_EXPERTS, clear, 0)
        c_slots.wait()

        def invert(g, c):
            t0 = g * 8
            for k in range(2):
                for j in range(8):
                    tok_s[slot_s[k * l + t0 + j]] = 16 * t0 + (16 * j + 8 * k)
            return c

        lax.fori_loop(0, l // 8, invert, 0)

    grp_per_blk = MOE_ROWS // 8

    def ffn(sub):
        xb = jnp.concatenate([xt[sub, j * ROW_STRIDE:j * ROW_STRIDE + MOE_ROWS, :] for j in range(nch)],
                             axis=-1).astype(BF16)
        mid = (_silu(_dot(xb, wg_ref[sub])) * _dot(xb, wu_ref[sub])).astype(BF16)
        y = _dot(mid, wd_ref[sub])
        for j in range(nch):
            yt[sub, j * ROW_STRIDE:j * ROW_STRIDE + MOE_ROWS, :] = y[:, j * LANES:(j + 1) * LANES]

    def gather_row(sub, slot, i):
        src = pl.multiple_of(lax.shift_right_logical(tok_s[slot], 1) & (8 * l - 8), 8)
        xt[sub, pl.ds(i, nch, stride=ROW_STRIDE), :] = h2v_ref[pl.ds(src, 8), :]

    def scatter_row(sub, slot, i):
        y_v[pl.ds(pl.multiple_of(tok_s[slot], 8), 8), :] = yt[sub, pl.ds(i, nch, stride=ROW_STRIDE), :]

    @pl.when(e < MOE_EXPERT_STEPS)
    def _():
        info = []
        for sub in range(MOE_EXPERTS_PER_STEP):
            ex = e * MOE_EXPERTS_PER_STEP + sub
            info.append((meta_ref[s, 0, ex], meta_ref[s, 1, ex]))
        for sub, (base, n) in enumerate(info):
            for i in range(MOE_ROWS):
                gather_row(sub, base + i, i)
        for sub in range(MOE_EXPERTS_PER_STEP):
            ffn(sub)
        for sub, (base, n) in enumerate(info):
            for i in range(MOE_ROWS):
                scatter_row(sub, base + i, i)
        for sub, (base, n) in enumerate(info):
            ngrp = (n + 7) // 8

            def block(blk, c, sub=sub, base=base, ngrp=ngrp):
                g_lo = blk * grp_per_blk
                ng = jnp.minimum(grp_per_blk, ngrp - g_lo)

                def gather(g, cc):
                    for j in range(8):
                        gather_row(sub, base + (g_lo + g) * 8 + j, g * 8 + j)
                    return cc

                lax.fori_loop(0, ng, gather, 0)
                ffn(sub)

                def scatter(g, cc):
                    for j in range(8):
                        scatter_row(sub, base + (g_lo + g) * 8 + j, g * 8 + j)
                    return cc

                lax.fori_loop(0, ng, scatter, 0)
                return c

            lax.fori_loop(1, (ngrp + grp_per_blk - 1) // grp_per_blk, block, 0)

    @pl.when(e >= MOE_EXPERT_STEPS)
    def _():
        tmo = x1_ref.shape[1]
        t0 = (e - MOE_EXPERT_STEPS) * tmo
        r = route_ref[0]
        y = (r[:, 2:3] * _tiles_to_rows(y_v, t0, tmo, ct1, 2, 0)
             + r[:, 3:4] * _tiles_to_rows(y_v, t0, tmo, ct2, 2, 1))
        m = mod_ref[0]
        o_ref[0] = _ln_rows(DN_ALPHA * x1_ref[0] + (1.0 + m[5:6]) * y, l2g_ref[...], l2b_ref[...])


def _moe(h2v, slots, meta, x1, route, mod, lw, tm):
    s, l, d = x1.shape
    assert l & (l - 1) == 0, "the gather address mask needs a power-of-two sequence length"
    n_slots = _slot_rows(l)
    eps = MOE_EXPERTS_PER_STEP
    wspec = lambda shape: pl.BlockSpec(shape, lambda si, e: (jnp.minimum(e, MOE_EXPERT_STEPS - 1), 0, 0))
    blk_idx = lambda si, e: (si, jnp.maximum(e - MOE_EXPERT_STEPS, 0), 0)
    tok_blk = pl.BlockSpec((1, tm, d), blk_idx)
    vec = pl.BlockSpec((1, d), lambda si, e: (0, 0))
    return pl.pallas_call(
        _moe_kernel,
        out_shape=jax.ShapeDtypeStruct((s, l, d), F32),
        grid=(s, MOE_EXPERT_STEPS + l // tm),
        in_specs=[pl.BlockSpec(memory_space=pltpu.SMEM),
                  pl.BlockSpec((l * 8, LANES), lambda si, e: (si, 0)),
                  pl.BlockSpec(memory_space=pl.ANY),
                  wspec((eps, d, D_FF_EXPERT)), wspec((eps, d, D_FF_EXPERT)), wspec((eps, D_FF_EXPERT, d)),
                  tok_blk, pl.BlockSpec((1, tm, LANES), blk_idx),
                  pl.BlockSpec((1, 6, d), lambda si, e: (si, 0, 0)), vec, vec],
        out_specs=tok_blk,
        scratch_shapes=[pltpu.SMEM((2 * l,), jnp.int32), pltpu.SMEM((n_slots,), jnp.int32),
                        pltpu.VMEM(((2 * l + 1) * 8, LANES), F32),
                        pltpu.VMEM((eps, 8 * ROW_STRIDE, LANES), F32), pltpu.VMEM((eps, 8 * ROW_STRIDE, LANES), F32),
                        pltpu.VMEM((8 * _pitch(tm), LANES), F32), pltpu.VMEM((8 * _pitch(tm), LANES), F32),
                        pltpu.SemaphoreType.DMA((1,))],
        compiler_params=_params(("parallel", "arbitrary"), 60),
        name="moe",
    )(meta, h2v, slots, lw["w_e_gate"], lw["w_e_up"], lw["w_e_down"], x1, route, mod,
      lw["ln2_g"], lw["ln2_b"])


def _rope_tables(seq_len):
    t = jnp.arange(seq_len, dtype=jnp.int32)
    row = (t // GRID_W).astype(F32)[:, None]
    col = (t % GRID_W).astype(F32)[:, None]

    def axis_tables(d_axis):
        inv = ROPE_THETA ** (-jnp.arange(0, d_axis, 2, dtype=F32) / d_axis)
        zero = jnp.zeros((seq_len, d_axis // 2), F32)
        cs, ps, ms = [], [], []
        for pos in (row, col):
            ang = pos * inv
            c, sn = jnp.cos(ang), jnp.sin(ang)
            cs += [c, c]
            ps += [zero, sn]
            ms += [-sn, zero]
        return [jnp.concatenate(v, axis=-1) for v in (cs, ps, ms)]

    ta = [jnp.tile(v, (1, 2)) for v in axis_tables(HEAD_DIM // 2)]
    one = jnp.ones((seq_len, MLA_NOPE), F32)
    zero = jnp.zeros((seq_len, MLA_NOPE), F32)
    pad1 = jnp.ones((seq_len, LANES - MLA_NOPE - MLA_ROPE), F32)
    pad0 = jnp.zeros((seq_len, LANES - MLA_NOPE - MLA_ROPE), F32)
    cb, pb, mb = axis_tables(MLA_ROPE // 2)
    tb = [jnp.concatenate([one, cb, pad1], -1), jnp.concatenate([zero, pb, pad0], -1),
          jnp.concatenate([zero, mb, pad0], -1)]
    return ta + tb


def _ssd_consts():
    r = np.arange(LANES)[:, None]
    c = np.arange(C_INNER)[None, :]
    exf = ((r < C_HEADS) & (c // C_HEAD_DIM == r)).astype(np.float32)
    exb = ((r >= C_HEADS) & (r < 2 * C_HEADS) & (c // C_HEAD_DIM == r - C_HEADS)).astype(np.float32)
    heads_per_group = C_HEADS // SSM_GROUPS
    bd = (r // D_STATE == c // (C_HEAD_DIM * heads_per_group)).astype(np.float32)
    return {"exf": jnp.asarray(exf, BF16), "exb": jnp.asarray(exb, BF16), "bd": jnp.asarray(bd, F32)}


def _pack_weights(p):
    w_in = p["w_in"]
    depth = w_in.shape[0]
    off = np.cumsum((0, 256, 128, 128, Q_LORA, KV_LORA, MLA_ROPE, C_INNER, CONV_DIM, 2 * C_HEADS))
    w_kr = w_in[:, :, off[5]:off[6]]
    zpad = lambda n: jnp.zeros((depth, D_MODEL, n), F32)
    dtkr = jnp.concatenate([w_in[:, :, off[8]:], zpad(KR_LANE - 2 * C_HEADS), w_kr,
                            zpad(LANES - KR_LANE - MLA_ROPE)], -1)
    w_in_p = jnp.concatenate([w_in[:, :, :off[5]], w_in[:, :, off[6]:off[8]], dtkr], -1).astype(BF16)
    dq = MLA_NOPE + MLA_ROPE
    w_uq = p["w_uq"].reshape(depth, Q_LORA, B_HEADS, dq)
    w_uq_p = jnp.concatenate([w_uq, jnp.zeros((depth, Q_LORA, B_HEADS, LANES - dq), F32)], -1)
    w_ukv = p["w_ukv"].reshape(depth, KV_LORA, B_HEADS, MLA_NOPE + MLA_V)
    w_ukk = jnp.concatenate([w_ukv[..., :MLA_NOPE],
                             jnp.zeros((depth, KV_LORA, B_HEADS, LANES - MLA_NOPE), F32)], -1)
    pad_lanes = lambda v: jnp.concatenate([v, jnp.zeros(v.shape[:-1] + (LANES - v.shape[-1],), F32)], -1)
    w_r = pad_lanes(jnp.concatenate([p["w_re"], p["w_rg"]], -1))
    w_r_hi = w_r.astype(BF16)
    return {
        "w_ada": p["w_ada"].astype(BF16), "b_ada": p["b_ada"],
        "w_in": w_in_p,
        "w_uq": w_uq_p.reshape(depth, Q_LORA, B_HEADS * LANES).astype(BF16),
        "w_ukk": w_ukk.reshape(depth, KV_LORA, B_HEADS * LANES).astype(BF16),
        "w_ukv": w_ukv[..., MLA_NOPE:].reshape(depth, KV_LORA, B_HEADS * MLA_V).astype(BF16),
        "gqa": jnp.tile(p["qa_norm_g"], (1, 2))[:, None, :], "gka": jnp.tile(p["ka_norm_g"], (1, 2))[:, None, :],
        "gqb": p["qb_norm_g"][:, None, :], "gkvb": p["kvb_norm_g"][:, None, :],
        "goa": p["oa_norm_g"][:, None, :], "gob": p["ob_norm_g"][:, None, :],
        "conv_w": p["conv_w"], "conv_b": p["conv_b"][:, None, :],
        "dt_bias": pad_lanes(p["dt_bias"].reshape(depth, 2 * C_HEADS))[:, None, :],
        "a_log": pad_lanes(p["a_log"].reshape(depth, 2 * C_HEADS))[:, None, :],
        "d_skip": jnp.repeat(p["d_skip"], C_HEAD_DIM, axis=-1)[:, None, :],
        "ssm_g": p["ssm_norm_g"][:, None, :],
        "w_out": p["w_out"].astype(BF16),
        "ln1_g": p["ln1_g"][:, None, :], "ln1_b": p["ln1_b"][:, None, :],
        "w_r": jnp.concatenate([w_r_hi, (w_r - w_r_hi.astype(F32)).astype(BF16)], -1),
        "b_r": pad_lanes(jnp.concatenate([p["b_re"], p["b_rg"]], -1))[:, None, :],
        "w_e_gate": p["w_e_gate"].astype(BF16), "w_e_up": p["w_e_up"].astype(BF16),
        "w_e_down": p["w_e_down"].astype(BF16),
        "ln2_g": p["ln2_g"][:, None, :], "ln2_b": p["ln2_b"][:, None, :],
    }


def _trunk(x, c, p, pw=None, tm=512, tq=1024):
    s, l, d = x.shape
    tm = min(tm, l)
    tq = min(tq, l)
    pw = _pack_weights(p) if pw is None else pw
    tabs = _rope_tables(l)
    consts = _ssd_consts()
    mods = _ada(c, pw["w_ada"], pw["b_ada"]).reshape(DEPTH, s, 6, d)
    for layer in range(DEPTH):
        lw = {k: v[layer] for k, v in pw.items() if k not in ("w_ada", "b_ada")}
        mod = mods[layer]
        if layer == 0:
            *proj, x = _inproj(x, mod, lw, tabs, tm, ln0=(p["ln0_g"][None, :], p["ln0_b"][None, :]))
        else:
            proj = _inproj(x, mod, lw, tabs, tm)
        qa, ka, va, qb, kb, vb, z, xbc, dt = proj
        oa = _attention(qa, ka, va, tq, "attn_a")
        ob = _attention(qb, kb, vb, tq, "attn_b")
        oc = _ssd(xbc, dt, z, lw, consts)
        x1, h2v, route = _outproj(oa, ob, oc, x, mod, lw, tm)
        slots, meta = _route_sort(route)
        x = _moe(h2v, slots.reshape(s, 2 * l), meta, x1, route, mod, lw, tm)
    return x


def kernel(x_prompt, x_sample, c_prompt, c_sample, ln0_g, ln0_b, w_ada, b_ada, w_in, qa_norm_g, ka_norm_g, oa_norm_g, qb_norm_g, w_uq, kvb_norm_g, w_ukv, ob_norm_g, conv_w, conv_b, dt_bias, a_log, d_skip, ssm_norm_g, w_out, ln1_g, ln1_b, w_rg, b_rg, w_re, b_re, w_e_gate, w_e_up, w_e_down, ln2_g, ln2_b):
    p = dict(ln0_g=ln0_g, ln0_b=ln0_b, w_ada=w_ada, b_ada=b_ada, w_in=w_in,
             qa_norm_g=qa_norm_g, ka_norm_g=ka_norm_g, oa_norm_g=oa_norm_g,
             qb_norm_g=qb_norm_g, w_uq=w_uq, kvb_norm_g=kvb_norm_g, w_ukv=w_ukv, ob_norm_g=ob_norm_g,
             conv_w=conv_w, conv_b=conv_b, dt_bias=dt_bias, a_log=a_log, d_skip=d_skip,
             ssm_norm_g=ssm_norm_g, w_out=w_out, ln1_g=ln1_g, ln1_b=ln1_b,
             w_rg=w_rg, b_rg=b_rg, w_re=w_re, b_re=b_re,
             w_e_gate=w_e_gate, w_e_up=w_e_up, w_e_down=w_e_down, ln2_g=ln2_g, ln2_b=ln2_b)
    pw = _pack_weights(p)
    return (_trunk(x_prompt, c_prompt, p, pw), _trunk(x_sample, c_sample, p, pw))
```

```python
import functools

import jax
import jax.numpy as jnp
import numpy as np
from jax import lax
from jax.experimental import pallas as pl
from jax.experimental.pallas import tpu as pltpu

F32 = jnp.float32
BF16 = jnp.bfloat16

D_MODEL = 1024
DEPTH = 4
GRID_W = 64
ROPE_THETA = 10000.0
HEAD_DIM = 64
A_HEADS = 4
A_KV_HEADS = 2
B_HEADS = 4
Q_LORA = 384
KV_LORA = 128
MLA_NOPE = 64
MLA_ROPE = 32
MLA_V = 64
C_HEADS = 8
C_HEAD_DIM = 64
SSM_GROUPS = 2
D_STATE = 64
D_CONV = 5
C_INNER = C_HEADS * C_HEAD_DIM
CONV_DIM = C_INNER + 2 * SSM_GROUPS * D_STATE
N_EGROUPS = 4
EXPERTS_PER_GROUP = 8
N_EXPERTS = N_EGROUPS * EXPERTS_PER_GROUP
D_FF_EXPERT = 256
DN_ALPHA = (2 * DEPTH) ** 0.25
EPS = 1e-6

LANES = 128
SUBLANES = 8
SSD_Q = 128
NEG_BIG = -1e30
ATT_DV = 64
ATT_Q_ROWS = 256
MOE_ROWS = 144
MOE_EXPERTS_PER_STEP = 2
MOE_EXPERT_STEPS = N_EXPERTS // MOE_EXPERTS_PER_STEP

P_QA, P_KA, P_VA, P_CQ, P_CKV, P_Z, P_XBC, P_DTKR, P_END = (
    0, 256, 384, 512, 896, 1024, 1536, 2304, 2432)
KR_LANE = MLA_NOPE


def _params(sem, vmem_mb):
    return pltpu.CompilerParams(dimension_semantics=sem, vmem_limit_bytes=vmem_mb * 1024 * 1024)


def _silu(x):
    return x * (1.0 / (1.0 + jnp.exp(-x)))


def _ln_rows(x, g, b):
    mu = jnp.mean(x, axis=-1, keepdims=True)
    xc = x - mu
    var = jnp.mean(xc * xc, axis=-1, keepdims=True)
    return xc * lax.rsqrt(var + EPS) * g + b


def _rms_rows(x, g):
    return x * lax.rsqrt(jnp.mean(x * x, axis=-1, keepdims=True) + EPS) * g


def _split_bf16(x):
    hi = x.astype(BF16)
    lo = (x - hi.astype(F32)).astype(BF16)
    return hi, lo


def _dot(a, b):
    return jnp.dot(a, b, preferred_element_type=F32)


def _dot_nt(a, b):
    return lax.dot_general(a, b, (((1,), (1,)), ((), ())), preferred_element_type=F32)


def _pitch(rows):
    return rows + SUBLANES


def _rows_to_tiles(x, ct, out_ref):
    m = x.shape[0]
    nch = x.shape[1] // LANES
    p = _pitch(m)
    for j in range(nch):
        ct[j * p:j * p + m, :] = x[:, j * LANES:(j + 1) * LANES]

    def grp(g, c):
        for j in range(8):
            r = g * 8 + j
            out_ref[pl.ds(pl.multiple_of(r * 8, 8), 8), :] = ct[pl.ds(r, nch, stride=p), :]
        return c

    lax.fori_loop(0, m // 8, grp, 0)


def _tiles_to_rows(src_ref, row0, m, ct, step=1, offset=0):
    nch = D_MODEL // LANES
    p = _pitch(m)

    def grp(g, c):
        for j in range(8):
            r = g * 8 + j
            src = pl.multiple_of(((row0 + r) * step + offset) * 8, 8)
            ct[pl.ds(r, nch, stride=p), :] = src_ref[pl.ds(src, 8), :]
        return c

    lax.fori_loop(0, m // 8, grp, 0)
    return jnp.concatenate([ct[j * p:j * p + m, :] for j in range(nch)], axis=-1)


def _ada_kernel(c_ref, w_ref, b_ref, o_ref):
    cs = _silu(c_ref[...]).astype(BF16)
    o_ref[0] = _dot(cs, w_ref[0]) + b_ref[0]


def _ada(c, w_bf, b, tn=2048):
    s, d = c.shape
    depth, _, n = w_bf.shape
    return pl.pallas_call(
        _ada_kernel,
        out_shape=jax.ShapeDtypeStruct((depth, s, n), F32),
        grid=(depth, n // tn),
        in_specs=[pl.BlockSpec((s, d), lambda l, j: (0, 0)),
                  pl.BlockSpec((1, d, tn), lambda l, j: (l, 0, j)),
                  pl.BlockSpec((1, 1, tn), lambda l, j: (l, 0, j))],
        out_specs=pl.BlockSpec((1, s, tn), lambda l, j: (l, 0, j)),
        compiler_params=_params(("parallel", "parallel"), 32),
        name="ada",
    )(c, w_bf, b.reshape(depth, 1, n))


def _rope(x, c, p, m, shift):
    return x * c + pltpu.roll(x, shift, 1) * p + pltpu.roll(x, LANES - shift, 1) * m


def _headnorm2(x, g):
    xx = x * x
    lo = lax.broadcasted_iota(jnp.int32, x.shape, 1) < HEAD_DIM
    s0 = jnp.sum(jnp.where(lo, xx, 0.0), axis=-1, keepdims=True)
    s1 = jnp.sum(jnp.where(lo, 0.0, xx), axis=-1, keepdims=True)
    inv = jnp.where(lo, lax.rsqrt(s0 * (1.0 / HEAD_DIM) + EPS), lax.rsqrt(s1 * (1.0 / HEAD_DIM) + EPS))
    return x * inv * g


def _store_values(v2, v_o, h0):
    lo = lax.broadcasted_iota(jnp.int32, v2.shape, 1) < ATT_DV
    v_o[0, h0] = jnp.where(lo, v2, 1.0).astype(BF16)
    v_o[0, h0 + 1] = jnp.where(lo, pltpu.roll(v2, ATT_DV, 1), 1.0).astype(BF16)


def _inproj_kernel(x_ref, mod_ref, w_ref, wuq_ref, wukk_ref, wukv_ref,
                   gqa_ref, gka_ref, gqb_ref, gkvb_ref,
                   ca_ref, pa_ref, ma_ref, cb_ref, pb_ref, mb_ref, g0_ref, b0_ref,
                   qa_o, ka_o, va_o, qb_o, kb_o, vb_o, z_o, xbc_o, dt_o, *xln_o):
    m = mod_ref[0]
    x = x_ref[0]
    if xln_o:
        x = _ln_rows(x, g0_ref[...], b0_ref[...])
        xln_o[0][0] = x
    h = (x * (1.0 + m[1:2]) + m[0:1]).astype(BF16)

    def proj(a, b):
        return _dot(h, w_ref[:, a:b])

    ca, pa, ma = ca_ref[...], pa_ref[...], ma_ref[...]
    cb, pb, mb = cb_ref[...], pb_ref[...], mb_ref[...]
    half = HEAD_DIM // 2
    pq = proj(P_QA, P_KA)
    for ch in range(A_HEADS // 2):
        xa = _headnorm2(pq[:, ch * LANES:(ch + 1) * LANES], gqa_ref[...])
        xa = _rope(xa, ca, pa, ma, half // 2) * (HEAD_DIM ** -0.5)
        qa_o[0, 2 * ch] = xa[:, :HEAD_DIM].astype(BF16)
        qa_o[0, 2 * ch + 1] = xa[:, HEAD_DIM:].astype(BF16)
    pkv = proj(P_KA, P_CQ)
    xk = _rope(_headnorm2(pkv[:, :LANES], gka_ref[...]), ca, pa, ma, half // 2)
    ka_o[0, 0] = xk[:, :HEAD_DIM].astype(BF16)
    ka_o[0, 1] = xk[:, HEAD_DIM:].astype(BF16)
    _store_values(pkv[:, LANES:], va_o, 0)
    pc = proj(P_CQ, P_Z)
    cqn = _rms_rows(pc[:, :Q_LORA], gqb_ref[...]).astype(BF16)
    qb = _dot(cqn, wuq_ref[...])
    ckvn = _rms_rows(pc[:, Q_LORA:], gkvb_ref[...]).astype(BF16)
    kn = _dot(ckvn, wukk_ref[...])
    vv = _dot(ckvn, wukv_ref[...])
    for pr in range(B_HEADS // 2):
        _store_values(vv[:, pr * LANES:(pr + 1) * LANES], vb_o, 2 * pr)
    px = proj(P_XBC, P_END)
    dtkr = px[:, CONV_DIM:]
    lane = lax.broadcasted_iota(jnp.int32, dtkr.shape, 1)
    is_kr = (lane >= KR_LANE) & (lane < KR_LANE + MLA_ROPE)
    k_rope = _rope(jnp.where(is_kr, dtkr, 0.0), cb, pb, mb, MLA_ROPE // 4)
    scale_b = (MLA_NOPE + MLA_ROPE) ** -0.5
    for hh in range(B_HEADS):
        sl = slice(hh * LANES, (hh + 1) * LANES)
        qb_o[0, hh] = (_rope(qb[:, sl], cb, pb, mb, MLA_ROPE // 4) * scale_b).astype(BF16)
        kb_o[0, hh] = (kn[:, sl] + k_rope).astype(BF16)
    z_o[0] = proj(P_Z, P_XBC)
    xbc_o[0] = px[:, :CONV_DIM]
    dt_o[0] = jnp.where(lane < 2 * C_HEADS, dtkr, 0.0)


def _inproj(x, mod, lw, tabs, tm, ln0=None):
    s, l, d = x.shape
    nt = l // tm
    full = lambda shape: pl.BlockSpec(shape, lambda si, i: (0,) * len(shape))
    tab = pl.BlockSpec((tm, LANES), lambda si, i: (i, 0))
    head = lambda nh, dh: pl.BlockSpec((1, nh, tm, dh), lambda si, i: (si, 0, i, 0))
    row = lambda w: pl.BlockSpec((1, tm, w), lambda si, i: (si, i, 0))
    out_shape = (
        jax.ShapeDtypeStruct((s, A_HEADS, l, HEAD_DIM), BF16),
        jax.ShapeDtypeStruct((s, A_KV_HEADS, l, HEAD_DIM), BF16),
        jax.ShapeDtypeStruct((s, A_KV_HEADS, l, LANES), BF16),
        jax.ShapeDtypeStruct((s, B_HEADS, l, LANES), BF16),
        jax.ShapeDtypeStruct((s, B_HEADS, l, LANES), BF16),
        jax.ShapeDtypeStruct((s, B_HEADS, l, LANES), BF16),
        jax.ShapeDtypeStruct((s, l, C_INNER), F32),
        jax.ShapeDtypeStruct((s, l, CONV_DIM), F32),
        jax.ShapeDtypeStruct((s, l, LANES), F32),
    )
    out_specs = (head(A_HEADS, HEAD_DIM), head(A_KV_HEADS, HEAD_DIM), head(A_KV_HEADS, LANES),
                 head(B_HEADS, LANES), head(B_HEADS, LANES), head(B_HEADS, LANES),
                 row(C_INNER), row(CONV_DIM), row(LANES))
    if ln0 is None:
        ln0 = (jnp.ones((1, d), F32), jnp.zeros((1, d), F32))
    else:
        out_shape += (jax.ShapeDtypeStruct((s, l, d), F32),)
        out_specs += (row(d),)
    return pl.pallas_call(
        _inproj_kernel,
        out_shape=out_shape,
        grid=(s, nt),
        in_specs=[row(d),
                  pl.BlockSpec((1, 6, d), lambda si, i: (si, 0, 0)),
                  full((d, P_END)), full((Q_LORA, B_HEADS * LANES)),
                  full((KV_LORA, B_HEADS * LANES)), full((KV_LORA, B_HEADS * MLA_V)),
                  full((1, LANES)), full((1, LANES)), full((1, Q_LORA)), full((1, KV_LORA)),
                  tab, tab, tab, tab, tab, tab, full((1, d)), full((1, d))],
        out_specs=out_specs,
        compiler_params=_params(("parallel", "parallel"), 48),
        name="inproj",
    )(x, mod, lw["w_in"], lw["w_uq"], lw["w_ukk"], lw["w_ukv"],
      lw["gqa"], lw["gka"], lw["gqb"], lw["gkvb"], *tabs, *ln0)


def _attn_kernel(q_ref, k_ref, v_ref, o_ref, *, shared_kv):
    tq = q_ref.shape[2]
    qr = min(ATT_Q_ROWS, tq)
    items = [(j, r) for j in range(2) for r in range(tq // qr)]

    def scores(j, r):
        kj = 0 if shared_kv else j
        return _dot_nt(q_ref[0, j, r * qr:(r + 1) * qr, :], k_ref[0, kj])

    s_next = scores(*items[0])
    outs = {}
    for i, (j, r) in enumerate(items):
        s = s_next
        if i + 1 < len(items):
            s_next = scores(*items[i + 1])
        kj = 0 if shared_kv else j
        p = jnp.exp(s - jnp.max(s, axis=-1, keepdims=True)).astype(BF16)
        o = _dot(p, v_ref[0, kj])
        outs[(j, r)] = o[:, :ATT_DV] / o[:, ATT_DV:ATT_DV + 1]
    for r in range(tq // qr):
        o_ref[0, r * qr:(r + 1) * qr, :] = jnp.concatenate([outs[(0, r)], outs[(1, r)]], axis=-1)


def _attention(q, k, v, tq, name):
    s, hq, l, dk = q.shape
    hk = k.shape[1]
    shared = hq // hk == 2
    kvb = 1 if shared else 2
    return pl.pallas_call(
        functools.partial(_attn_kernel, shared_kv=shared),
        out_shape=jax.ShapeDtypeStruct((s, l, hq * ATT_DV), F32),
        grid=(s, hq // 2, l // tq),
        in_specs=[pl.BlockSpec((1, 2, tq, dk), lambda si, hp, qi: (si, hp, qi, 0)),
                  pl.BlockSpec((1, kvb, l, dk), lambda si, hp, qi: (si, hp, 0, 0)),
                  pl.BlockSpec((1, kvb, l, LANES), lambda si, hp, qi: (si, hp, 0, 0))],
        out_specs=pl.BlockSpec((1, tq, 2 * ATT_DV), lambda si, hp, qi: (si, qi, hp)),
        compiler_params=_params(("parallel", "parallel", "arbitrary"), 48),
        name=name,
    )(q, k, v)


def _ssd_kernel(xbc_ref, dt_ref, z_ref, cw_ref, cb_ref, dtb_ref, alog_ref, dskip_ref, g_ref,
                exf_ref, exb_ref, bd_ref, o_ref, xpad, xact, ad_s, hf, hb, pb_s, xsb_s, bt_s, dec_s):
    l = xbc_ref.shape[1]
    nc = l // SSD_Q
    q = SSD_Q
    pad = 8
    xpad[0:pad, :] = jnp.zeros((pad, CONV_DIM), F32)
    xpad[l + pad:l + 2 * pad, :] = jnp.zeros((pad, CONV_DIM), F32)
    xpad[pad:l + pad, :] = xbc_ref[0]
    def conv_chunk(c, carry):
        r0 = pl.multiple_of(c * q, q)
        for st in range(CONV_DIM // LANES):
            cols = slice(st * LANES, (st + 1) * LANES)
            win = xpad[pl.ds(r0, q + 2 * pad), cols]
            acc = jnp.zeros((q, LANES), F32) + cb_ref[:, cols]
            for k in range(D_CONV):
                lo = pad - D_CONV // 2 + k
                acc = acc + win[lo:lo + q] * cw_ref[k:k + 1, cols]
            xact[pl.ds(r0, q), cols] = _silu(acc)
        return carry

    lax.fori_loop(0, nc, conv_chunk, 0)
    dtr = dt_ref[0] + dtb_ref[...]
    dt = jnp.maximum(dtr, 0.0) + jnp.log1p(jnp.exp(-jnp.abs(dtr)))
    lane_l = lax.broadcasted_iota(jnp.int32, dt.shape, 1)
    a = dt * (-jnp.exp(alog_ref[...]))
    ad_s[...] = jnp.where(lane_l < 2 * C_HEADS, a, pltpu.roll(dt, 2 * C_HEADS, 1))

    ri = lax.broadcasted_iota(jnp.int32, (q, q), 0)
    ci = lax.broadcasted_iota(jnp.int32, (q, q), 1)
    lower = ri >= ci
    upper = ci >= ri
    tril = jnp.where(lower, 1.0, 0.0).astype(BF16)
    lane_q = lax.broadcasted_iota(jnp.int32, (q, LANES), 1)
    first8 = lane_q < C_HEADS
    mid8 = (lane_q >= C_HEADS) & (lane_q < 2 * C_HEADS)
    lo64 = lane_q < C_HEAD_DIM
    bd = bd_ref[...] > 0.5

    def chunk_decay(r0):
        ad = ad_s[pl.ds(r0, q), :]
        hi, lo = _split_bf16(ad)
        inc = _dot(tril, hi) + _dot(tril, lo)
        exc = inc - ad
        tot = inc[q - 1:q, :]
        dts = pltpu.roll(ad, LANES - 2 * C_HEADS, 1)
        return ad, inc, exc, tot, dts

    def expand(v, ex_ref):
        return _dot(v.astype(BF16), ex_ref[...])

    def expand_row(v_row, ex_ref):
        hi, lo = _split_bf16(jnp.broadcast_to(v_row, (8, LANES)))
        return (_dot(hi, ex_ref[...]) + _dot(lo, ex_ref[...]))[0:1]

    hf[...] = jnp.zeros_like(hf)
    hb[...] = jnp.zeros_like(hb)

    def fwd_pre(c):
        r0 = pl.multiple_of(c * q, q)
        ad, inc, exc, tot, dts = chunk_decay(r0)
        xq = xact[pl.ds(r0, q), :]
        xs = xq[:, :C_INNER]
        bm = xq[:, C_INNER:C_INNER + LANES]
        cm = xq[:, C_INNER + LANES:]
        bt_bf = bm.T.astype(BF16)
        gmat = []
        for g in range(SSM_GROUPS):
            cg = jnp.where(lo64 if g == 0 else jnp.logical_not(lo64), cm, 0.0).astype(BF16)
            gmat.append(_dot(cg, bt_bf))
        pf = expand(jnp.where(first8, jnp.exp(inc), 0.0), exf_ref)
        sf = expand(jnp.where(first8, jnp.exp(tot - inc) * dts, 0.0), exf_ref)
        dec = expand_row(jnp.where(first8[0:1], jnp.exp(tot), 0.0), exf_ref)
        pbk = expand(jnp.where(mid8, jnp.exp(tot - exc), 0.0), exb_ref)
        sbk = expand(jnp.where(mid8, jnp.exp(exc) * dts, 0.0), exb_ref)
        pb_s[pl.ds(r0, q), :] = pbk
        xsb_s[pl.ds(r0, q), :] = (xs * sbk).astype(BF16)
        bt_s[pl.ds(r0, q), :] = bt_bf
        dec_b = expand_row(jnp.where(mid8[0:1], jnp.exp(tot), 0.0), exb_ref)
        dec_s[pl.ds(pl.multiple_of(c * 8, 8), 8), :] = jnp.broadcast_to(dec_b, (8, C_INNER))
        return dict(r0=r0, ad=ad, inc=inc, exc=exc, xs=xs, bt_bf=bt_bf, cm_bf=cm.astype(BF16),
                    gmat=gmat, pf=pf, dec=dec, xsd=(xs * sf).astype(BF16))

    def fwd_state(v):
        y_off = _dot(v["cm_bf"], hf[...].astype(BF16)) * v["pf"]
        st = _dot(v["bt_bf"], v["xsd"])
        hf[...] = v["dec"] * hf[...] + jnp.where(bd, st, 0.0)
        return y_off

    def fwd_heads(v, y_off):
        inc, exc, ad, r0 = v["inc"], v["exc"], v["ad"], v["r0"]
        inc_t = inc.T
        ad_t = ad.T
        exc_t = inc_t - ad_t
        xs_bf = v["xs"].astype(BF16)
        for pr in range(C_HEADS // 2):
            ys = []
            for hh in (2 * pr, 2 * pr + 1):
                icol = jnp.broadcast_to(inc[:, hh:hh + 1], (q, q))
                irow = jnp.broadcast_to(inc_t[hh:hh + 1, :], (q, q))
                lf = jnp.exp(jnp.where(lower, icol - irow, NEG_BIG)) * ad_t[16 + hh:17 + hh, :]
                ecol = jnp.broadcast_to(exc[:, 8 + hh:9 + hh], (q, q))
                erow = jnp.broadcast_to(exc_t[8 + hh:9 + hh, :], (q, q))
                lb = jnp.exp(jnp.where(upper, erow - ecol, NEG_BIG)) * ad_t[24 + hh:25 + hh, :]
                w = (v["gmat"][hh // (C_HEADS // SSM_GROUPS)] * (lf + lb)).astype(BF16)
                ys.append(_dot(w, xs_bf[:, pr * LANES:(pr + 1) * LANES]))
            cols = slice(pr * LANES, (pr + 1) * LANES)
            o_ref[0, pl.ds(r0, q), cols] = jnp.where(lo64, ys[0], ys[1]) + y_off[:, cols]

    def fwd_body(i, carry):
        va = fwd_pre(2 * i)
        vb = fwd_pre(2 * i + 1)
        ya = fwd_state(va)
        yb = fwd_state(vb)
        fwd_heads(va, ya)
        fwd_heads(vb, yb)
        return carry

    lax.fori_loop(0, nc // 2, fwd_body, 0)

    def bwd_body(i, carry):
        c = nc - 1 - i
        r0 = pl.multiple_of(c * q, q)
        xs = xact[pl.ds(r0, q), :C_INNER]
        cm_bf = xact[pl.ds(r0, q), C_INNER + LANES:].astype(BF16)
        y = o_ref[0, pl.ds(r0, q), :] + _dot(cm_bf, hb[...].astype(BF16)) * pb_s[pl.ds(r0, q), :]
        st = _dot(bt_s[pl.ds(r0, q), :], xsb_s[pl.ds(r0, q), :])
        dec = dec_s[pl.ds(pl.multiple_of(c * 8, 8), 1), :]
        hb[...] = dec * hb[...] + jnp.where(bd, st, 0.0)
        y = (y + dskip_ref[...] * xs) * _silu(z_ref[0, pl.ds(r0, q), :])
        o_ref[0, pl.ds(r0, q), :] = _rms_rows(y, g_ref[...])
        return carry

    lax.fori_loop(0, nc // 2, lambda i, cr: bwd_body(2 * i + 1, bwd_body(2 * i, cr)), 0)


def _ssd(xbc, dt, z, lw, consts):
    s, l, _ = xbc.shape
    assert l % (2 * SSD_Q) == 0
    seq = lambda w: pl.BlockSpec((1, l, w), lambda si: (si, 0, 0))
    full = lambda shape: pl.BlockSpec(shape, lambda si: (0,) * len(shape))
    return pl.pallas_call(
        _ssd_kernel,
        out_shape=jax.ShapeDtypeStruct((s, l, C_INNER), F32),
        grid=(s,),
        in_specs=[seq(CONV_DIM), seq(LANES), seq(C_INNER),
                  full((D_CONV, CONV_DIM)), full((1, CONV_DIM)), full((1, LANES)), full((1, LANES)),
                  full((1, C_INNER)), full((1, C_INNER)),
                  full((LANES, C_INNER)), full((LANES, C_INNER)), full((LANES, C_INNER))],
        out_specs=seq(C_INNER),
        scratch_shapes=[pltpu.VMEM((l + 16, CONV_DIM), F32), pltpu.VMEM((l, CONV_DIM), F32),
                        pltpu.VMEM((l, LANES), F32),
                        pltpu.VMEM((LANES, C_INNER), F32), pltpu.VMEM((LANES, C_INNER), F32),
                        pltpu.VMEM((l, C_INNER), F32), pltpu.VMEM((l, C_INNER), BF16),
                        pltpu.VMEM((l, SSD_Q), BF16), pltpu.VMEM((8 * (l // SSD_Q), C_INNER), F32)],
        compiler_params=_params(("parallel",), 58),
        name="ssd",
    )(xbc, dt, z, lw["conv_w"], lw["conv_b"], lw["dt_bias"], lw["a_log"], lw["d_skip"], lw["ssm_g"],
      consts["exf"], consts["exb"], consts["bd"])


def _route(logits):
    lane = lax.broadcasted_iota(jnp.int32, logits.shape, 1)
    lane_f = lane.astype(F32)
    neg_inf = -jnp.inf
    isg = (lane >= N_EXPERTS) & (lane < N_EXPERTS + N_EGROUPS)
    gl = jnp.where(isg, logits, neg_inf)
    gmax = jnp.max(gl, axis=-1, keepdims=True)
    gsel = jnp.min(jnp.where(gl == gmax, lane_f - N_EXPERTS, 1e9), axis=-1, keepdims=True)
    gw = 1.0 / jnp.sum(jnp.where(isg, jnp.exp(gl - gmax), 0.0), axis=-1, keepdims=True)
    grp = jnp.floor(lane_f * (1.0 / EXPERTS_PER_GROUP))
    el = jnp.where((lane < N_EXPERTS) & (grp == gsel), logits, neg_inf)
    l1 = jnp.max(el, axis=-1, keepdims=True)
    i1 = jnp.min(jnp.where(el == l1, lane_f, 1e9), axis=-1, keepdims=True)
    el2 = jnp.where(lane_f == i1, neg_inf, el)
    l2 = jnp.max(el2, axis=-1, keepdims=True)
    i2 = jnp.min(jnp.where(el2 == l2, lane_f, 1e9), axis=-1, keepdims=True)
    r = jnp.exp(l2 - l1)
    g1 = gw / (1.0 + r)
    g2 = g1 * r
    return jnp.where(lane == 0, i1, jnp.where(lane == 1, i2, jnp.where(lane == 2, g1,
                                                                    jnp.where(lane == 3, g2, 0.0))))


def _outproj_kernel(oa_ref, ob_ref, oc_ref, x_ref, mod_ref, wout_ref, goa_ref, gob_ref,
                    l1g_ref, l1b_ref, wrc_ref, br_ref, x1_o, h2v_o, route_o, ct):
    hm = x_ref.shape[1] // 2
    wa = A_HEADS * HEAD_DIM
    wb = wa + B_HEADS * MLA_V
    m = mod_ref[0]

    def mix(rows):
        oa = _rms_rows(oa_ref[0, rows, :], goa_ref[...]).astype(BF16)
        ob = _rms_rows(ob_ref[0, rows, :], gob_ref[...]).astype(BF16)
        oc = oc_ref[0, rows, :].astype(BF16)
        return _dot(oa, wout_ref[0:wa, :]) + _dot(ob, wout_ref[wa:wb, :]) + _dot(oc, wout_ref[wb:, :])

    halves = [pl.ds(0, hm), pl.ds(hm, hm)]
    ys = [mix(rows) for rows in halves]
    h2s = []
    for rows, y in zip(halves, ys):
        x1 = _ln_rows(DN_ALPHA * x_ref[0, rows, :] + (1.0 + m[2:3]) * y, l1g_ref[...], l1b_ref[...])
        x1_o[0, rows, :] = x1
        h2 = x1 * (1.0 + m[4:5]) + m[3:4]
        h2s.append(h2)
        hi, lo = _split_bf16(h2)
        hw = _dot(hi, wrc_ref[...])
        logits = hw[:, :LANES] + hw[:, LANES:] + _dot(lo, wrc_ref[:, :LANES]) + br_ref[...]
        route_o[0, rows, :] = _route(logits)
    _rows_to_tiles(jnp.concatenate(h2s, axis=0), ct, h2v_o)


def _outproj(oa, ob, oc, x, mod, lw, tm):
    s, l, d = x.shape
    nt = l // tm
    nch = d // LANES
    row = lambda w: pl.BlockSpec((1, tm, w), lambda si, i: (si, i, 0))
    full = lambda shape: pl.BlockSpec(shape, lambda si, i: (0,) * len(shape))
    return pl.pallas_call(
        _outproj_kernel,
        out_shape=(jax.ShapeDtypeStruct((s, l, d), F32), jax.ShapeDtypeStruct((s * l * nch, LANES), F32),
                   jax.ShapeDtypeStruct((s, l, LANES), F32)),
        grid=(s, l // tm),
        in_specs=[row(A_HEADS * HEAD_DIM), row(B_HEADS * MLA_V), row(C_INNER), row(d),
                  pl.BlockSpec((1, 6, d), lambda si, i: (si, 0, 0)),
                  full((d, d)), full((1, A_HEADS * HEAD_DIM)), full((1, B_HEADS * MLA_V)),
                  full((1, d)), full((1, d)), full((d, 2 * LANES)), full((1, LANES))],
        out_specs=(row(d), pl.BlockSpec((tm * nch, LANES), lambda si, i: (si * nt + i, 0)), row(LANES)),
        scratch_shapes=[pltpu.VMEM((nch * _pitch(tm), LANES), F32)],
        compiler_params=_params(("parallel", "parallel"), 48),
        name="outproj",
    )(oa, ob, oc, x, mod, lw["w_out"], lw["goa"], lw["gob"], lw["ln1_g"], lw["ln1_b"],
      lw["w_r"], lw["b_r"])


ROW_STRIDE = _pitch(MOE_ROWS)


def _slot_rows(l):
    return 2 * l + N_EXPERTS * 8 + MOE_ROWS


def _route_sort_kernel(route_ref, slots_o, meta_o):
    r = route_ref[0]
    l = r.shape[0]
    lane = lax.broadcasted_iota(jnp.int32, (l, LANES), 1)
    lane_f = lane.astype(F32)
    oh1 = lane_f == jnp.broadcast_to(r[:, 0:1], (l, LANES))
    oh2 = lane_f == jnp.broadcast_to(r[:, 1:2], (l, LANES))
    member = jnp.where(oh1 | oh2, 1.0, 0.0).astype(BF16)
    tb = min(256, l)
    ri = lax.broadcasted_iota(jnp.int32, (tb, tb), 0)
    ci = lax.broadcasted_iota(jnp.int32, (tb, tb), 1)
    stril = jnp.where(ri > ci, 1.0, 0.0).astype(BF16)
    run = jnp.zeros((1, LANES), F32)
    ranks = []
    for i in range(l // tb):
        blk = member[i * tb:(i + 1) * tb]
        ranks.append(_dot(stril, blk) + run)
        run = run + jnp.sum(blk.astype(F32), axis=0, keepdims=True)
    rank_all = jnp.concatenate(ranks, axis=0)
    padded = jnp.floor((run + 7.0) * 0.125) * 8.0
    hi, lo = _split_bf16(jnp.broadcast_to(padded, (8, LANES)))
    rl = lax.broadcasted_iota(jnp.int32, (LANES, LANES), 0)
    cl = lax.broadcasted_iota(jnp.int32, (LANES, LANES), 1)
    upper = jnp.where(rl < cl, 1.0, 0.0).astype(BF16)
    base = (_dot(hi, upper) + _dot(lo, upper))[0:1]
    pos = base + rank_all
    slot1 = jnp.sum(jnp.where(oh1, pos, 0.0), axis=-1, keepdims=True)
    slot2 = jnp.sum(jnp.where(oh2, pos, 0.0), axis=-1, keepdims=True)
    packed = jnp.where(lane == 0, slot1, jnp.where(lane == 1, slot2, 0.0))
    slots_o[0] = packed.T[0:2].astype(jnp.int32)
    meta_o[0] = jnp.concatenate([base, run, jnp.zeros((6, LANES), F32)], axis=0).astype(jnp.int32)


def _route_sort(route):
    s, l, _ = route.shape
    return pl.pallas_call(
        _route_sort_kernel,
        out_shape=(jax.ShapeDtypeStruct((s, 2, l), jnp.int32), jax.ShapeDtypeStruct((s, 8, LANES), jnp.int32)),
        grid=(s,),
        in_specs=[pl.BlockSpec((1, l, LANES), lambda si: (si, 0, 0))],
        out_specs=(pl.BlockSpec((1, 2, l), lambda si: (si, 0, 0)), pl.BlockSpec((1, 8, LANES), lambda si: (si, 0, 0))),
        compiler_params=_params(("parallel",), 32),
        name="route_sort",
    )(route)


def _moe_kernel(meta_ref, h2v_ref, slots_hbm, wg_ref, wu_ref, wd_ref,
                x1_ref, route_ref, mod_ref, l2g_ref, l2b_ref, o_ref,
                slot_s, tok_s, y_v, xt, yt, ct1, ct2, sem):
    s = pl.program_id(0)
    e = pl.program_id(1)
    l = h2v_ref.shape[0] // 8
    nch = D_MODEL // LANES
    n_slots = tok_s.shape[0]
    spare = 8 * 2 * l

    @pl.when(e == 0)
    def _():
        c_slots = pltpu.make_async_copy(slots_hbm.at[s], slot_s, sem.at[0])
        c_slots.start()
        used = meta_ref[s, 0, N_EXPERTS]

        def clear_tail(g, c):
            for j in range(8):
                tok_s[used + g * 8 + j] = spare
            return c

        lax.fori_loop(0, (n_slots - used) // 8, clear_tail, 0)

        def clear(ex, c):
            n = meta_ref[s, 1, ex]
            last = meta_ref[s, 0, ex] + ((n + 7) // 8) * 8 - 8

            @pl.when(n > 0)
            def _():
                for j in range(8):
                    tok_s[last + j] = spare

            return c

        lax.fori_loop(0, N_EXPERTS, clear, 0)
        c_slots.wait()

        def invert(g, c):
            t0 = g * 8
            for k in range(2):
                for j in range(8):
                    tok_s[slot_s[k * l + t0 + j]] = 16 * t0 + (16 * j + 8 * k)
            return c

        lax.fori_loop(0, l // 8, invert, 0)

    grp_per_blk = MOE_ROWS // 8

    def ffn(sub):
        xb = jnp.concatenate([xt[sub, j * ROW_STRIDE:j * ROW_STRIDE + MOE_ROWS, :] for j in range(nch)],
                             axis=-1).astype(BF16)
        mid = (_silu(_dot(xb, wg_ref[sub])) * _dot(xb, wu_ref[sub])).astype(BF16)
        y = _dot(mid, wd_ref[sub])
        for j in range(nch):
            yt[sub, j * ROW_STRIDE:j * ROW_STRIDE + MOE_ROWS, :] = y[:, j * LANES:(j + 1) * LANES]

    def gather_row(sub, slot, i):
        src = pl.multiple_of(lax.shift_right_logical(tok_s[slot], 1) & (8 * l - 8), 8)
        xt[sub, pl.ds(i, nch, stride=ROW_STRIDE), :] = h2v_ref[pl.ds(src, 8), :]

    def scatter_row(sub, slot, i):
        y_v[pl.ds(pl.multiple_of(tok_s[slot], 8), 8), :] = yt[sub, pl.ds(i, nch, stride=ROW_STRIDE), :]

    @pl.when(e < MOE_EXPERT_STEPS)
    def _():
        info = []
        for sub in range(MOE_EXPERTS_PER_STEP):
            ex = e * MOE_EXPERTS_PER_STEP + sub
            info.append((meta_ref[s, 0, ex], meta_ref[s, 1, ex]))
        for sub, (base, n) in enumerate(info):
            for i in range(MOE_ROWS):
                gather_row(sub, base + i, i)
        for sub in range(MOE_EXPERTS_PER_STEP):
            ffn(sub)
        for sub, (base, n) in enumerate(info):
            for i in range(MOE_ROWS):
                scatter_row(sub, base + i, i)
        for sub, (base, n) in enumerate(info):
            ngrp = (n + 7) // 8

            def block(blk, c, sub=sub, base=base, ngrp=ngrp):
                g_lo = blk * grp_per_blk
                ng = jnp.minimum(grp_per_blk, ngrp - g_lo)

                def gather(g, cc):
                    for j in range(8):
                        gather_row(sub, base + (g_lo + g) * 8 + j, g * 8 + j)
                    return cc

                lax.fori_loop(0, ng, gather, 0)
                ffn(sub)

                def scatter(g, cc):
                    for j in range(8):
                        scatter_row(sub, base + (g_lo + g) * 8 + j, g * 8 + j)
                    return cc

                lax.fori_loop(0, ng, scatter, 0)
                return c

            lax.fori_loop(1, (ngrp + grp_per_blk - 1) // grp_per_blk, block, 0)

    @pl.when(e >= MOE_EXPERT_STEPS)
    def _():
        tmo = x1_ref.shape[1]
        t0 = (e - MOE_EXPERT_STEPS) * tmo
        r = route_ref[0]
        y = (r[:, 2:3] * _tiles_to_rows(y_v, t0, tmo, ct1, 2, 0)
             + r[:, 3:4] * _tiles_to_rows(y_v, t0, tmo, ct2, 2, 1))
        m = mod_ref[0]
        o_ref[0] = _ln_rows(DN_ALPHA * x1_ref[0] + (1.0 + m[5:6]) * y, l2g_ref[...], l2b_ref[...])


def _moe(h2v, slots, meta, x1, route, mod, lw, tm):
    s, l, d = x1.shape
    assert l & (l - 1) == 0, "the gather address mask needs a power-of-two sequence length"
    n_slots = _slot_rows(l)
    eps = MOE_EXPERTS_PER_STEP
    wspec = lambda shape: pl.BlockSpec(shape, lambda si, e: (jnp.minimum(e, MOE_EXPERT_STEPS - 1), 0, 0))
    blk_idx = lambda si, e: (si, jnp.maximum(e - MOE_EXPERT_STEPS, 0), 0)
    tok_blk = pl.BlockSpec((1, tm, d), blk_idx)
    vec = pl.BlockSpec((1, d), lambda si, e: (0, 0))
    return pl.pallas_call(
        _moe_kernel,
        out_shape=jax.ShapeDtypeStruct((s, l, d), F32),
        grid=(s, MOE_EXPERT_STEPS + l // tm),
        in_specs=[pl.BlockSpec(memory_space=pltpu.SMEM),
                  pl.BlockSpec((l * 8, LANES), lambda si, e: (si, 0)),
                  pl.BlockSpec(memory_space=pl.ANY),
                  wspec((eps, d, D_FF_EXPERT)), wspec((eps, d, D_FF_EXPERT)), wspec((eps, D_FF_EXPERT, d)),
                  tok_blk, pl.BlockSpec((1, tm, LANES), blk_idx),
                  pl.BlockSpec((1, 6, d), lambda si, e: (si, 0, 0)), vec, vec],
        out_specs=tok_blk,
        scratch_shapes=[pltpu.SMEM((2 * l,), jnp.int32), pltpu.SMEM((n_slots,), jnp.int32),
                        pltpu.VMEM(((2 * l + 1) * 8, LANES), F32),
                        pltpu.VMEM((eps, 8 * ROW_STRIDE, LANES), F32), pltpu.VMEM((eps, 8 * ROW_STRIDE, LANES), F32),
                        pltpu.VMEM((8 * _pitch(tm), LANES), F32), pltpu.VMEM((8 * _pitch(tm), LANES), F32),
                        pltpu.SemaphoreType.DMA((1,))],
        compiler_params=_params(("parallel", "arbitrary"), 60),
        name="moe",
    )(meta, h2v, slots, lw["w_e_gate"], lw["w_e_up"], lw["w_e_down"], x1, route, mod,
      lw["ln2_g"], lw["ln2_b"])


def _rope_tables(seq_len):
    t = jnp.arange(seq_len, dtype=jnp.int32)
    row = (t // GRID_W).astype(F32)[:, None]
    col = (t % GRID_W).astype(F32)[:, None]

    def axis_tables(d_axis):
        inv = ROPE_THETA ** (-jnp.arange(0, d_axis, 2, dtype=F32) / d_axis)
        zero = jnp.zeros((seq_len, d_axis // 2), F32)
        cs, ps, ms = [], [], []
        for pos in (row, col):
            ang = pos * inv
            c, sn = jnp.cos(ang), jnp.sin(ang)
            cs += [c, c]
            ps += [zero, sn]
            ms += [-sn, zero]
        return [jnp.concatenate(v, axis=-1) for v in (cs, ps, ms)]

    ta = [jnp.tile(v, (1, 2)) for v in axis_tables(HEAD_DIM // 2)]
    one = jnp.ones((seq_len, MLA_NOPE), F32)
    zero = jnp.zeros((seq_len, MLA_NOPE), F32)
    pad1 = jnp.ones((seq_len, LANES - MLA_NOPE - MLA_ROPE), F32)
    pad0 = jnp.zeros((seq_len, LANES - MLA_NOPE - MLA_ROPE), F32)
    cb, pb, mb = axis_tables(MLA_ROPE // 2)
    tb = [jnp.concatenate([one, cb, pad1], -1), jnp.concatenate([zero, pb, pad0], -1),
          jnp.concatenate([zero, mb, pad0], -1)]
    return ta + tb


def _ssd_consts():
    r = np.arange(LANES)[:, None]
    c = np.arange(C_INNER)[None, :]
    exf = ((r < C_HEADS) & (c // C_HEAD_DIM == r)).astype(np.float32)
    exb = ((r >= C_HEADS) & (r < 2 * C_HEADS) & (c // C_HEAD_DIM == r - C_HEADS)).astype(np.float32)
    heads_per_group = C_HEADS // SSM_GROUPS
    bd = (r // D_STATE == c // (C_HEAD_DIM * heads_per_group)).astype(np.float32)
    return {"exf": jnp.asarray(exf, BF16), "exb": jnp.asarray(exb, BF16), "bd": jnp.asarray(bd, F32)}


def _pack_weights(p):
    w_in = p["w_in"]
    depth = w_in.shape[0]
    off = np.cumsum((0, 256, 128, 128, Q_LORA, KV_LORA, MLA_ROPE, C_INNER, CONV_DIM, 2 * C_HEADS))
    w_kr = w_in[:, :, off[5]:off[6]]
    zpad = lambda n: jnp.zeros((depth, D_MODEL, n), F32)
    dtkr = jnp.concatenate([w_in[:, :, off[8]:], zpad(KR_LANE - 2 * C_HEADS), w_kr,
                            zpad(LANES - KR_LANE - MLA_ROPE)], -1)
    w_in_p = jnp.concatenate([w_in[:, :, :off[5]], w_in[:, :, off[6]:off[8]], dtkr], -1).astype(BF16)
    dq = MLA_NOPE + MLA_ROPE
    w_uq = p["w_uq"].reshape(depth, Q_LORA, B_HEADS, dq)
    w_uq_p = jnp.concatenate([w_uq, jnp.zeros((depth, Q_LORA, B_HEADS, LANES - dq), F32)], -1)
    w_ukv = p["w_ukv"].reshape(depth, KV_LORA, B_HEADS, MLA_NOPE + MLA_V)
    w_ukk = jnp.concatenate([w_ukv[..., :MLA_NOPE],
                             jnp.zeros((depth, KV_LORA, B_HEADS, LANES - MLA_NOPE), F32)], -1)
    pad_lanes = lambda v: jnp.concatenate([v, jnp.zeros(v.shape[:-1] + (LANES - v.shape[-1],), F32)], -1)
    w_r = pad_lanes(jnp.concatenate([p["w_re"], p["w_rg"]], -1))
    w_r_hi = w_r.astype(BF16)
    return {
        "w_ada": p["w_ada"].astype(BF16), "b_ada": p["b_ada"],
        "w_in": w_in_p,
        "w_uq": w_uq_p.reshape(depth, Q_LORA, B_HEADS * LANES).astype(BF16),
        "w_ukk": w_ukk.reshape(depth, KV_LORA, B_HEADS * LANES).astype(BF16),
        "w_ukv": w_ukv[..., MLA_NOPE:].reshape(depth, KV_LORA, B_HEADS * MLA_V).astype(BF16),
        "gqa": jnp.tile(p["qa_norm_g"], (1, 2))[:, None, :], "gka": jnp.tile(p["ka_norm_g"], (1, 2))[:, None, :],
        "gqb": p["qb_norm_g"][:, None, :], "gkvb": p["kvb_norm_g"][:, None, :],
        "goa": p["oa_norm_g"][:, None, :], "gob": p["ob_norm_g"][:, None, :],
        "conv_w": p["conv_w"], "conv_b": p["conv_b"][:, None, :],
        "dt_bias": pad_lanes(p["dt_bias"].reshape(depth, 2 * C_HEADS))[:, None, :],
        "a_log": pad_lanes(p["a_log"].reshape(depth, 2 * C_HEADS))[:, None, :],
        "d_skip": jnp.repeat(p["d_skip"], C_HEAD_DIM, axis=-1)[:, None, :],
        "ssm_g": p["ssm_norm_g"][:, None, :],
        "w_out": p["w_out"].astype(BF16),
        "ln1_g": p["ln1_g"][:, None, :], "ln1_b": p["ln1_b"][:, None, :],
        "w_r": jnp.concatenate([w_r_hi, (w_r - w_r_hi.astype(F32)).astype(BF16)], -1),
        "b_r": pad_lanes(jnp.concatenate([p["b_re"], p["b_rg"]], -1))[:, None, :],
        "w_e_gate": p["w_e_gate"].astype(BF16), "w_e_up": p["w_e_up"].astype(BF16),
        "w_e_down": p["w_e_down"].astype(BF16),
        "ln2_g": p["ln2_g"][:, None, :], "ln2_b": p["ln2_b"][:, None, :],
    }


def _trunk(x, c, p, pw=None, tm=512, tq=2048):
    s, l, d = x.shape
    tm = min(tm, l)
    tq = min(tq, l)
    pw = _pack_weights(p) if pw is None else pw
    tabs = _rope_tables(l)
    consts = _ssd_consts()
    mods = _ada(c, pw["w_ada"], pw["b_ada"]).reshape(DEPTH, s, 6, d)
    for layer in range(DEPTH):
        lw = {k: v[layer] for k, v in pw.items() if k not in ("w_ada", "b_ada")}
        mod = mods[layer]
        if layer == 0:
            *proj, x = _inproj(x, mod, lw, tabs, tm, ln0=(p["ln0_g"][None, :], p["ln0_b"][None, :]))
        else:
            proj = _inproj(x, mod, lw, tabs, tm)
        qa, ka, va, qb, kb, vb, z, xbc, dt = proj
        oa = _attention(qa, ka, va, tq, "attn_a")
        ob = _attention(qb, kb, vb, tq, "attn_b")
        oc = _ssd(xbc, dt, z, lw, consts)
        x1, h2v, route = _outproj(oa, ob, oc, x, mod, lw, tm)
        slots, meta = _route_sort(route)
        x = _moe(h2v, slots.reshape(s, 2 * l), meta, x1, route, mod, lw, tm)
    return x


def kernel(x_prompt, x_sample, c_prompt, c_sample, ln0_g, ln0_b, w_ada, b_ada, w_in, qa_norm_g, ka_norm_g, oa_norm_g, qb_norm_g, w_uq, kvb_norm_g, w_ukv, ob_norm_g, conv_w, conv_b, dt_bias, a_log, d_skip, ssm_norm_g, w_out, ln1_g, ln1_b, w_rg, b_rg, w_re, b_re, w_e_gate, w_e_up, w_e_down, ln2_g, ln2_b):
    p = dict(ln0_g=ln0_g, ln0_b=ln0_b, w_ada=w_ada, b_ada=b_ada, w_in=w_in,
             qa_norm_g=qa_norm_g, ka_norm_g=ka_norm_g, oa_norm_g=oa_norm_g,
             qb_norm_g=qb_norm_g, w_uq=w_uq, kvb_norm_g=kvb_norm_g, w_ukv=w_ukv, ob_norm_g=ob_norm_g,
             conv_w=conv_w, conv_b=conv_b, dt_bias=dt_bias, a_log=a_log, d_skip=d_skip,
             ssm_norm_g=ssm_norm_g, w_out=w_out, ln1_g=ln1_g, ln1_b=ln1_b,
             w_rg=w_rg, b_rg=b_rg, w_re=w_re, b_re=b_re,
             w_e_gate=w_e_gate, w_e_up=w_e_up, w_e_down=w_e_down, ln2_g=ln2_g, ln2_b=ln2_b)
    pw = _pack_weights(p)
    return (_trunk(x_prompt, c_prompt, p, pw), _trunk(x_sample, c_sample, p, pw))
```

```python
import functools

import jax
import jax.numpy as jnp
import numpy as np
from jax import lax
from jax.experimental import pallas as pl
from jax.experimental.pallas import tpu as pltpu

F32 = jnp.float32
BF16 = jnp.bfloat16

D_MODEL = 1024
DEPTH = 4
GRID_W = 64
ROPE_THETA = 10000.0
HEAD_DIM = 64
A_HEADS = 4
A_KV_HEADS = 2
B_HEADS = 4
Q_LORA = 384
KV_LORA = 128
MLA_NOPE = 64
MLA_ROPE = 32
MLA_V = 64
C_HEADS = 8
C_HEAD_DIM = 64
SSM_GROUPS = 2
D_STATE = 64
D_CONV = 5
C_INNER = C_HEADS * C_HEAD_DIM
CONV_DIM = C_INNER + 2 * SSM_GROUPS * D_STATE
N_EGROUPS = 4
EXPERTS_PER_GROUP = 8
N_EXPERTS = N_EGROUPS * EXPERTS_PER_GROUP
D_FF_EXPERT = 256
DN_ALPHA = (2 * DEPTH) ** 0.25
EPS = 1e-6

LANES = 128
SUBLANES = 8
SSD_Q = 128
NEG_BIG = -1e30
ATT_DV = 64
ATT_Q_ROWS = 256
MOE_ROWS = 144
MOE_EXPERTS_PER_STEP = 2
MOE_EXPERT_STEPS = N_EXPERTS // MOE_EXPERTS_PER_STEP

P_QA, P_KA, P_VA, P_CQ, P_CKV, P_Z, P_XBC, P_DTKR, P_END = (
    0, 256, 384, 512, 896, 1024, 1536, 2304, 2432)
KR_LANE = MLA_NOPE


def _params(sem, vmem_mb):
    return pltpu.CompilerParams(dimension_semantics=sem, vmem_limit_bytes=vmem_mb * 1024 * 1024)


def _silu(x):
    return x * (1.0 / (1.0 + jnp.exp(-x)))


def _ln_rows(x, g, b):
    mu = jnp.mean(x, axis=-1, keepdims=True)
    xc = x - mu
    var = jnp.mean(xc * xc, axis=-1, keepdims=True)
    return xc * lax.rsqrt(var + EPS) * g + b


def _rms_rows(x, g):
    return x * lax.rsqrt(jnp.mean(x * x, axis=-1, keepdims=True) + EPS) * g


def _split_bf16(x):
    hi = x.astype(BF16)
    lo = (x - hi.astype(F32)).astype(BF16)
    return hi, lo


def _dot(a, b):
    return jnp.dot(a, b, preferred_element_type=F32)


def _dot_nt(a, b):
    return lax.dot_general(a, b, (((1,), (1,)), ((), ())), preferred_element_type=F32)


def _pitch(rows):
    return rows + SUBLANES


def _rows_to_tiles(x, ct, out_ref):
    m = x.shape[0]
    nch = x.shape[1] // LANES
    p = _pitch(m)
    for j in range(nch):
        ct[j * p:j * p + m, :] = x[:, j * LANES:(j + 1) * LANES]

    def grp(g, c):
        for j in range(8):
            r = g * 8 + j
            out_ref[pl.ds(pl.multiple_of(r * 8, 8), 8), :] = ct[pl.ds(r, nch, stride=p), :]
        return c

    lax.fori_loop(0, m // 8, grp, 0)


def _tiles_to_rows(src_ref, row0, m, ct, step=1, offset=0):
    nch = D_MODEL // LANES
    p = _pitch(m)

    def grp(g, c):
        for j in range(8):
            r = g * 8 + j
            src = pl.multiple_of(((row0 + r) * step + offset) * 8, 8)
            ct[pl.ds(r, nch, stride=p), :] = src_ref[pl.ds(src, 8), :]
        return c

    lax.fori_loop(0, m // 8, grp, 0)
    return jnp.concatenate([ct[j * p:j * p + m, :] for j in range(nch)], axis=-1)


def _ada_kernel(c_ref, w_ref, b_ref, o_ref):
    cs = _silu(c_ref[...]).astype(BF16)
    o_ref[0] = _dot(cs, w_ref[0].astype(BF16)) + b_ref[0]


def _ada(c, w, b, tn=1024):
    s, d = c.shape
    depth, _, n = w.shape
    return pl.pallas_call(
        _ada_kernel,
        out_shape=jax.ShapeDtypeStruct((depth, s, n), F32),
        grid=(depth, n // tn),
        in_specs=[pl.BlockSpec((s, d), lambda l, j: (0, 0)),
                  pl.BlockSpec((1, d, tn), lambda l, j: (l, 0, j)),
                  pl.BlockSpec((1, 1, tn), lambda l, j: (l, 0, j))],
        out_specs=pl.BlockSpec((1, s, tn), lambda l, j: (l, 0, j)),
        compiler_params=_params(("parallel", "parallel"), 32),
        name="ada",
    )(c, w, b.reshape(depth, 1, n))


def _rope(x, c, p, m, shift):
    return x * c + pltpu.roll(x, shift, 1) * p + pltpu.roll(x, LANES - shift, 1) * m


def _headnorm2(x, g):
    xx = x * x
    lo = lax.broadcasted_iota(jnp.int32, x.shape, 1) < HEAD_DIM
    s0 = jnp.sum(jnp.where(lo, xx, 0.0), axis=-1, keepdims=True)
    s1 = jnp.sum(jnp.where(lo, 0.0, xx), axis=-1, keepdims=True)
    inv = jnp.where(lo, lax.rsqrt(s0 * (1.0 / HEAD_DIM) + EPS), lax.rsqrt(s1 * (1.0 / HEAD_DIM) + EPS))
    return x * inv * g


def _store_values(v2, v_o, h0):
    lo = lax.broadcasted_iota(jnp.int32, v2.shape, 1) < ATT_DV
    v_o[0, h0] = jnp.where(lo, v2, 1.0).astype(BF16)
    v_o[0, h0 + 1] = jnp.where(lo, pltpu.roll(v2, ATT_DV, 1), 1.0).astype(BF16)


def _inproj_kernel(x_ref, mod_ref, w_ref, wuq_ref, wukk_ref, wukv_ref,
                   gqa_ref, gka_ref, gqb_ref, gkvb_ref,
                   ca_ref, pa_ref, ma_ref, cb_ref, pb_ref, mb_ref, g0_ref, b0_ref,
                   qa_o, ka_o, va_o, qb_o, kb_o, vb_o, z_o, xbc_o, dt_o, *xln_o):
    m = mod_ref[0]
    x = x_ref[0]
    if xln_o:
        x = _ln_rows(x, g0_ref[...], b0_ref[...])
        xln_o[0][0] = x
    h = (x * (1.0 + m[1:2]) + m[0:1]).astype(BF16)

    def proj(a, b):
        return _dot(h, w_ref[:, a:b])

    ca, pa, ma = ca_ref[...], pa_ref[...], ma_ref[...]
    cb, pb, mb = cb_ref[...], pb_ref[...], mb_ref[...]
    half = HEAD_DIM // 2
    pq = proj(P_QA, P_KA)
    for ch in range(A_HEADS // 2):
        xa = _headnorm2(pq[:, ch * LANES:(ch + 1) * LANES], gqa_ref[...])
        xa = _rope(xa, ca, pa, ma, half // 2) * (HEAD_DIM ** -0.5)
        qa_o[0, 2 * ch] = xa[:, :HEAD_DIM].astype(BF16)
        qa_o[0, 2 * ch + 1] = xa[:, HEAD_DIM:].astype(BF16)
    pkv = proj(P_KA, P_CQ)
    xk = _rope(_headnorm2(pkv[:, :LANES], gka_ref[...]), ca, pa, ma, half // 2)
    ka_o[0, 0] = xk[:, :HEAD_DIM].astype(BF16)
    ka_o[0, 1] = xk[:, HEAD_DIM:].astype(BF16)
    _store_values(pkv[:, LANES:], va_o, 0)
    pc = proj(P_CQ, P_Z)
    cqn = _rms_rows(pc[:, :Q_LORA], gqb_ref[...]).astype(BF16)
    qb = _dot(cqn, wuq_ref[...])
    ckvn = _rms_rows(pc[:, Q_LORA:], gkvb_ref[...]).astype(BF16)
    kn = _dot(ckvn, wukk_ref[...])
    vv = _dot(ckvn, wukv_ref[...])
    for pr in range(B_HEADS // 2):
        _store_values(vv[:, pr * LANES:(pr + 1) * LANES], vb_o, 2 * pr)
    px = proj(P_XBC, P_END)
    dtkr = px[:, CONV_DIM:]
    lane = lax.broadcasted_iota(jnp.int32, dtkr.shape, 1)
    is_kr = (lane >= KR_LANE) & (lane < KR_LANE + MLA_ROPE)
    k_rope = _rope(jnp.where(is_kr, dtkr, 0.0), cb, pb, mb, MLA_ROPE // 4)
    scale_b = (MLA_NOPE + MLA_ROPE) ** -0.5
    for hh in range(B_HEADS):
        sl = slice(hh * LANES, (hh + 1) * LANES)
        qb_o[0, hh] = (_rope(qb[:, sl], cb, pb, mb, MLA_ROPE // 4) * scale_b).astype(BF16)
        kb_o[0, hh] = (kn[:, sl] + k_rope).astype(BF16)
    z_o[0] = proj(P_Z, P_XBC)
    xbc_o[0] = px[:, :CONV_DIM]
    dt_o[0] = jnp.where(lane < 2 * C_HEADS, dtkr, 0.0)


def _inproj(x, mod, lw, tabs, tm, ln0=None):
    s, l, d = x.shape
    nt = l // tm
    full = lambda shape: pl.BlockSpec(shape, lambda si, i: (0,) * len(shape))
    tab = pl.BlockSpec((tm, LANES), lambda si, i: (i, 0))
    head = lambda nh, dh: pl.BlockSpec((1, nh, tm, dh), lambda si, i: (si, 0, i, 0))
    row = lambda w: pl.BlockSpec((1, tm, w), lambda si, i: (si, i, 0))
    out_shape = (
        jax.ShapeDtypeStruct((s, A_HEADS, l, HEAD_DIM), BF16),
        jax.ShapeDtypeStruct((s, A_KV_HEADS, l, HEAD_DIM), BF16),
        jax.ShapeDtypeStruct((s, A_KV_HEADS, l, LANES), BF16),
        jax.ShapeDtypeStruct((s, B_HEADS, l, LANES), BF16),
        jax.ShapeDtypeStruct((s, B_HEADS, l, LANES), BF16),
        jax.ShapeDtypeStruct((s, B_HEADS, l, LANES), BF16),
        jax.ShapeDtypeStruct((s, l, C_INNER), F32),
        jax.ShapeDtypeStruct((s, l, CONV_DIM), F32),
        jax.ShapeDtypeStruct((s, l, LANES), F32),
    )
    out_specs = (head(A_HEADS, HEAD_DIM), head(A_KV_HEADS, HEAD_DIM), head(A_KV_HEADS, LANES),
                 head(B_HEADS, LANES), head(B_HEADS, LANES), head(B_HEADS, LANES),
                 row(C_INNER), row(CONV_DIM), row(LANES))
    if ln0 is None:
        ln0 = (jnp.ones((1, d), F32), jnp.zeros((1, d), F32))
    else:
        out_shape += (jax.ShapeDtypeStruct((s, l, d), F32),)
        out_specs += (row(d),)
    return pl.pallas_call(
        _inproj_kernel,
        out_shape=out_shape,
        grid=(s, nt),
        in_specs=[row(d),
                  pl.BlockSpec((1, 6, d), lambda si, i: (si, 0, 0)),
                  full((d, P_END)), full((Q_LORA, B_HEADS * LANES)),
                  full((KV_LORA, B_HEADS * LANES)), full((KV_LORA, B_HEADS * MLA_V)),
                  full((1, LANES)), full((1, LANES)), full((1, Q_LORA)), full((1, KV_LORA)),
                  tab, tab, tab, tab, tab, tab, full((1, d)), full((1, d))],
        out_specs=out_specs,
        compiler_params=_params(("parallel", "parallel"), 48),
        name="inproj",
    )(x, mod, lw["w_in"], lw["w_uq"], lw["w_ukk"], lw["w_ukv"],
      lw["gqa"], lw["gka"], lw["gqb"], lw["gkvb"], *tabs, *ln0)


def _attn_kernel(q_ref, k_ref, v_ref, o_ref, *, shared_kv):
    tq = q_ref.shape[2]
    qr = min(ATT_Q_ROWS, tq)
    items = [(j, r) for j in range(2) for r in range(tq // qr)]

    def scores(j, r):
        kj = 0 if shared_kv else j
        return _dot_nt(q_ref[0, j, r * qr:(r + 1) * qr, :], k_ref[0, kj])

    s_next = scores(*items[0])
    outs = {}
    for i, (j, r) in enumerate(items):
        s = s_next
        if i + 1 < len(items):
            s_next = scores(*items[i + 1])
        kj = 0 if shared_kv else j
        p = jnp.exp(s - jnp.max(s, axis=-1, keepdims=True)).astype(BF16)
        o = _dot(p, v_ref[0, kj])
        outs[(j, r)] = o[:, :ATT_DV] / o[:, ATT_DV:ATT_DV + 1]
    for r in range(tq // qr):
        o_ref[0, r * qr:(r + 1) * qr, :] = jnp.concatenate([outs[(0, r)], outs[(1, r)]], axis=-1)


def _attention(q, k, v, tq, name):
    s, hq, l, dk = q.shape
    hk = k.shape[1]
    shared = hq // hk == 2
    kvb = 1 if shared else 2
    return pl.pallas_call(
        functools.partial(_attn_kernel, shared_kv=shared),
        out_shape=jax.ShapeDtypeStruct((s, l, hq * ATT_DV), F32),
        grid=(s, hq // 2, l // tq),
        in_specs=[pl.BlockSpec((1, 2, tq, dk), lambda si, hp, qi: (si, hp, qi, 0)),
                  pl.BlockSpec((1, kvb, l, dk), lambda si, hp, qi: (si, hp, 0, 0)),
                  pl.BlockSpec((1, kvb, l, LANES), lambda si, hp, qi: (si, hp, 0, 0))],
        out_specs=pl.BlockSpec((1, tq, 2 * ATT_DV), lambda si, hp, qi: (si, qi, hp)),
        compiler_params=_params(("parallel", "parallel", "arbitrary"), 48),
        name=name,
    )(q, k, v)


def _ssd_kernel(xbc_ref, dt_ref, z_ref, cw_ref, cb_ref, dtb_ref, alog_ref, dskip_ref, g_ref,
                exf_ref, exb_ref, bd_ref, o_ref, xpad, xact, ad_s, hf, hb, pb_s, xsb_s, bt_s, dec_s):
    l = xbc_ref.shape[1]
    nc = l // SSD_Q
    q = SSD_Q
    pad = 8
    xpad[0:pad, :] = jnp.zeros((pad, CONV_DIM), F32)
    xpad[l + pad:l + 2 * pad, :] = jnp.zeros((pad, CONV_DIM), F32)
    xpad[pad:l + pad, :] = xbc_ref[0]
    def conv_chunk(c, carry):
        r0 = pl.multiple_of(c * q, q)
        for st in range(CONV_DIM // LANES):
            cols = slice(st * LANES, (st + 1) * LANES)
            win = xpad[pl.ds(r0, q + 2 * pad), cols]
            acc = jnp.zeros((q, LANES), F32) + cb_ref[:, cols]
            for k in range(D_CONV):
                lo = pad - D_CONV // 2 + k
                acc = acc + win[lo:lo + q] * cw_ref[k:k + 1, cols]
            xact[pl.ds(r0, q), cols] = _silu(acc)
        return carry

    lax.fori_loop(0, nc, conv_chunk, 0)
    dtr = dt_ref[0] + dtb_ref[...]
    dt = jnp.maximum(dtr, 0.0) + jnp.log1p(jnp.exp(-jnp.abs(dtr)))
    lane_l = lax.broadcasted_iota(jnp.int32, dt.shape, 1)
    a = dt * (-jnp.exp(alog_ref[...]))
    ad_s[...] = jnp.where(lane_l < 2 * C_HEADS, a, pltpu.roll(dt, 2 * C_HEADS, 1))

    ri = lax.broadcasted_iota(jnp.int32, (q, q), 0)
    ci = lax.broadcasted_iota(jnp.int32, (q, q), 1)
    lower = ri >= ci
    upper = ci >= ri
    tril = jnp.where(lower, 1.0, 0.0).astype(BF16)
    lane_q = lax.broadcasted_iota(jnp.int32, (q, LANES), 1)
    first8 = lane_q < C_HEADS
    mid8 = (lane_q >= C_HEADS) & (lane_q < 2 * C_HEADS)
    lo64 = lane_q < C_HEAD_DIM
    bd = bd_ref[...] > 0.5

    def chunk_decay(r0):
        ad = ad_s[pl.ds(r0, q), :]
        hi, lo = _split_bf16(ad)
        inc = _dot(tril, hi) + _dot(tril, lo)
        exc = inc - ad
        tot = inc[q - 1:q, :]
        dts = pltpu.roll(ad, LANES - 2 * C_HEADS, 1)
        return ad, inc, exc, tot, dts

    def expand(v, ex_ref):
        return _dot(v.astype(BF16), ex_ref[...])

    def expand_row(v_row, ex_ref):
        hi, lo = _split_bf16(jnp.broadcast_to(v_row, (8, LANES)))
        return (_dot(hi, ex_ref[...]) + _dot(lo, ex_ref[...]))[0:1]

    hf[...] = jnp.zeros_like(hf)
    hb[...] = jnp.zeros_like(hb)

    def fwd_pre(c):
        r0 = pl.multiple_of(c * q, q)
        ad, inc, exc, tot, dts = chunk_decay(r0)
        xq = xact[pl.ds(r0, q), :]
        xs = xq[:, :C_INNER]
        bm = xq[:, C_INNER:C_INNER + LANES]
        cm = xq[:, C_INNER + LANES:]
        bt_bf = bm.T.astype(BF16)
        gmat = []
        for g in range(SSM_GROUPS):
            cg = jnp.where(lo64 if g == 0 else jnp.logical_not(lo64), cm, 0.0).astype(BF16)
            gmat.append(_dot(cg, bt_bf))
        pf = expand(jnp.where(first8, jnp.exp(inc), 0.0), exf_ref)
        sf = expand(jnp.where(first8, jnp.exp(tot - inc) * dts, 0.0), exf_ref)
        dec = expand_row(jnp.where(first8[0:1], jnp.exp(tot), 0.0), exf_ref)
        pbk = expand(jnp.where(mid8, jnp.exp(tot - exc), 0.0), exb_ref)
        sbk = expand(jnp.where(mid8, jnp.exp(exc) * dts, 0.0), exb_ref)
        pb_s[pl.ds(r0, q), :] = pbk
        xsb_s[pl.ds(r0, q), :] = (xs * sbk).astype(BF16)
        bt_s[pl.ds(r0, q), :] = bt_bf
        dec_b = expand_row(jnp.where(mid8[0:1], jnp.exp(tot), 0.0), exb_ref)
        dec_s[pl.ds(pl.multiple_of(c * 8, 8), 8), :] = jnp.broadcast_to(dec_b, (8, C_INNER))
        return dict(r0=r0, ad=ad, inc=inc, exc=exc, xs=xs, bt_bf=bt_bf, cm_bf=cm.astype(BF16),
                    gmat=gmat, pf=pf, dec=dec, xsd=(xs * sf).astype(BF16))

    def fwd_state(v):
        y_off = _dot(v["cm_bf"], hf[...].astype(BF16)) * v["pf"]
        st = _dot(v["bt_bf"], v["xsd"])
        hf[...] = v["dec"] * hf[...] + jnp.where(bd, st, 0.0)
        return y_off

    def fwd_heads(v, y_off):
        inc, exc, ad, r0 = v["inc"], v["exc"], v["ad"], v["r0"]
        inc_t = inc.T
        ad_t = ad.T
        exc_t = inc_t - ad_t
        xs_bf = v["xs"].astype(BF16)
        for pr in range(C_HEADS // 2):
            ys = []
            for hh in (2 * pr, 2 * pr + 1):
                icol = jnp.broadcast_to(inc[:, hh:hh + 1], (q, q))
                irow = jnp.broadcast_to(inc_t[hh:hh + 1, :], (q, q))
                lf = jnp.exp(jnp.where(lower, icol - irow, NEG_BIG)) * ad_t[16 + hh:17 + hh, :]
                ecol = jnp.broadcast_to(exc[:, 8 + hh:9 + hh], (q, q))
                erow = jnp.broadcast_to(exc_t[8 + hh:9 + hh, :], (q, q))
                lb = jnp.exp(jnp.where(upper, erow - ecol, NEG_BIG)) * ad_t[24 + hh:25 + hh, :]
                w = (v["gmat"][hh // (C_HEADS // SSM_GROUPS)] * (lf + lb)).astype(BF16)
                ys.append(_dot(w, xs_bf[:, pr * LANES:(pr + 1) * LANES]))
            cols = slice(pr * LANES, (pr + 1) * LANES)
            o_ref[0, pl.ds(r0, q), cols] = jnp.where(lo64, ys[0], ys[1]) + y_off[:, cols]

    def fwd_body(i, carry):
        va = fwd_pre(2 * i)
        vb = fwd_pre(2 * i + 1)
        ya = fwd_state(va)
        yb = fwd_state(vb)
        fwd_heads(va, ya)
        fwd_heads(vb, yb)
        return carry

    lax.fori_loop(0, nc // 2, fwd_body, 0)

    def bwd_body(i, carry):
        c = nc - 1 - i
        r0 = pl.multiple_of(c * q, q)
        xs = xact[pl.ds(r0, q), :C_INNER]
        cm_bf = xact[pl.ds(r0, q), C_INNER + LANES:].astype(BF16)
        y = o_ref[0, pl.ds(r0, q), :] + _dot(cm_bf, hb[...].astype(BF16)) * pb_s[pl.ds(r0, q), :]
        st = _dot(bt_s[pl.ds(r0, q), :], xsb_s[pl.ds(r0, q), :])
        dec = dec_s[pl.ds(pl.multiple_of(c * 8, 8), 1), :]
        hb[...] = dec * hb[...] + jnp.where(bd, st, 0.0)
        y = (y + dskip_ref[...] * xs) * _silu(z_ref[0, pl.ds(r0, q), :])
        o_ref[0, pl.ds(r0, q), :] = _rms_rows(y, g_ref[...])
        return carry

    lax.fori_loop(0, nc // 2, lambda i, cr: bwd_body(2 * i + 1, bwd_body(2 * i, cr)), 0)


def _ssd(xbc, dt, z, lw, consts):
    s, l, _ = xbc.shape
    assert l % (2 * SSD_Q) == 0
    seq = lambda w: pl.BlockSpec((1, l, w), lambda si: (si, 0, 0))
    full = lambda shape: pl.BlockSpec(shape, lambda si: (0,) * len(shape))
    return pl.pallas_call(
        _ssd_kernel,
        out_shape=jax.ShapeDtypeStruct((s, l, C_INNER), F32),
        grid=(s,),
        in_specs=[seq(CONV_DIM), seq(LANES), seq(C_INNER),
                  full((D_CONV, CONV_DIM)), full((1, CONV_DIM)), full((1, LANES)), full((1, LANES)),
                  full((1, C_INNER)), full((1, C_INNER)),
                  full((LANES, C_INNER)), full((LANES, C_INNER)), full((LANES, C_INNER))],
        out_specs=seq(C_INNER),
        scratch_shapes=[pltpu.VMEM((l + 16, CONV_DIM), F32), pltpu.VMEM((l, CONV_DIM), F32),
                        pltpu.VMEM((l, LANES), F32),
                        pltpu.VMEM((LANES, C_INNER), F32), pltpu.VMEM((LANES, C_INNER), F32),
                        pltpu.VMEM((l, C_INNER), F32), pltpu.VMEM((l, C_INNER), BF16),
                        pltpu.VMEM((l, SSD_Q), BF16), pltpu.VMEM((8 * (l // SSD_Q), C_INNER), F32)],
        compiler_params=_params(("parallel",), 58),
        name="ssd",
    )(xbc, dt, z, lw["conv_w"], lw["conv_b"], lw["dt_bias"], lw["a_log"], lw["d_skip"], lw["ssm_g"],
      consts["exf"], consts["exb"], consts["bd"])


def _route(logits):
    lane = lax.broadcasted_iota(jnp.int32, logits.shape, 1)
    lane_f = lane.astype(F32)
    neg_inf = -jnp.inf
    isg = (lane >= N_EXPERTS) & (lane < N_EXPERTS + N_EGROUPS)
    gl = jnp.where(isg, logits, neg_inf)
    gmax = jnp.max(gl, axis=-1, keepdims=True)
    gsel = jnp.min(jnp.where(gl == gmax, lane_f - N_EXPERTS, 1e9), axis=-1, keepdims=True)
    gw = 1.0 / jnp.sum(jnp.where(isg, jnp.exp(gl - gmax), 0.0), axis=-1, keepdims=True)
    grp = jnp.floor(lane_f * (1.0 / EXPERTS_PER_GROUP))
    el = jnp.where((lane < N_EXPERTS) & (grp == gsel), logits, neg_inf)
    l1 = jnp.max(el, axis=-1, keepdims=True)
    i1 = jnp.min(jnp.where(el == l1, lane_f, 1e9), axis=-1, keepdims=True)
    el2 = jnp.where(lane_f == i1, neg_inf, el)
    l2 = jnp.max(el2, axis=-1, keepdims=True)
    i2 = jnp.min(jnp.where(el2 == l2, lane_f, 1e9), axis=-1, keepdims=True)
    r = jnp.exp(l2 - l1)
    g1 = gw / (1.0 + r)
    g2 = g1 * r
    return jnp.where(lane == 0, i1, jnp.where(lane == 1, i2, jnp.where(lane == 2, g1,
                                                                    jnp.where(lane == 3, g2, 0.0))))


def _outproj_kernel(oa_ref, ob_ref, oc_ref, x_ref, mod_ref, wout_ref, goa_ref, gob_ref,
                    l1g_ref, l1b_ref, wrc_ref, br_ref, x1_o, h2v_o, route_o, ct):
    hm = x_ref.shape[1] // 2
    wa = A_HEADS * HEAD_DIM
    wb = wa + B_HEADS * MLA_V
    m = mod_ref[0]

    def mix(rows):
        oa = _rms_rows(oa_ref[0, rows, :], goa_ref[...]).astype(BF16)
        ob = _rms_rows(ob_ref[0, rows, :], gob_ref[...]).astype(BF16)
        oc = oc_ref[0, rows, :].astype(BF16)
        return _dot(oa, wout_ref[0:wa, :]) + _dot(ob, wout_ref[wa:wb, :]) + _dot(oc, wout_ref[wb:, :])

    halves = [pl.ds(0, hm), pl.ds(hm, hm)]
    ys = [mix(rows) for rows in halves]
    h2s = []
    for rows, y in zip(halves, ys):
        x1 = _ln_rows(DN_ALPHA * x_ref[0, rows, :] + (1.0 + m[2:3]) * y, l1g_ref[...], l1b_ref[...])
        x1_o[0, rows, :] = x1
        h2 = x1 * (1.0 + m[4:5]) + m[3:4]
        h2s.append(h2)
        hi, lo = _split_bf16(h2)
        hw = _dot(hi, wrc_ref[...])
        logits = hw[:, :LANES] + hw[:, LANES:] + _dot(lo, wrc_ref[:, :LANES]) + br_ref[...]
        route_o[0, rows, :] = _route(logits)
    _rows_to_tiles(jnp.concatenate(h2s, axis=0), ct, h2v_o)


def _outproj(oa, ob, oc, x, mod, lw, tm):
    s, l, d = x.shape
    nt = l // tm
    nch = d // LANES
    row = lambda w: pl.BlockSpec((1, tm, w), lambda si, i: (si, i, 0))
    full = lambda shape: pl.BlockSpec(shape, lambda si, i: (0,) * len(shape))
    return pl.pallas_call(
        _outproj_kernel,
        out_shape=(jax.ShapeDtypeStruct((s, l, d), F32), jax.ShapeDtypeStruct((s * l * nch, LANES), F32),
                   jax.ShapeDtypeStruct((s, l, LANES), F32)),
        grid=(s, l // tm),
        in_specs=[row(A_HEADS * HEAD_DIM), row(B_HEADS * MLA_V), row(C_INNER), row(d),
                  pl.BlockSpec((1, 6, d), lambda si, i: (si, 0, 0)),
                  full((d, d)), full((1, A_HEADS * HEAD_DIM)), full((1, B_HEADS * MLA_V)),
                  full((1, d)), full((1, d)), full((d, 2 * LANES)), full((1, LANES))],
        out_specs=(row(d), pl.BlockSpec((tm * nch, LANES), lambda si, i: (si * nt + i, 0)), row(LANES)),
        scratch_shapes=[pltpu.VMEM((nch * _pitch(tm), LANES), F32)],
        compiler_params=_params(("parallel", "parallel"), 48),
        name="outproj",
    )(oa, ob, oc, x, mod, lw["w_out"], lw["goa"], lw["gob"], lw["ln1_g"], lw["ln1_b"],
      lw["w_r"], lw["b_r"])


ROW_STRIDE = _pitch(MOE_ROWS)


def _slot_rows(l):
    return 2 * l + N_EXPERTS * 8 + MOE_ROWS


def _route_sort_kernel(route_ref, slots_o, meta_o):
    r = route_ref[0]
    l = r.shape[0]
    lane = lax.broadcasted_iota(jnp.int32, (l, LANES), 1)
    lane_f = lane.astype(F32)
    oh1 = lane_f == jnp.broadcast_to(r[:, 0:1], (l, LANES))
    oh2 = lane_f == jnp.broadcast_to(r[:, 1:2], (l, LANES))
    member = jnp.where(oh1 | oh2, 1.0, 0.0).astype(BF16)
    tb = min(256, l)
    ri = lax.broadcasted_iota(jnp.int32, (tb, tb), 0)
    ci = lax.broadcasted_iota(jnp.int32, (tb, tb), 1)
    stril = jnp.where(ri > ci, 1.0, 0.0).astype(BF16)
    run = jnp.zeros((1, LANES), F32)
    ranks = []
    for i in range(l // tb):
        blk = member[i * tb:(i + 1) * tb]
        ranks.append(_dot(stril, blk) + run)
        run = run + jnp.sum(blk.astype(F32), axis=0, keepdims=True)
    rank_all = jnp.concatenate(ranks, axis=0)
    padded = jnp.floor((run + 7.0) * 0.125) * 8.0
    hi, lo = _split_bf16(jnp.broadcast_to(padded, (8, LANES)))
    rl = lax.broadcasted_iota(jnp.int32, (LANES, LANES), 0)
    cl = lax.broadcasted_iota(jnp.int32, (LANES, LANES), 1)
    upper = jnp.where(rl < cl, 1.0, 0.0).astype(BF16)
    base = (_dot(hi, upper) + _dot(lo, upper))[0:1]
    pos = base + rank_all
    slot1 = jnp.sum(jnp.where(oh1, pos, 0.0), axis=-1, keepdims=True)
    slot2 = jnp.sum(jnp.where(oh2, pos, 0.0), axis=-1, keepdims=True)
    packed = jnp.where(lane == 0, slot1, jnp.where(lane == 1, slot2, 0.0))
    slots_o[0] = packed.T[0:2].astype(jnp.int32)
    meta_o[0] = jnp.concatenate([base, run, jnp.zeros((6, LANES), F32)], axis=0).astype(jnp.int32)


def _route_sort(route):
    s, l, _ = route.shape
    return pl.pallas_call(
        _route_sort_kernel,
        out_shape=(jax.ShapeDtypeStruct((s, 2, l), jnp.int32), jax.ShapeDtypeStruct((s, 8, LANES), jnp.int32)),
        grid=(s,),
        in_specs=[pl.BlockSpec((1, l, LANES), lambda si: (si, 0, 0))],
        out_specs=(pl.BlockSpec((1, 2, l), lambda si: (si, 0, 0)), pl.BlockSpec((1, 8, LANES), lambda si: (si, 0, 0))),
        compiler_params=_params(("parallel",), 32),
        name="route_sort",
    )(route)


def _moe_kernel(meta_ref, h2v_ref, slots_hbm, wg_ref, wu_ref, wd_ref,
                x1_ref, route_ref, mod_ref, l2g_ref, l2b_ref, o_ref,
                slot_s, tok_s, y_v, xt, yt, ct1, ct2, sem):
    s = pl.program_id(0)
    e = pl.program_id(1)
    l = h2v_ref.shape[0] // 8
    nch = D_MODEL // LANES
    n_slots = tok_s.shape[0]
    spare = 8 * 2 * l

    @pl.when(e == 0)
    def _():
        c_slots = pltpu.make_async_copy(slots_hbm.at[s], slot_s, sem.at[0])
        c_slots.start()
        used = meta_ref[s, 0, N_EXPERTS]

        def clear_tail(g, c):
            for j in range(8):
                tok_s[used + g * 8 + j] = spare
            return c

        lax.fori_loop(0, (n_slots - used) // 8, clear_tail, 0)

        def clear(ex, c):
            n = meta_ref[s, 1, ex]
            last = meta_ref[s, 0, ex] + ((n + 7) // 8) * 8 - 8

            @pl.when(n > 0)
            def _():
                for j in range(8):
                    tok_s[last + j] = spare

            return c

        lax.fori_loop(0, N_EXPERTS, clear, 0)
        c_slots.wait()

        def invert(g, c):
            t0 = g * 8
            for k in range(2):
                for j in range(8):
                    tok_s[slot_s[k * l + t0 + j]] = 16 * t0 + (16 * j + 8 * k)
            return c

        lax.fori_loop(0, l // 8, invert, 0)

    grp_per_blk = MOE_ROWS // 8

    def ffn(sub):
        xb = jnp.concatenate([xt[sub, j * ROW_STRIDE:j * ROW_STRIDE + MOE_ROWS, :] for j in range(nch)],
                             axis=-1).astype(BF16)
        mid = (_silu(_dot(xb, wg_ref[sub])) * _dot(xb, wu_ref[sub])).astype(BF16)
        y = _dot(mid, wd_ref[sub])
        for j in range(nch):
            yt[sub, j * ROW_STRIDE:j * ROW_STRIDE + MOE_ROWS, :] = y[:, j * LANES:(j + 1) * LANES]

    def gather_row(sub, slot, i):
        src = pl.multiple_of(lax.shift_right_logical(tok_s[slot], 1) & (8 * l - 8), 8)
        xt[sub, pl.ds(i, nch, stride=ROW_STRIDE), :] = h2v_ref[pl.ds(src, 8), :]

    def scatter_row(sub, slot, i):
        y_v[pl.ds(pl.multiple_of(tok_s[slot], 8), 8), :] = yt[sub, pl.ds(i, nch, stride=ROW_STRIDE), :]

    @pl.when(e < MOE_EXPERT_STEPS)
    def _():
        info = []
        for sub in range(MOE_EXPERTS_PER_STEP):
            ex = e * MOE_EXPERTS_PER_STEP + sub
            info.append((meta_ref[s, 0, ex], meta_ref[s, 1, ex]))
        for sub, (base, n) in enumerate(info):
            for i in range(MOE_ROWS):
                gather_row(sub, base + i, i)
        for sub in range(MOE_EXPERTS_PER_STEP):
            ffn(sub)
        for sub, (base, n) in enumerate(info):
            for i in range(MOE_ROWS):
                scatter_row(sub, base + i, i)
        for sub, (base, n) in enumerate(info):
            ngrp = (n + 7) // 8

            def block(blk, c, sub=sub, base=base, ngrp=ngrp):
                g_lo = blk * grp_per_blk
                ng = jnp.minimum(grp_per_blk, ngrp - g_lo)

                def gather(g, cc):
                    for j in range(8):
                        gather_row(sub, base + (g_lo + g) * 8 + j, g * 8 + j)
                    return cc

                lax.fori_loop(0, ng, gather, 0)
                ffn(sub)

                def scatter(g, cc):
                    for j in range(8):
                        scatter_row(sub, base + (g_lo + g) * 8 + j, g * 8 + j)
                    return cc

                lax.fori_loop(0, ng, scatter, 0)
                return c

            lax.fori_loop(1, (ngrp + grp_per_blk - 1) // grp_per_blk, block, 0)

    @pl.when(e >= MOE_EXPERT_STEPS)
    def _():
        tmo = x1_ref.shape[1]
        t0 = (e - MOE_EXPERT_STEPS) * tmo
        r = route_ref[0]
        y = (r[:, 2:3] * _tiles_to_rows(y_v, t0, tmo, ct1, 2, 0)
             + r[:, 3:4] * _tiles_to_rows(y_v, t0, tmo, ct2, 2, 1))
        m = mod_ref[0]
        o_ref[0] = _ln_rows(DN_ALPHA * x1_ref[0] + (1.0 + m[5:6]) * y, l2g_ref[...], l2b_ref[...])


def _moe(h2v, slots, meta, x1, route, mod, lw, tm):
    s, l, d = x1.shape
    assert l & (l - 1) == 0, "the gather address mask needs a power-of-two sequence length"
    n_slots = _slot_rows(l)
    eps = MOE_EXPERTS_PER_STEP
    wspec = lambda shape: pl.BlockSpec(shape, lambda si, e: (jnp.minimum(e, MOE_EXPERT_STEPS - 1), 0, 0))
    blk_idx = lambda si, e: (si, jnp.maximum(e - MOE_EXPERT_STEPS, 0), 0)
    tok_blk = pl.BlockSpec((1, tm, d), blk_idx)
    vec = pl.BlockSpec((1, d), lambda si, e: (0, 0))
    return pl.pallas_call(
        _moe_kernel,
        out_shape=jax.ShapeDtypeStruct((s, l, d), F32),
        grid=(s, MOE_EXPERT_STEPS + l // tm),
        in_specs=[pl.BlockSpec(memory_space=pltpu.SMEM),
                  pl.BlockSpec((l * 8, LANES), lambda si, e: (si, 0)),
                  pl.BlockSpec(memory_space=pl.ANY),
                  wspec((eps, d, D_FF_EXPERT)), wspec((eps, d, D_FF_EXPERT)), wspec((eps, D_FF_EXPERT, d)),
                  tok_blk, pl.BlockSpec((1, tm, LANES), blk_idx),
                  pl.BlockSpec((1, 6, d), lambda si, e: (si, 0, 0)), vec, vec],
        out_specs=tok_blk,
        scratch_shapes=[pltpu.SMEM((2 * l,), jnp.int32), pltpu.SMEM((n_slots,), jnp.int32),
                        pltpu.VMEM(((2 * l + 1) * 8, LANES), F32),
                        pltpu.VMEM((eps, 8 * ROW_STRIDE, LANES), F32), pltpu.VMEM((eps, 8 * ROW_STRIDE, LANES), F32),
                        pltpu.VMEM((8 * _pitch(tm), LANES), F32), pltpu.VMEM((8 * _pitch(tm), LANES), F32),
                        pltpu.SemaphoreType.DMA((1,))],
        compiler_params=_params(("parallel", "arbitrary"), 60),
        name="moe",
    )(meta, h2v, slots, lw["w_e_gate"], lw["w_e_up"], lw["w_e_down"], x1, route, mod,
      lw["ln2_g"], lw["ln2_b"])


def _rope_tables(seq_len):
    t = jnp.arange(seq_len, dtype=jnp.int32)
    row = (t // GRID_W).astype(F32)[:, None]
    col = (t % GRID_W).astype(F32)[:, None]

    def axis_tables(d_axis):
        inv = ROPE_THETA ** (-jnp.arange(0, d_axis, 2, dtype=F32) / d_axis)
        zero = jnp.zeros((seq_len, d_axis // 2), F32)
        cs, ps, ms = [], [], []
        for pos in (row, col):
            ang = pos * inv
            c, sn = jnp.cos(ang), jnp.sin(ang)
            cs += [c, c]
            ps += [zero, sn]
            ms += [-sn, zero]
        return [jnp.concatenate(v, axis=-1) for v in (cs, ps, ms)]

    ta = [jnp.tile(v, (1, 2)) for v in axis_tables(HEAD_DIM // 2)]
    one = jnp.ones((seq_len, MLA_NOPE), F32)
    zero = jnp.zeros((seq_len, MLA_NOPE), F32)
    pad1 = jnp.ones((seq_len, LANES - MLA_NOPE - MLA_ROPE), F32)
    pad0 = jnp.zeros((seq_len, LANES - MLA_NOPE - MLA_ROPE), F32)
    cb, pb, mb = axis_tables(MLA_ROPE // 2)
    tb = [jnp.concatenate([one, cb, pad1], -1), jnp.concatenate([zero, pb, pad0], -1),
          jnp.concatenate([zero, mb, pad0], -1)]
    return ta + tb


def _ssd_consts():
    r = np.arange(LANES)[:, None]
    c = np.arange(C_INNER)[None, :]
    exf = ((r < C_HEADS) & (c // C_HEAD_DIM == r)).astype(np.float32)
    exb = ((r >= C_HEADS) & (r < 2 * C_HEADS) & (c // C_HEAD_DIM == r - C_HEADS)).astype(np.float32)
    heads_per_group = C_HEADS // SSM_GROUPS
    bd = (r // D_STATE == c // (C_HEAD_DIM * heads_per_group)).astype(np.float32)
    return {"exf": jnp.asarray(exf, BF16), "exb": jnp.asarray(exb, BF16), "bd": jnp.asarray(bd, F32)}


def _pack_weights(p):
    w_in = p["w_in"].astype(BF16)
    depth = w_in.shape[0]
    off = np.cumsum((0, 256, 128, 128, Q_LORA, KV_LORA, MLA_ROPE, C_INNER, CONV_DIM, 2 * C_HEADS))
    w_kr = w_in[:, :, off[5]:off[6]]
    zpad = lambda n: jnp.zeros((depth, D_MODEL, n), BF16)
    dtkr = jnp.concatenate([w_in[:, :, off[8]:], zpad(KR_LANE - 2 * C_HEADS), w_kr,
                            zpad(LANES - KR_LANE - MLA_ROPE)], -1)
    w_in_p = jnp.concatenate([w_in[:, :, :off[5]], w_in[:, :, off[6]:off[8]], dtkr], -1)
    dq = MLA_NOPE + MLA_ROPE
    w_uq = p["w_uq"].reshape(depth, Q_LORA, B_HEADS, dq)
    w_uq_p = jnp.concatenate([w_uq, jnp.zeros((depth, Q_LORA, B_HEADS, LANES - dq), F32)], -1)
    w_ukv = p["w_ukv"].reshape(depth, KV_LORA, B_HEADS, MLA_NOPE + MLA_V)
    w_ukk = jnp.concatenate([w_ukv[..., :MLA_NOPE],
                             jnp.zeros((depth, KV_LORA, B_HEADS, LANES - MLA_NOPE), F32)], -1)
    pad_lanes = lambda v: jnp.concatenate([v, jnp.zeros(v.shape[:-1] + (LANES - v.shape[-1],), F32)], -1)
    w_r = pad_lanes(jnp.concatenate([p["w_re"], p["w_rg"]], -1))
    w_r_hi = w_r.astype(BF16)
    return {
        "w_in": w_in_p,
        "w_uq": w_uq_p.reshape(depth, Q_LORA, B_HEADS * LANES).astype(BF16),
        "w_ukk": w_ukk.reshape(depth, KV_LORA, B_HEADS * LANES).astype(BF16),
        "w_ukv": w_ukv[..., MLA_NOPE:].reshape(depth, KV_LORA, B_HEADS * MLA_V).astype(BF16),
        "gqa": jnp.tile(p["qa_norm_g"], (1, 2))[:, None, :], "gka": jnp.tile(p["ka_norm_g"], (1, 2))[:, None, :],
        "gqb": p["qb_norm_g"][:, None, :], "gkvb": p["kvb_norm_g"][:, None, :],
        "goa": p["oa_norm_g"][:, None, :], "gob": p["ob_norm_g"][:, None, :],
        "conv_w": p["conv_w"], "conv_b": p["conv_b"][:, None, :],
        "dt_bias": pad_lanes(p["dt_bias"].reshape(depth, 2 * C_HEADS))[:, None, :],
        "a_log": pad_lanes(p["a_log"].reshape(depth, 2 * C_HEADS))[:, None, :],
        "d_skip": jnp.repeat(p["d_skip"], C_HEAD_DIM, axis=-1)[:, None, :],
        "ssm_g": p["ssm_norm_g"][:, None, :],
        "w_out": p["w_out"].astype(BF16),
        "ln1_g": p["ln1_g"][:, None, :], "ln1_b": p["ln1_b"][:, None, :],
        "w_r": jnp.concatenate([w_r_hi, (w_r - w_r_hi.astype(F32)).astype(BF16)], -1),
        "b_r": pad_lanes(jnp.concatenate([p["b_re"], p["b_rg"]], -1))[:, None, :],
        "w_e_gate": p["w_e_gate"].astype(BF16), "w_e_up": p["w_e_up"].astype(BF16),
        "w_e_down": p["w_e_down"].astype(BF16),
        "ln2_g": p["ln2_g"][:, None, :], "ln2_b": p["ln2_b"][:, None, :],
    }


def _trunk(x, c, p, pw=None, mods=None, tm=512, tq=2048):
    s, l, d = x.shape
    tm = min(tm, l)
    tq = min(tq, l)
    pw = _pack_weights(p) if pw is None else pw
    tabs = _rope_tables(l)
    consts = _ssd_consts()
    if mods is None:
        mods = _ada(c, p["w_ada"], p["b_ada"])
    mods = mods.reshape(DEPTH, s, 6, d)
    for layer in range(DEPTH):
        lw = {k: v[layer] for k, v in pw.items()}
        mod = mods[layer]
        if layer == 0:
            *proj, x = _inproj(x, mod, lw, tabs, tm, ln0=(p["ln0_g"][None, :], p["ln0_b"][None, :]))
        else:
            proj = _inproj(x, mod, lw, tabs, tm)
        qa, ka, va, qb, kb, vb, z, xbc, dt = proj
        oa = _attention(qa, ka, va, tq, "attn_a")
        ob = _attention(qb, kb, vb, tq, "attn_b")
        oc = _ssd(xbc, dt, z, lw, consts)
        x1, h2v, route = _outproj(oa, ob, oc, x, mod, lw, tm)
        slots, meta = _route_sort(route)
        x = _moe(h2v, slots.reshape(s, 2 * l), meta, x1, route, mod, lw, tm)
    return x


def kernel(x_prompt, x_sample, c_prompt, c_sample, ln0_g, ln0_b, w_ada, b_ada, w_in, qa_norm_g, ka_norm_g, oa_norm_g, qb_norm_g, w_uq, kvb_norm_g, w_ukv, ob_norm_g, conv_w, conv_b, dt_bias, a_log, d_skip, ssm_norm_g, w_out, ln1_g, ln1_b, w_rg, b_rg, w_re, b_re, w_e_gate, w_e_up, w_e_down, ln2_g, ln2_b):
    p = dict(ln0_g=ln0_g, ln0_b=ln0_b, w_ada=w_ada, b_ada=b_ada, w_in=w_in,
             qa_norm_g=qa_norm_g, ka_norm_g=ka_norm_g, oa_norm_g=oa_norm_g,
             qb_norm_g=qb_norm_g, w_uq=w_uq, kvb_norm_g=kvb_norm_g, w_ukv=w_ukv, ob_norm_g=ob_norm_g,
             conv_w=conv_w, conv_b=conv_b, dt_bias=dt_bias, a_log=a_log, d_skip=d_skip,
             ssm_norm_g=ssm_norm_g, w_out=w_out, ln1_g=ln1_g, ln1_b=ln1_b,
             w_rg=w_rg, b_rg=b_rg, w_re=w_re, b_re=b_re,
             w_e_gate=w_e_gate, w_e_up=w_e_up, w_e_down=w_e_down, ln2_g=ln2_g, ln2_b=ln2_b)
    pw = _pack_weights(p)
    nb = x_prompt.shape[0]
    mods = _ada(jnp.concatenate([c_prompt, c_sample], 0), w_ada, b_ada)
    return (_trunk(x_prompt, c_prompt, p, pw, mods[:, :nb]), _trunk(x_sample, c_sample, p, pw, mods[:, nb:]))
```

```python
import functools

import jax
import jax.numpy as jnp
import numpy as np
from jax import lax
from jax.experimental import pallas as pl
from jax.experimental.pallas import tpu as pltpu

F32 = jnp.float32
BF16 = jnp.bfloat16

D_MODEL = 1024
DEPTH = 4
GRID_W = 64
ROPE_THETA = 10000.0
HEAD_DIM = 64
A_HEADS = 4
A_KV_HEADS = 2
B_HEADS = 4
Q_LORA = 384
KV_LORA = 128
MLA_NOPE = 64
MLA_ROPE = 32
MLA_V = 64
C_HEADS = 8
C_HEAD_DIM = 64
SSM_GROUPS = 2
D_STATE = 64
D_CONV = 5
C_INNER = C_HEADS * C_HEAD_DIM
CONV_DIM = C_INNER + 2 * SSM_GROUPS * D_STATE
N_EGROUPS = 4
EXPERTS_PER_GROUP = 8
N_EXPERTS = N_EGROUPS * EXPERTS_PER_GROUP
D_FF_EXPERT = 256
DN_ALPHA = (2 * DEPTH) ** 0.25
EPS = 1e-6

LANES = 128
SUBLANES = 8
SSD_Q = 128
NEG_BIG = -1e30
ATT_DV = 64
ATT_Q_ROWS = 256
MOE_ROWS = 144
MOE_EXPERTS_PER_STEP = 2
MOE_EXPERT_STEPS = N_EXPERTS // MOE_EXPERTS_PER_STEP

P_QA, P_KA, P_VA, P_CQ, P_CKV, P_Z, P_XBC, P_DTKR, P_END = (
    0, 256, 384, 512, 896, 1024, 1536, 2304, 2432)
KR_LANE = MLA_NOPE


def _params(sem, vmem_mb):
    return pltpu.CompilerParams(dimension_semantics=sem, vmem_limit_bytes=vmem_mb * 1024 * 1024)


def _silu(x):
    return x * (1.0 / (1.0 + jnp.exp(-x)))


def _ln_rows(x, g, b):
    mu = jnp.mean(x, axis=-1, keepdims=True)
    xc = x - mu
    var = jnp.mean(xc * xc, axis=-1, keepdims=True)
    return xc * lax.rsqrt(var + EPS) * g + b


def _rms_rows(x, g):
    return x * lax.rsqrt(jnp.mean(x * x, axis=-1, keepdims=True) + EPS) * g


def _split_bf16(x):
    hi = x.astype(BF16)
    lo = (x - hi.astype(F32)).astype(BF16)
    return hi, lo


def _dot(a, b):
    return jnp.dot(a, b, preferred_element_type=F32)


def _dot_nt(a, b):
    return lax.dot_general(a, b, (((1,), (1,)), ((), ())), preferred_element_type=F32)


def _pitch(rows):
    return rows + SUBLANES


def _rows_to_tiles(x, ct, out_ref):
    m = x.shape[0]
    nch = x.shape[1] // LANES
    p = _pitch(m)
    for j in range(nch):
        ct[j * p:j * p + m, :] = x[:, j * LANES:(j + 1) * LANES]

    def grp(g, c):
        for j in range(8):
            r = g * 8 + j
            out_ref[pl.ds(pl.multiple_of(r * 8, 8), 8), :] = ct[pl.ds(r, nch, stride=p), :]
        return c

    lax.fori_loop(0, m // 8, grp, 0)


def _tiles_to_rows(src_ref, row0, m, ct, step=1, offset=0):
    nch = D_MODEL // LANES
    p = _pitch(m)

    def grp(g, c):
        for j in range(8):
            r = g * 8 + j
            src = pl.multiple_of(((row0 + r) * step + offset) * 8, 8)
            ct[pl.ds(r, nch, stride=p), :] = src_ref[pl.ds(src, 8), :]
        return c

    lax.fori_loop(0, m // 8, grp, 0)
    return jnp.concatenate([ct[j * p:j * p + m, :] for j in range(nch)], axis=-1)


def _ada_kernel(c_ref, w_ref, b_ref, o_ref):
    cs = _silu(c_ref[...]).astype(BF16)
    o_ref[0] = _dot(cs, w_ref[0].astype(BF16)) + b_ref[0]


def _ada(c, w, b, tn=1024):
    s, d = c.shape
    depth, _, n = w.shape
    return pl.pallas_call(
        _ada_kernel,
        out_shape=jax.ShapeDtypeStruct((depth, s, n), F32),
        grid=(depth, n // tn),
        in_specs=[pl.BlockSpec((s, d), lambda l, j: (0, 0)),
                  pl.BlockSpec((1, d, tn), lambda l, j: (l, 0, j)),
                  pl.BlockSpec((1, 1, tn), lambda l, j: (l, 0, j))],
        out_specs=pl.BlockSpec((1, s, tn), lambda l, j: (l, 0, j)),
        compiler_params=_params(("parallel", "parallel"), 32),
        name="ada",
    )(c, w, b.reshape(depth, 1, n))


def _rope(x, c, p, m, shift):
    return x * c + pltpu.roll(x, shift, 1) * p + pltpu.roll(x, LANES - shift, 1) * m


def _headnorm2(x, g):
    xx = x * x
    lo = lax.broadcasted_iota(jnp.int32, x.shape, 1) < HEAD_DIM
    s0 = jnp.sum(jnp.where(lo, xx, 0.0), axis=-1, keepdims=True)
    s1 = jnp.sum(jnp.where(lo, 0.0, xx), axis=-1, keepdims=True)
    inv = jnp.where(lo, lax.rsqrt(s0 * (1.0 / HEAD_DIM) + EPS), lax.rsqrt(s1 * (1.0 / HEAD_DIM) + EPS))
    return x * inv * g


def _store_values(v2, v_o, h0):
    lo = lax.broadcasted_iota(jnp.int32, v2.shape, 1) < ATT_DV
    v_o[0, h0] = jnp.where(lo, v2, 1.0).astype(BF16)
    v_o[0, h0 + 1] = jnp.where(lo, pltpu.roll(v2, ATT_DV, 1), 1.0).astype(BF16)


def _inproj_kernel(x_ref, mod_ref, w_ref, wuq_ref, wukk_ref, wukv_ref,
                   gqa_ref, gka_ref, gqb_ref, gkvb_ref,
                   ca_ref, pa_ref, ma_ref, cb_ref, pb_ref, mb_ref, g0_ref, b0_ref,
                   qa_o, ka_o, va_o, qb_o, kb_o, vb_o, z_o, xbc_o, dt_o, *xln_o):
    m = mod_ref[0]
    x = x_ref[0]
    if xln_o:
        x = _ln_rows(x, g0_ref[...], b0_ref[...])
        xln_o[0][0] = x
    h = (x * (1.0 + m[1:2]) + m[0:1]).astype(BF16)

    def proj(a, b):
        return _dot(h, w_ref[:, a:b])

    ca, pa, ma = ca_ref[...], pa_ref[...], ma_ref[...]
    cb, pb, mb = cb_ref[...], pb_ref[...], mb_ref[...]
    half = HEAD_DIM // 2
    pq = proj(P_QA, P_KA)
    for ch in range(A_HEADS // 2):
        xa = _headnorm2(pq[:, ch * LANES:(ch + 1) * LANES], gqa_ref[...])
        xa = _rope(xa, ca, pa, ma, half // 2) * (HEAD_DIM ** -0.5)
        qa_o[0, 2 * ch] = xa[:, :HEAD_DIM].astype(BF16)
        qa_o[0, 2 * ch + 1] = xa[:, HEAD_DIM:].astype(BF16)
    pkv = proj(P_KA, P_CQ)
    xk = _rope(_headnorm2(pkv[:, :LANES], gka_ref[...]), ca, pa, ma, half // 2)
    ka_o[0, 0] = xk[:, :HEAD_DIM].astype(BF16)
    ka_o[0, 1] = xk[:, HEAD_DIM:].astype(BF16)
    _store_values(pkv[:, LANES:], va_o, 0)
    pc = proj(P_CQ, P_Z)
    cqn = _rms_rows(pc[:, :Q_LORA], gqb_ref[...]).astype(BF16)
    qb = _dot(cqn, wuq_ref[...])
    ckvn = _rms_rows(pc[:, Q_LORA:], gkvb_ref[...]).astype(BF16)
    kn = _dot(ckvn, wukk_ref[...])
    vv = _dot(ckvn, wukv_ref[...])
    for pr in range(B_HEADS // 2):
        _store_values(vv[:, pr * LANES:(pr + 1) * LANES], vb_o, 2 * pr)
    px = proj(P_XBC, P_END)
    dtkr = px[:, CONV_DIM:]
    lane = lax.broadcasted_iota(jnp.int32, dtkr.shape, 1)
    is_kr = (lane >= KR_LANE) & (lane < KR_LANE + MLA_ROPE)
    k_rope = _rope(jnp.where(is_kr, dtkr, 0.0), cb, pb, mb, MLA_ROPE // 4)
    scale_b = (MLA_NOPE + MLA_ROPE) ** -0.5
    for hh in range(B_HEADS):
        sl = slice(hh * LANES, (hh + 1) * LANES)
        qb_o[0, hh] = (_rope(qb[:, sl], cb, pb, mb, MLA_ROPE // 4) * scale_b).astype(BF16)
        kb_o[0, hh] = (kn[:, sl] + k_rope).astype(BF16)
    z_o[0] = proj(P_Z, P_XBC)
    xbc_o[0] = px[:, :CONV_DIM]
    dt_o[0] = jnp.where(lane < 2 * C_HEADS, dtkr, 0.0)


def _inproj(x, mod, lw, tabs, tm, ln0=None):
    s, l, d = x.shape
    nt = l // tm
    full = lambda shape: pl.BlockSpec(shape, lambda si, i: (0,) * len(shape))
    tab = pl.BlockSpec((tm, LANES), lambda si, i: (i, 0))
    head = lambda nh, dh: pl.BlockSpec((1, nh, tm, dh), lambda si, i: (si, 0, i, 0))
    row = lambda w: pl.BlockSpec((1, tm, w), lambda si, i: (si, i, 0))
    out_shape = (
        jax.ShapeDtypeStruct((s, A_HEADS, l, HEAD_DIM), BF16),
        jax.ShapeDtypeStruct((s, A_KV_HEADS, l, HEAD_DIM), BF16),
        jax.ShapeDtypeStruct((s, A_KV_HEADS, l, LANES), BF16),
        jax.ShapeDtypeStruct((s, B_HEADS, l, LANES), BF16),
        jax.ShapeDtypeStruct((s, B_HEADS, l, LANES), BF16),
        jax.ShapeDtypeStruct((s, B_HEADS, l, LANES), BF16),
        jax.ShapeDtypeStruct((s, l, C_INNER), F32),
        jax.ShapeDtypeStruct((s, l, CONV_DIM), F32),
        jax.ShapeDtypeStruct((s, l, LANES), F32),
    )
    out_specs = (head(A_HEADS, HEAD_DIM), head(A_KV_HEADS, HEAD_DIM), head(A_KV_HEADS, LANES),
                 head(B_HEADS, LANES), head(B_HEADS, LANES), head(B_HEADS, LANES),
                 row(C_INNER), row(CONV_DIM), row(LANES))
    if ln0 is None:
        ln0 = (jnp.ones((1, d), F32), jnp.zeros((1, d), F32))
    else:
        out_shape += (jax.ShapeDtypeStruct((s, l, d), F32),)
        out_specs += (row(d),)
    return pl.pallas_call(
        _inproj_kernel,
        out_shape=out_shape,
        grid=(s, nt),
        in_specs=[row(d),
                  pl.BlockSpec((1, 6, d), lambda si, i: (si, 0, 0)),
                  full((d, P_END)), full((Q_LORA, B_HEADS * LANES)),
                  full((KV_LORA, B_HEADS * LANES)), full((KV_LORA, B_HEADS * MLA_V)),
                  full((1, LANES)), full((1, LANES)), full((1, Q_LORA)), full((1, KV_LORA)),
                  tab, tab, tab, tab, tab, tab, full((1, d)), full((1, d))],
        out_specs=out_specs,
        compiler_params=_params(("parallel", "parallel"), 48),
        name="inproj",
    )(x, mod, lw["w_in"], lw["w_uq"], lw["w_ukk"], lw["w_ukv"],
      lw["gqa"], lw["gka"], lw["gqb"], lw["gkvb"], *tabs, *ln0)


def _attn_kernel(q_ref, k_ref, v_ref, o_ref, *, shared_kv):
    tq = q_ref.shape[2]
    qr = min(ATT_Q_ROWS, tq)
    items = [(j, r) for j in range(2) for r in range(tq // qr)]

    def scores(j, r):
        kj = 0 if shared_kv else j
        return _dot_nt(q_ref[0, j, r * qr:(r + 1) * qr, :], k_ref[0, kj])

    s_next = scores(*items[0])
    outs = {}
    for i, (j, r) in enumerate(items):
        s = s_next
        if i + 1 < len(items):
            s_next = scores(*items[i + 1])
        kj = 0 if shared_kv else j
        p = jnp.exp(s - jnp.max(s, axis=-1, keepdims=True)).astype(BF16)
        o = _dot(p, v_ref[0, kj])
        outs[(j, r)] = o[:, :ATT_DV] / o[:, ATT_DV:ATT_DV + 1]
    for r in range(tq // qr):
        o_ref[0, r * qr:(r + 1) * qr, :] = jnp.concatenate([outs[(0, r)], outs[(1, r)]], axis=-1)


def _attention(q, k, v, tq, name):
    s, hq, l, dk = q.shape
    hk = k.shape[1]
    shared = hq // hk == 2
    kvb = 1 if shared else 2
    return pl.pallas_call(
        functools.partial(_attn_kernel, shared_kv=shared),
        out_shape=jax.ShapeDtypeStruct((s, l, hq * ATT_DV), F32),
        grid=(s, hq // 2, l // tq),
        in_specs=[pl.BlockSpec((1, 2, tq, dk), lambda si, hp, qi: (si, hp, qi, 0)),
                  pl.BlockSpec((1, kvb, l, dk), lambda si, hp, qi: (si, hp, 0, 0)),
                  pl.BlockSpec((1, kvb, l, LANES), lambda si, hp, qi: (si, hp, 0, 0))],
        out_specs=pl.BlockSpec((1, tq, 2 * ATT_DV), lambda si, hp, qi: (si, qi, hp)),
        compiler_params=_params(("parallel", "parallel", "arbitrary"), 48),
        name=name,
    )(q, k, v)


def _ssd_kernel(xbc_ref, dt_ref, z_ref, cw_ref, cb_ref, dtb_ref, alog_ref, dskip_ref, g_ref,
                exf_ref, exb_ref, bd_ref, o_ref, xpad, xact, ad_s, hf, hb, pb_s, xsb_s, bt_s, dec_s):
    l = xbc_ref.shape[1]
    nc = l // SSD_Q
    q = SSD_Q
    pad = 8
    xpad[0:pad, :] = jnp.zeros((pad, CONV_DIM), F32)
    xpad[l + pad:l + 2 * pad, :] = jnp.zeros((pad, CONV_DIM), F32)
    xpad[pad:l + pad, :] = xbc_ref[0]
    def conv_chunk(c, carry):
        r0 = pl.multiple_of(c * q, q)
        for st in range(CONV_DIM // LANES):
            cols = slice(st * LANES, (st + 1) * LANES)
            win = xpad[pl.ds(r0, q + 2 * pad), cols]
            acc = jnp.zeros((q, LANES), F32) + cb_ref[:, cols]
            for k in range(D_CONV):
                lo = pad - D_CONV // 2 + k
                acc = acc + win[lo:lo + q] * cw_ref[k:k + 1, cols]
            xact[pl.ds(r0, q), cols] = _silu(acc)
        return carry

    lax.fori_loop(0, nc, conv_chunk, 0)
    dtr = dt_ref[0] + dtb_ref[...]
    dt = jnp.maximum(dtr, 0.0) + jnp.log1p(jnp.exp(-jnp.abs(dtr)))
    lane_l = lax.broadcasted_iota(jnp.int32, dt.shape, 1)
    a = dt * (-jnp.exp(alog_ref[...]))
    ad_s[...] = jnp.where(lane_l < 2 * C_HEADS, a, pltpu.roll(dt, 2 * C_HEADS, 1))

    ri = lax.broadcasted_iota(jnp.int32, (q, q), 0)
    ci = lax.broadcasted_iota(jnp.int32, (q, q), 1)
    lower = ri >= ci
    upper = ci >= ri
    tril = jnp.where(lower, 1.0, 0.0).astype(BF16)
    lane_q = lax.broadcasted_iota(jnp.int32, (q, LANES), 1)
    first8 = lane_q < C_HEADS
    mid8 = (lane_q >= C_HEADS) & (lane_q < 2 * C_HEADS)
    lo64 = lane_q < C_HEAD_DIM
    bd = bd_ref[...] > 0.5

    def chunk_decay(r0):
        ad = ad_s[pl.ds(r0, q), :]
        hi, lo = _split_bf16(ad)
        inc = _dot(tril, hi) + _dot(tril, lo)
        exc = inc - ad
        tot = inc[q - 1:q, :]
        dts = pltpu.roll(ad, LANES - 2 * C_HEADS, 1)
        return ad, inc, exc, tot, dts

    def expand(v, ex_ref):
        return _dot(v.astype(BF16), ex_ref[...])

    def expand_row(v_row, ex_ref):
        hi, lo = _split_bf16(jnp.broadcast_to(v_row, (8, LANES)))
        return (_dot(hi, ex_ref[...]) + _dot(lo, ex_ref[...]))[0:1]

    hf[...] = jnp.zeros_like(hf)
    hb[...] = jnp.zeros_like(hb)

    def fwd_pre(c):
        r0 = pl.multiple_of(c * q, q)
        ad, inc, exc, tot, dts = chunk_decay(r0)
        xq = xact[pl.ds(r0, q), :]
        xs = xq[:, :C_INNER]
        bm = xq[:, C_INNER:C_INNER + LANES]
        cm = xq[:, C_INNER + LANES:]
        bt_bf = bm.T.astype(BF16)
        gmat = []
        for g in range(SSM_GROUPS):
            cg = jnp.where(lo64 if g == 0 else jnp.logical_not(lo64), cm, 0.0).astype(BF16)
            gmat.append(_dot(cg, bt_bf))
        pf = expand(jnp.where(first8, jnp.exp(inc), 0.0), exf_ref)
        sf = expand(jnp.where(first8, jnp.exp(tot - inc) * dts, 0.0), exf_ref)
        dec = expand_row(jnp.where(first8[0:1], jnp.exp(tot), 0.0), exf_ref)
        pbk = expand(jnp.where(mid8, jnp.exp(tot - exc), 0.0), exb_ref)
        sbk = expand(jnp.where(mid8, jnp.exp(exc) * dts, 0.0), exb_ref)
        pb_s[pl.ds(r0, q), :] = pbk
        xsb_s[pl.ds(r0, q), :] = (xs * sbk).astype(BF16)
        bt_s[pl.ds(r0, q), :] = bt_bf
        dec_b = expand_row(jnp.where(mid8[0:1], jnp.exp(tot), 0.0), exb_ref)
        dec_s[pl.ds(pl.multiple_of(c * 8, 8), 8), :] = jnp.broadcast_to(dec_b, (8, C_INNER))
        return dict(r0=r0, ad=ad, inc=inc, exc=exc, xs=xs, bt_bf=bt_bf, cm_bf=cm.astype(BF16),
                    gmat=gmat, pf=pf, dec=dec, xsd=(xs * sf).astype(BF16))

    def fwd_state(v):
        y_off = _dot(v["cm_bf"], hf[...].astype(BF16)) * v["pf"]
        st = _dot(v["bt_bf"], v["xsd"])
        hf[...] = v["dec"] * hf[...] + jnp.where(bd, st, 0.0)
        return y_off

    def fwd_heads(v, y_off):
        inc, exc, ad, r0 = v["inc"], v["exc"], v["ad"], v["r0"]
        inc_t = inc.T
        ad_t = ad.T
        exc_t = inc_t - ad_t
        xs_bf = v["xs"].astype(BF16)
        for pr in range(C_HEADS // 2):
            ys = []
            for hh in (2 * pr, 2 * pr + 1):
                icol = jnp.broadcast_to(inc[:, hh:hh + 1], (q, q))
                irow = jnp.broadcast_to(inc_t[hh:hh + 1, :], (q, q))
                lf = jnp.exp(jnp.where(lower, icol - irow, NEG_BIG)) * ad_t[16 + hh:17 + hh, :]
                ecol = jnp.broadcast_to(exc[:, 8 + hh:9 + hh], (q, q))
                erow = jnp.broadcast_to(exc_t[8 + hh:9 + hh, :], (q, q))
                lb = jnp.exp(jnp.where(upper, erow - ecol, NEG_BIG)) * ad_t[24 + hh:25 + hh, :]
                w = (v["gmat"][hh // (C_HEADS // SSM_GROUPS)] * (lf + lb)).astype(BF16)
                ys.append(_dot(w, xs_bf[:, pr * LANES:(pr + 1) * LANES]))
            cols = slice(pr * LANES, (pr + 1) * LANES)
            o_ref[0, pl.ds(r0, q), cols] = jnp.where(lo64, ys[0], ys[1]) + y_off[:, cols]

    def fwd_body(i, carry):
        va = fwd_pre(2 * i)
        vb = fwd_pre(2 * i + 1)
        ya = fwd_state(va)
        yb = fwd_state(vb)
        fwd_heads(va, ya)
        fwd_heads(vb, yb)
        return carry

    lax.fori_loop(0, nc // 2, fwd_body, 0)

    def bwd_body(i, carry):
        c = nc - 1 - i
        r0 = pl.multiple_of(c * q, q)
        xs = xact[pl.ds(r0, q), :C_INNER]
        cm_bf = xact[pl.ds(r0, q), C_INNER + LANES:].astype(BF16)
        y = o_ref[0, pl.ds(r0, q), :] + _dot(cm_bf, hb[...].astype(BF16)) * pb_s[pl.ds(r0, q), :]
        st = _dot(bt_s[pl.ds(r0, q), :], xsb_s[pl.ds(r0, q), :])
        dec = dec_s[pl.ds(pl.multiple_of(c * 8, 8), 1), :]
        hb[...] = dec * hb[...] + jnp.where(bd, st, 0.0)
        y = (y + dskip_ref[...] * xs) * _silu(z_ref[0, pl.ds(r0, q), :])
        o_ref[0, pl.ds(r0, q), :] = _rms_rows(y, g_ref[...])
        return carry

    lax.fori_loop(0, nc // 2, lambda i, cr: bwd_body(2 * i + 1, bwd_body(2 * i, cr)), 0)


def _ssd(xbc, dt, z, lw, consts):
    s, l, _ = xbc.shape
    assert l % (2 * SSD_Q) == 0
    seq = lambda w: pl.BlockSpec((1, l, w), lambda si: (si, 0, 0))
    full = lambda shape: pl.BlockSpec(shape, lambda si: (0,) * len(shape))
    return pl.pallas_call(
        _ssd_kernel,
        out_shape=jax.ShapeDtypeStruct((s, l, C_INNER), F32),
        grid=(s,),
        in_specs=[seq(CONV_DIM), seq(LANES), seq(C_INNER),
                  full((D_CONV, CONV_DIM)), full((1, CONV_DIM)), full((1, LANES)), full((1, LANES)),
                  full((1, C_INNER)), full((1, C_INNER)),
                  full((LANES, C_INNER)), full((LANES, C_INNER)), full((LANES, C_INNER))],
        out_specs=seq(C_INNER),
        scratch_shapes=[pltpu.VMEM((l + 16, CONV_DIM), F32), pltpu.VMEM((l, CONV_DIM), F32),
                        pltpu.VMEM((l, LANES), F32),
                        pltpu.VMEM((LANES, C_INNER), F32), pltpu.VMEM((LANES, C_INNER), F32),
                        pltpu.VMEM((l, C_INNER), F32), pltpu.VMEM((l, C_INNER), BF16),
                        pltpu.VMEM((l, SSD_Q), BF16), pltpu.VMEM((8 * (l // SSD_Q), C_INNER), F32)],
        compiler_params=_params(("parallel",), 58),
        name="ssd",
    )(xbc, dt, z, lw["conv_w"], lw["conv_b"], lw["dt_bias"], lw["a_log"], lw["d_skip"], lw["ssm_g"],
      consts["exf"], consts["exb"], consts["bd"])


def _route(logits):
    lane = lax.broadcasted_iota(jnp.int32, logits.shape, 1)
    lane_f = lane.astype(F32)
    neg_inf = -jnp.inf
    isg = (lane >= N_EXPERTS) & (lane < N_EXPERTS + N_EGROUPS)
    gl = jnp.where(isg, logits, neg_inf)
    gmax = jnp.max(gl, axis=-1, keepdims=True)
    gsel = jnp.min(jnp.where(gl == gmax, lane_f - N_EXPERTS, 1e9), axis=-1, keepdims=True)
    gw = 1.0 / jnp.sum(jnp.where(isg, jnp.exp(gl - gmax), 0.0), axis=-1, keepdims=True)
    grp = jnp.floor(lane_f * (1.0 / EXPERTS_PER_GROUP))
    el = jnp.where((lane < N_EXPERTS) & (grp == gsel), logits, neg_inf)
    l1 = jnp.max(el, axis=-1, keepdims=True)
    i1 = jnp.min(jnp.where(el == l1, lane_f, 1e9), axis=-1, keepdims=True)
    el2 = jnp.where(lane_f == i1, neg_inf, el)
    l2 = jnp.max(el2, axis=-1, keepdims=True)
    i2 = jnp.min(jnp.where(el2 == l2, lane_f, 1e9), axis=-1, keepdims=True)
    r = jnp.exp(l2 - l1)
    g1 = gw / (1.0 + r)
    g2 = g1 * r
    return jnp.where(lane == 0, i1, jnp.where(lane == 1, i2, jnp.where(lane == 2, g1,
                                                                    jnp.where(lane == 3, g2, 0.0))))


def _outproj_kernel(oa_ref, ob_ref, oc_ref, x_ref, mod_ref, wout_ref, goa_ref, gob_ref,
                    l1g_ref, l1b_ref, wrc_ref, br_ref, x1_o, h2v_o, route_o, ct):
    hm = x_ref.shape[1] // 2
    wa = A_HEADS * HEAD_DIM
    wb = wa + B_HEADS * MLA_V
    m = mod_ref[0]

    def mix(rows):
        oa = _rms_rows(oa_ref[0, rows, :], goa_ref[...]).astype(BF16)
        ob = _rms_rows(ob_ref[0, rows, :], gob_ref[...]).astype(BF16)
        oc = oc_ref[0, rows, :].astype(BF16)
        return _dot(oa, wout_ref[0:wa, :]) + _dot(ob, wout_ref[wa:wb, :]) + _dot(oc, wout_ref[wb:, :])

    halves = [pl.ds(0, hm), pl.ds(hm, hm)]
    ys = [mix(rows) for rows in halves]
    h2s = []
    for rows, y in zip(halves, ys):
        x1 = _ln_rows(DN_ALPHA * x_ref[0, rows, :] + (1.0 + m[2:3]) * y, l1g_ref[...], l1b_ref[...])
        x1_o[0, rows, :] = x1
        h2 = x1 * (1.0 + m[4:5]) + m[3:4]
        h2s.append(h2)
        hi, lo = _split_bf16(h2)
        hw = _dot(hi, wrc_ref[...])
        logits = hw[:, :LANES] + hw[:, LANES:] + _dot(lo, wrc_ref[:, :LANES]) + br_ref[...]
        route_o[0, rows, :] = _route(logits)
    _rows_to_tiles(jnp.concatenate(h2s, axis=0), ct, h2v_o)


def _outproj(oa, ob, oc, x, mod, lw, tm):
    s, l, d = x.shape
    nt = l // tm
    nch = d // LANES
    row = lambda w: pl.BlockSpec((1, tm, w), lambda si, i: (si, i, 0))
    full = lambda shape: pl.BlockSpec(shape, lambda si, i: (0,) * len(shape))
    return pl.pallas_call(
        _outproj_kernel,
        out_shape=(jax.ShapeDtypeStruct((s, l, d), F32), jax.ShapeDtypeStruct((s * l * nch, LANES), F32),
                   jax.ShapeDtypeStruct((s, l, LANES), F32)),
        grid=(s, l // tm),
        in_specs=[row(A_HEADS * HEAD_DIM), row(B_HEADS * MLA_V), row(C_INNER), row(d),
                  pl.BlockSpec((1, 6, d), lambda si, i: (si, 0, 0)),
                  full((d, d)), full((1, A_HEADS * HEAD_DIM)), full((1, B_HEADS * MLA_V)),
                  full((1, d)), full((1, d)), full((d, 2 * LANES)), full((1, LANES))],
        out_specs=(row(d), pl.BlockSpec((tm * nch, LANES), lambda si, i: (si * nt + i, 0)), row(LANES)),
        scratch_shapes=[pltpu.VMEM((nch * _pitch(tm), LANES), F32)],
        compiler_params=_params(("parallel", "parallel"), 48),
        name="outproj",
    )(oa, ob, oc, x, mod, lw["w_out"], lw["goa"], lw["gob"], lw["ln1_g"], lw["ln1_b"],
      lw["w_r"], lw["b_r"])


ROW_STRIDE = _pitch(MOE_ROWS)


def _slot_rows(l):
    return 2 * l + N_EXPERTS * 8 + MOE_ROWS


def _route_sort_kernel(route_ref, slots_o, meta_o):
    r = route_ref[0]
    l = r.shape[0]
    lane = lax.broadcasted_iota(jnp.int32, (l, LANES), 1)
    lane_f = lane.astype(F32)
    oh1 = lane_f == jnp.broadcast_to(r[:, 0:1], (l, LANES))
    oh2 = lane_f == jnp.broadcast_to(r[:, 1:2], (l, LANES))
    member = jnp.where(oh1 | oh2, 1.0, 0.0).astype(BF16)
    tb = min(256, l)
    ri = lax.broadcasted_iota(jnp.int32, (tb, tb), 0)
    ci = lax.broadcasted_iota(jnp.int32, (tb, tb), 1)
    stril = jnp.where(ri > ci, 1.0, 0.0).astype(BF16)
    run = jnp.zeros((1, LANES), F32)
    ranks = []
    for i in range(l // tb):
        blk = member[i * tb:(i + 1) * tb]
        ranks.append(_dot(stril, blk) + run)
        run = run + jnp.sum(blk.astype(F32), axis=0, keepdims=True)
    rank_all = jnp.concatenate(ranks, axis=0)
    padded = jnp.floor((run + 7.0) * 0.125) * 8.0
    hi, lo = _split_bf16(jnp.broadcast_to(padded, (8, LANES)))
    rl = lax.broadcasted_iota(jnp.int32, (LANES, LANES), 0)
    cl = lax.broadcasted_iota(jnp.int32, (LANES, LANES), 1)
    upper = jnp.where(rl < cl, 1.0, 0.0).astype(BF16)
    base = (_dot(hi, upper) + _dot(lo, upper))[0:1]
    pos = base + rank_all
    slot1 = jnp.sum(jnp.where(oh1, pos, 0.0), axis=-1, keepdims=True)
    slot2 = jnp.sum(jnp.where(oh2, pos, 0.0), axis=-1, keepdims=True)
    packed = jnp.where(lane == 0, slot1, jnp.where(lane == 1, slot2, 0.0))
    slots_o[0] = packed.T[0:2].astype(jnp.int32)
    meta_o[0] = jnp.concatenate([base, run, jnp.zeros((6, LANES), F32)], axis=0).astype(jnp.int32)


def _route_sort(route):
    s, l, _ = route.shape
    return pl.pallas_call(
        _route_sort_kernel,
        out_shape=(jax.ShapeDtypeStruct((s, 2, l), jnp.int32), jax.ShapeDtypeStruct((s, 8, LANES), jnp.int32)),
        grid=(s,),
        in_specs=[pl.BlockSpec((1, l, LANES), lambda si: (si, 0, 0))],
        out_specs=(pl.BlockSpec((1, 2, l), lambda si: (si, 0, 0)), pl.BlockSpec((1, 8, LANES), lambda si: (si, 0, 0))),
        compiler_params=_params(("parallel",), 32),
        name="route_sort",
    )(route)


def _moe_kernel(meta_ref, h2v_ref, slots_hbm, wg_ref, wu_ref, wd_ref,
                x1_ref, route_ref, mod_ref, l2g_ref, l2b_ref, o_ref,
                slot_s, tok_s, y_v, xt, yt, ct1, ct2, sem):
    s = pl.program_id(0)
    e = pl.program_id(1)
    l = h2v_ref.shape[0] // 8
    nch = D_MODEL // LANES
    n_slots = tok_s.shape[0]
    spare = 8 * 2 * l

    @pl.when(e == 0)
    def _():
        c_slots = pltpu.make_async_copy(slots_hbm.at[s], slot_s, sem.at[0])
        c_slots.start()
        used = meta_ref[s, 0, N_EXPERTS]

        def clear_tail(g, c):
            for j in range(8):
                tok_s[used + g * 8 + j] = spare
            return c

        lax.fori_loop(0, (n_slots - used) // 8, clear_tail, 0)

        def clear(ex, c):
            n = meta_ref[s, 1, ex]
            last = meta_ref[s, 0, ex] + ((n + 7) // 8) * 8 - 8

            @pl.when(n > 0)
            def _():
                for j in range(8):
                    tok_s[last + j] = spare

            return c

        lax.fori_loop(0, N_EXPERTS, clear, 0)
        c_slots.wait()

        def invert(g, c):
            t0 = g * 8
            for k in range(2):
                for j in range(8):
                    tok_s[slot_s[k * l + t0 + j]] = 16 * t0 + (16 * j + 8 * k)
            return c

        lax.fori_loop(0, l // 8, invert, 0)

    grp_per_blk = MOE_ROWS // 8

    def ffn(sub):
        xb = jnp.concatenate([xt[sub, j * ROW_STRIDE:j * ROW_STRIDE + MOE_ROWS, :] for j in range(nch)],
                             axis=-1).astype(BF16)
        mid = (_silu(_dot(xb, wg_ref[sub])) * _dot(xb, wu_ref[sub])).astype(BF16)
        y = _dot(mid, wd_ref[sub])
        for j in range(nch):
            yt[sub, j * ROW_STRIDE:j * ROW_STRIDE + MOE_ROWS, :] = y[:, j * LANES:(j + 1) * LANES]

    def gather_row(sub, slot, i):
        src = pl.multiple_of(lax.shift_right_logical(tok_s[slot], 1) & (8 * l - 8), 8)
        xt[sub, pl.ds(i, nch, stride=ROW_STRIDE), :] = h2v_ref[pl.ds(src, 8), :]

    def scatter_row(sub, slot, i):
        y_v[pl.ds(pl.multiple_of(tok_s[slot], 8), 8), :] = yt[sub, pl.ds(i, nch, stride=ROW_STRIDE), :]

    @pl.when(e < MOE_EXPERT_STEPS)
    def _():
        info = []
        for sub in range(MOE_EXPERTS_PER_STEP):
            ex = e * MOE_EXPERTS_PER_STEP + sub
            info.append((meta_ref[s, 0, ex], meta_ref[s, 1, ex]))
        for sub, (base, n) in enumerate(info):
            for i in range(MOE_ROWS):
                gather_row(sub, base + i, i)
        for sub in range(MOE_EXPERTS_PER_STEP):
            ffn(sub)
        for sub, (base, n) in enumerate(info):
            for i in range(MOE_ROWS):
                scatter_row(sub, base + i, i)
        for sub, (base, n) in enumerate(info):
            ngrp = (n + 7) // 8

            def block(blk, c, sub=sub, base=base, ngrp=ngrp):
                g_lo = blk * grp_per_blk
                ng = jnp.minimum(grp_per_blk, ngrp - g_lo)

                def gather(g, cc):
                    for j in range(8):
                        gather_row(sub, base + (g_lo + g) * 8 + j, g * 8 + j)
                    return cc

                lax.fori_loop(0, ng, gather, 0)
                ffn(sub)

                def scatter(g, cc):
                    for j in range(8):
                        scatter_row(sub, base + (g_lo + g) * 8 + j, g * 8 + j)
                    return cc

                lax.fori_loop(0, ng, scatter, 0)
                return c

            lax.fori_loop(1, (ngrp + grp_per_blk - 1) // grp_per_blk, block, 0)

    @pl.when(e >= MOE_EXPERT_STEPS)
    def _():
        tmo = x1_ref.shape[1]
        t0 = (e - MOE_EXPERT_STEPS) * tmo
        r = route_ref[0]
        y = (r[:, 2:3] * _tiles_to_rows(y_v, t0, tmo, ct1, 2, 0)
             + r[:, 3:4] * _tiles_to_rows(y_v, t0, tmo, ct2, 2, 1))
        m = mod_ref[0]
        o_ref[0] = _ln_rows(DN_ALPHA * x1_ref[0] + (1.0 + m[5:6]) * y, l2g_ref[...], l2b_ref[...])


def _moe(h2v, slots, meta, x1, route, mod, lw, experts, layer, tm):
    s, l, d = x1.shape
    assert l & (l - 1) == 0, "the gather address mask needs a power-of-two sequence length"
    n_slots = _slot_rows(l)
    eps = MOE_EXPERTS_PER_STEP
    wspec = lambda shape: pl.BlockSpec(
        shape, lambda si, e: (layer * MOE_EXPERT_STEPS + jnp.minimum(e, MOE_EXPERT_STEPS - 1), 0, 0))
    blk_idx = lambda si, e: (si, jnp.maximum(e - MOE_EXPERT_STEPS, 0), 0)
    tok_blk = pl.BlockSpec((1, tm, d), blk_idx)
    vec = pl.BlockSpec((1, d), lambda si, e: (0, 0))
    return pl.pallas_call(
        _moe_kernel,
        out_shape=jax.ShapeDtypeStruct((s, l, d), F32),
        grid=(s, MOE_EXPERT_STEPS + l // tm),
        in_specs=[pl.BlockSpec(memory_space=pltpu.SMEM),
                  pl.BlockSpec((l * 8, LANES), lambda si, e: (si, 0)),
                  pl.BlockSpec(memory_space=pl.ANY),
                  wspec((eps, d, D_FF_EXPERT)), wspec((eps, d, D_FF_EXPERT)), wspec((eps, D_FF_EXPERT, d)),
                  tok_blk, pl.BlockSpec((1, tm, LANES), blk_idx),
                  pl.BlockSpec((1, 6, d), lambda si, e: (si, 0, 0)), vec, vec],
        out_specs=tok_blk,
        scratch_shapes=[pltpu.SMEM((2 * l,), jnp.int32), pltpu.SMEM((n_slots,), jnp.int32),
                        pltpu.VMEM(((2 * l + 1) * 8, LANES), F32),
                        pltpu.VMEM((eps, 8 * ROW_STRIDE, LANES), F32), pltpu.VMEM((eps, 8 * ROW_STRIDE, LANES), F32),
                        pltpu.VMEM((8 * _pitch(tm), LANES), F32), pltpu.VMEM((8 * _pitch(tm), LANES), F32),
                        pltpu.SemaphoreType.DMA((1,))],
        compiler_params=_params(("parallel", "arbitrary"), 60),
        name="moe",
    )(meta, h2v, slots, *experts, x1, route, mod,
      lw["ln2_g"], lw["ln2_b"])


def _rope_tables(seq_len):
    t = jnp.arange(seq_len, dtype=jnp.int32)
    row = (t // GRID_W).astype(F32)[:, None]
    col = (t % GRID_W).astype(F32)[:, None]

    def axis_tables(d_axis):
        inv = ROPE_THETA ** (-jnp.arange(0, d_axis, 2, dtype=F32) / d_axis)
        zero = jnp.zeros((seq_len, d_axis // 2), F32)
        cs, ps, ms = [], [], []
        for pos in (row, col):
            ang = pos * inv
            c, sn = jnp.cos(ang), jnp.sin(ang)
            cs += [c, c]
            ps += [zero, sn]
            ms += [-sn, zero]
        return [jnp.concatenate(v, axis=-1) for v in (cs, ps, ms)]

    ta = [jnp.tile(v, (1, 2)) for v in axis_tables(HEAD_DIM // 2)]
    one = jnp.ones((seq_len, MLA_NOPE), F32)
    zero = jnp.zeros((seq_len, MLA_NOPE), F32)
    pad1 = jnp.ones((seq_len, LANES - MLA_NOPE - MLA_ROPE), F32)
    pad0 = jnp.zeros((seq_len, LANES - MLA_NOPE - MLA_ROPE), F32)
    cb, pb, mb = axis_tables(MLA_ROPE // 2)
    tb = [jnp.concatenate([one, cb, pad1], -1), jnp.concatenate([zero, pb, pad0], -1),
          jnp.concatenate([zero, mb, pad0], -1)]
    return ta + tb


def _ssd_consts():
    r = np.arange(LANES)[:, None]
    c = np.arange(C_INNER)[None, :]
    exf = ((r < C_HEADS) & (c // C_HEAD_DIM == r)).astype(np.float32)
    exb = ((r >= C_HEADS) & (r < 2 * C_HEADS) & (c // C_HEAD_DIM == r - C_HEADS)).astype(np.float32)
    heads_per_group = C_HEADS // SSM_GROUPS
    bd = (r // D_STATE == c // (C_HEAD_DIM * heads_per_group)).astype(np.float32)
    return {"exf": jnp.asarray(exf, BF16), "exb": jnp.asarray(exb, BF16), "bd": jnp.asarray(bd, F32)}


def _pack_weights(p):
    w_in = p["w_in"].astype(BF16)
    depth = w_in.shape[0]
    off = np.cumsum((0, 256, 128, 128, Q_LORA, KV_LORA, MLA_ROPE, C_INNER, CONV_DIM, 2 * C_HEADS))
    w_kr = w_in[:, :, off[5]:off[6]]
    zpad = lambda n: jnp.zeros((depth, D_MODEL, n), BF16)
    dtkr = jnp.concatenate([w_in[:, :, off[8]:], zpad(KR_LANE - 2 * C_HEADS), w_kr,
                            zpad(LANES - KR_LANE - MLA_ROPE)], -1)
    w_in_p = jnp.concatenate([w_in[:, :, :off[5]], w_in[:, :, off[6]:off[8]], dtkr], -1)
    dq = MLA_NOPE + MLA_ROPE
    w_uq = p["w_uq"].reshape(depth, Q_LORA, B_HEADS, dq)
    w_uq_p = jnp.concatenate([w_uq, jnp.zeros((depth, Q_LORA, B_HEADS, LANES - dq), F32)], -1)
    w_ukv = p["w_ukv"].reshape(depth, KV_LORA, B_HEADS, MLA_NOPE + MLA_V)
    w_ukk = jnp.concatenate([w_ukv[..., :MLA_NOPE],
                             jnp.zeros((depth, KV_LORA, B_HEADS, LANES - MLA_NOPE), F32)], -1)
    pad_lanes = lambda v: jnp.concatenate([v, jnp.zeros(v.shape[:-1] + (LANES - v.shape[-1],), F32)], -1)
    w_r = pad_lanes(jnp.concatenate([p["w_re"], p["w_rg"]], -1))
    w_r_hi = w_r.astype(BF16)
    return {
        "w_in": w_in_p,
        "w_uq": w_uq_p.reshape(depth, Q_LORA, B_HEADS * LANES).astype(BF16),
        "w_ukk": w_ukk.reshape(depth, KV_LORA, B_HEADS * LANES).astype(BF16),
        "w_ukv": w_ukv[..., MLA_NOPE:].reshape(depth, KV_LORA, B_HEADS * MLA_V).astype(BF16),
        "gqa": jnp.tile(p["qa_norm_g"], (1, 2))[:, None, :], "gka": jnp.tile(p["ka_norm_g"], (1, 2))[:, None, :],
        "gqb": p["qb_norm_g"][:, None, :], "gkvb": p["kvb_norm_g"][:, None, :],
        "goa": p["oa_norm_g"][:, None, :], "gob": p["ob_norm_g"][:, None, :],
        "conv_w": p["conv_w"], "conv_b": p["conv_b"][:, None, :],
        "dt_bias": pad_lanes(p["dt_bias"].reshape(depth, 2 * C_HEADS))[:, None, :],
        "a_log": pad_lanes(p["a_log"].reshape(depth, 2 * C_HEADS))[:, None, :],
        "d_skip": jnp.repeat(p["d_skip"], C_HEAD_DIM, axis=-1)[:, None, :],
        "ssm_g": p["ssm_norm_g"][:, None, :],
        "w_out": p["w_out"].astype(BF16),
        "ln1_g": p["ln1_g"][:, None, :], "ln1_b": p["ln1_b"][:, None, :],
        "w_r": jnp.concatenate([w_r_hi, (w_r - w_r_hi.astype(F32)).astype(BF16)], -1),
        "b_r": pad_lanes(jnp.concatenate([p["b_re"], p["b_rg"]], -1))[:, None, :],
        "ln2_g": p["ln2_g"][:, None, :], "ln2_b": p["ln2_b"][:, None, :],
        "experts": tuple(p[k].astype(BF16).reshape((depth * N_EXPERTS,) + p[k].shape[2:])
                         for k in ("w_e_gate", "w_e_up", "w_e_down")),
    }


def _trunk(x, c, p, pw=None, mods=None, tm=512, tq=2048):
    s, l, d = x.shape
    tm = min(tm, l)
    tq = min(tq, l)
    pw = _pack_weights(p) if pw is None else pw
    tabs = _rope_tables(l)
    consts = _ssd_consts()
    if mods is None:
        mods = _ada(c, p["w_ada"], p["b_ada"])
    mods = mods.reshape(DEPTH, s, 6, d)
    for layer in range(DEPTH):
        lw = {k: v[layer] for k, v in pw.items() if k != "experts"}
        mod = mods[layer]
        if layer == 0:
            *proj, x = _inproj(x, mod, lw, tabs, tm, ln0=(p["ln0_g"][None, :], p["ln0_b"][None, :]))
        else:
            proj = _inproj(x, mod, lw, tabs, tm)
        qa, ka, va, qb, kb, vb, z, xbc, dt = proj
        oa = _attention(qa, ka, va, tq, "attn_a")
        ob = _attention(qb, kb, vb, tq, "attn_b")
        oc = _ssd(xbc, dt, z, lw, consts)
        x1, h2v, route = _outproj(oa, ob, oc, x, mod, lw, tm)
        slots, meta = _route_sort(route)
        x = _moe(h2v, slots.reshape(s, 2 * l), meta, x1, route, mod, lw, pw["experts"], layer, tm)
    return x


def kernel(x_prompt, x_sample, c_prompt, c_sample, ln0_g, ln0_b, w_ada, b_ada, w_in, qa_norm_g, ka_norm_g, oa_norm_g, qb_norm_g, w_uq, kvb_norm_g, w_ukv, ob_norm_g, conv_w, conv_b, dt_bias, a_log, d_skip, ssm_norm_g, w_out, ln1_g, ln1_b, w_rg, b_rg, w_re, b_re, w_e_gate, w_e_up, w_e_down, ln2_g, ln2_b):
    p = dict(ln0_g=ln0_g, ln0_b=ln0_b, w_ada=w_ada, b_ada=b_ada, w_in=w_in,
             qa_norm_g=qa_norm_g, ka_norm_g=ka_norm_g, oa_norm_g=oa_norm_g,
             qb_norm_g=qb_norm_g, w_uq=w_uq, kvb_norm_g=kvb_norm_g, w_ukv=w_ukv, ob_norm_g=ob_norm_g,
             conv_w=conv_w, conv_b=conv_b, dt_bias=dt_bias, a_log=a_log, d_skip=d_skip,
             ssm_norm_g=ssm_norm_g, w_out=w_out, ln1_g=ln1_g, ln1_b=ln1_b,
             w_rg=w_rg, b_rg=b_rg, w_re=w_re, b_re=b_re,
             w_e_gate=w_e_gate, w_e_up=w_e_up, w_e_down=w_e_down, ln2_g=ln2_g, ln2_b=ln2_b)
    pw = _pack_weights(p)
    nb = x_prompt.shape[0]
    mods = _ada(jnp.concatenate([c_prompt, c_sample], 0), w_ada, b_ada)
    return (_trunk(x_prompt, c_prompt, p, pw, mods[:, :nb]), _trunk(x_sample, c_sample, p, pw, mods[:, nb:]))
```

```python
import functools

import jax
import jax.numpy as jnp
import numpy as np
from jax import lax
from jax.experimental import pallas as pl
from jax.experimental.pallas import tpu as pltpu

F32 = jnp.float32
BF16 = jnp.bfloat16

D_MODEL = 1024
DEPTH = 4
GRID_W = 64
ROPE_THETA = 10000.0
HEAD_DIM = 64
A_HEADS = 4
A_KV_HEADS = 2
B_HEADS = 4
Q_LORA = 384
KV_LORA = 128
MLA_NOPE = 64
MLA_ROPE = 32
MLA_V = 64
C_HEADS = 8
C_HEAD_DIM = 64
SSM_GROUPS = 2
D_STATE = 64
D_CONV = 5
C_INNER = C_HEADS * C_HEAD_DIM
CONV_DIM = C_INNER + 2 * SSM_GROUPS * D_STATE
N_EGROUPS = 4
EXPERTS_PER_GROUP = 8
N_EXPERTS = N_EGROUPS * EXPERTS_PER_GROUP
D_FF_EXPERT = 256
DN_ALPHA = (2 * DEPTH) ** 0.25
EPS = 1e-6

LANES = 128
SUBLANES = 8
SSD_Q = 128
NEG_BIG = -1e30
ATT_DV = 64
ATT_Q_ROWS = 512
MOE_ROWS = 144
MOE_EXPERTS_PER_STEP = 2
MOE_EXPERT_STEPS = N_EXPERTS // MOE_EXPERTS_PER_STEP

P_QA, P_KA, P_VA, P_CQ, P_CKV, P_Z, P_XBC, P_DTKR, P_END = (
    0, 256, 384, 512, 896, 1024, 1536, 2304, 2432)
KR_LANE = MLA_NOPE


def _params(sem, vmem_mb):
    return pltpu.CompilerParams(dimension_semantics=sem, vmem_limit_bytes=vmem_mb * 1024 * 1024)


def _silu(x):
    return x * (1.0 / (1.0 + jnp.exp(-x)))


def _ln_rows(x, g, b):
    mu = jnp.mean(x, axis=-1, keepdims=True)
    xc = x - mu
    var = jnp.mean(xc * xc, axis=-1, keepdims=True)
    return xc * lax.rsqrt(var + EPS) * g + b


def _rms_rows(x, g):
    return x * lax.rsqrt(jnp.mean(x * x, axis=-1, keepdims=True) + EPS) * g


def _split_bf16(x):
    hi = x.astype(BF16)
    lo = (x - hi.astype(F32)).astype(BF16)
    return hi, lo


def _dot(a, b):
    return jnp.dot(a, b, preferred_element_type=F32)


def _dot_nt(a, b):
    return lax.dot_general(a, b, (((1,), (1,)), ((), ())), preferred_element_type=F32)


def _pitch(rows):
    return rows + SUBLANES


def _rows_to_tiles(x, ct, out_ref):
    m = x.shape[0]
    nch = x.shape[1] // LANES
    p = _pitch(m)
    for j in range(nch):
        ct[j * p:j * p + m, :] = x[:, j * LANES:(j + 1) * LANES]

    def grp(g, c):
        for j in range(8):
            r = g * 8 + j
            out_ref[pl.ds(pl.multiple_of(r * 8, 8), 8), :] = ct[pl.ds(r, nch, stride=p), :]
        return c

    lax.fori_loop(0, m // 8, grp, 0)


def _tiles_to_rows(src_ref, row0, m, ct, step=1, offset=0):
    nch = D_MODEL // LANES
    p = _pitch(m)

    def grp(g, c):
        for j in range(8):
            r = g * 8 + j
            src = pl.multiple_of(((row0 + r) * step + offset) * 8, 8)
            ct[pl.ds(r, nch, stride=p), :] = src_ref[pl.ds(src, 8), :]
        return c

    lax.fori_loop(0, m // 8, grp, 0)
    return jnp.concatenate([ct[j * p:j * p + m, :] for j in range(nch)], axis=-1)


def _ada_kernel(c_ref, w_ref, b_ref, o_ref):
    cs = _silu(c_ref[...]).astype(BF16)
    o_ref[0] = _dot(cs, w_ref[0].astype(BF16)) + b_ref[0]


def _ada(c, w, b, tn=1024):
    s, d = c.shape
    depth, _, n = w.shape
    return pl.pallas_call(
        _ada_kernel,
        out_shape=jax.ShapeDtypeStruct((depth, s, n), F32),
        grid=(depth, n // tn),
        in_specs=[pl.BlockSpec((s, d), lambda l, j: (0, 0)),
                  pl.BlockSpec((1, d, tn), lambda l, j: (l, 0, j)),
                  pl.BlockSpec((1, 1, tn), lambda l, j: (l, 0, j))],
        out_specs=pl.BlockSpec((1, s, tn), lambda l, j: (l, 0, j)),
        compiler_params=_params(("parallel", "parallel"), 32),
        name="ada",
    )(c, w, b.reshape(depth, 1, n))


def _rope(x, c, p, m, shift):
    return x * c + pltpu.roll(x, shift, 1) * p + pltpu.roll(x, LANES - shift, 1) * m


def _headnorm2(x, g):
    xx = x * x
    lo = lax.broadcasted_iota(jnp.int32, x.shape, 1) < HEAD_DIM
    s0 = jnp.sum(jnp.where(lo, xx, 0.0), axis=-1, keepdims=True)
    s1 = jnp.sum(jnp.where(lo, 0.0, xx), axis=-1, keepdims=True)
    inv = jnp.where(lo, lax.rsqrt(s0 * (1.0 / HEAD_DIM) + EPS), lax.rsqrt(s1 * (1.0 / HEAD_DIM) + EPS))
    return x * inv * g


def _store_values(v2, v_o, h0):
    lo = lax.broadcasted_iota(jnp.int32, v2.shape, 1) < ATT_DV
    v_o[0, h0] = jnp.where(lo, v2, 1.0).astype(BF16)
    v_o[0, h0 + 1] = jnp.where(lo, pltpu.roll(v2, ATT_DV, 1), 1.0).astype(BF16)


def _inproj_kernel(x_ref, mod_ref, w_ref, wuq_ref, wukk_ref, wukv_ref,
                   gqa_ref, gka_ref, gqb_ref, gkvb_ref,
                   ca_ref, pa_ref, ma_ref, cb_ref, pb_ref, mb_ref, g0_ref, b0_ref,
                   qa_o, ka_o, va_o, qb_o, kb_o, vb_o, z_o, xbc_o, dt_o, *xln_o):
    m = mod_ref[0]
    x = x_ref[0]
    if xln_o:
        x = _ln_rows(x, g0_ref[...], b0_ref[...])
        xln_o[0][0] = x
    h = (x * (1.0 + m[1:2]) + m[0:1]).astype(BF16)

    def proj(a, b):
        return _dot(h, w_ref[:, a:b])

    ca, pa, ma = ca_ref[...], pa_ref[...], ma_ref[...]
    cb, pb, mb = cb_ref[...], pb_ref[...], mb_ref[...]
    half = HEAD_DIM // 2
    pq = proj(P_QA, P_KA)
    for ch in range(A_HEADS // 2):
        xa = _headnorm2(pq[:, ch * LANES:(ch + 1) * LANES], gqa_ref[...])
        xa = _rope(xa, ca, pa, ma, half // 2) * (HEAD_DIM ** -0.5)
        qa_o[0, 2 * ch] = xa[:, :HEAD_DIM].astype(BF16)
        qa_o[0, 2 * ch + 1] = xa[:, HEAD_DIM:].astype(BF16)
    pkv = proj(P_KA, P_CQ)
    xk = _rope(_headnorm2(pkv[:, :LANES], gka_ref[...]), ca, pa, ma, half // 2)
    ka_o[0, 0] = xk[:, :HEAD_DIM].astype(BF16)
    ka_o[0, 1] = xk[:, HEAD_DIM:].astype(BF16)
    _store_values(pkv[:, LANES:], va_o, 0)
    pc = proj(P_CQ, P_Z)
    cqn = _rms_rows(pc[:, :Q_LORA], gqb_ref[...]).astype(BF16)
    qb = _dot(cqn, wuq_ref[...])
    ckvn = _rms_rows(pc[:, Q_LORA:], gkvb_ref[...]).astype(BF16)
    kn = _dot(ckvn, wukk_ref[...])
    vv = _dot(ckvn, wukv_ref[...])
    for pr in range(B_HEADS // 2):
        _store_values(vv[:, pr * LANES:(pr + 1) * LANES], vb_o, 2 * pr)
    px = proj(P_XBC, P_END)
    dtkr = px[:, CONV_DIM:]
    lane = lax.broadcasted_iota(jnp.int32, dtkr.shape, 1)
    is_kr = (lane >= KR_LANE) & (lane < KR_LANE + MLA_ROPE)
    k_rope = _rope(jnp.where(is_kr, dtkr, 0.0), cb, pb, mb, MLA_ROPE // 4)
    scale_b = (MLA_NOPE + MLA_ROPE) ** -0.5
    for hh in range(B_HEADS):
        sl = slice(hh * LANES, (hh + 1) * LANES)
        qb_o[0, hh] = (_rope(qb[:, sl], cb, pb, mb, MLA_ROPE // 4) * scale_b).astype(BF16)
        kb_o[0, hh] = (kn[:, sl] + k_rope).astype(BF16)
    z_o[0] = proj(P_Z, P_XBC)
    xbc_o[0] = px[:, :CONV_DIM]
    dt_o[0] = jnp.where(lane < 2 * C_HEADS, dtkr, 0.0)


def _inproj(x, mod, lw, tabs, tm, ln0=None):
    s, l, d = x.shape
    nt = l // tm
    full = lambda shape: pl.BlockSpec(shape, lambda si, i: (0,) * len(shape))
    tab = pl.BlockSpec((tm, LANES), lambda si, i: (i, 0))
    head = lambda nh, dh: pl.BlockSpec((1, nh, tm, dh), lambda si, i: (si, 0, i, 0))
    row = lambda w: pl.BlockSpec((1, tm, w), lambda si, i: (si, i, 0))
    out_shape = (
        jax.ShapeDtypeStruct((s, A_HEADS, l, HEAD_DIM), BF16),
        jax.ShapeDtypeStruct((s, A_KV_HEADS, l, HEAD_DIM), BF16),
        jax.ShapeDtypeStruct((s, A_KV_HEADS, l, LANES), BF16),
        jax.ShapeDtypeStruct((s, B_HEADS, l, LANES), BF16),
        jax.ShapeDtypeStruct((s, B_HEADS, l, LANES), BF16),
        jax.ShapeDtypeStruct((s, B_HEADS, l, LANES), BF16),
        jax.ShapeDtypeStruct((s, l, C_INNER), F32),
        jax.ShapeDtypeStruct((s, l, CONV_DIM), F32),
        jax.ShapeDtypeStruct((s, l, LANES), F32),
    )
    out_specs = (head(A_HEADS, HEAD_DIM), head(A_KV_HEADS, HEAD_DIM), head(A_KV_HEADS, LANES),
                 head(B_HEADS, LANES), head(B_HEADS, LANES), head(B_HEADS, LANES),
                 row(C_INNER), row(CONV_DIM), row(LANES))
    if ln0 is None:
        ln0 = (jnp.ones((1, d), F32), jnp.zeros((1, d), F32))
    else:
        out_shape += (jax.ShapeDtypeStruct((s, l, d), F32),)
        out_specs += (row(d),)
    return pl.pallas_call(
        _inproj_kernel,
        out_shape=out_shape,
        grid=(s, nt),
        in_specs=[row(d),
                  pl.BlockSpec((1, 6, d), lambda si, i: (si, 0, 0)),
                  full((d, P_END)), full((Q_LORA, B_HEADS * LANES)),
                  full((KV_LORA, B_HEADS * LANES)), full((KV_LORA, B_HEADS * MLA_V)),
                  full((1, LANES)), full((1, LANES)), full((1, Q_LORA)), full((1, KV_LORA)),
                  tab, tab, tab, tab, tab, tab, full((1, d)), full((1, d))],
        out_specs=out_specs,
        compiler_params=_params(("parallel", "parallel"), 48),
        name="inproj",
    )(x, mod, lw["w_in"], lw["w_uq"], lw["w_ukk"], lw["w_ukv"],
      lw["gqa"], lw["gka"], lw["gqb"], lw["gkvb"], *tabs, *ln0)


def _attn_kernel(q_ref, k_ref, v_ref, o_ref, *, shared_kv):
    tq = q_ref.shape[2]
    qr = min(ATT_Q_ROWS, tq)
    items = [(j, r) for j in range(2) for r in range(tq // qr)]

    def scores(j, r):
        kj = 0 if shared_kv else j
        return _dot_nt(q_ref[0, j, r * qr:(r + 1) * qr, :], k_ref[0, kj])

    s_next = scores(*items[0])
    outs = {}
    for i, (j, r) in enumerate(items):
        s = s_next
        if i + 1 < len(items):
            s_next = scores(*items[i + 1])
        kj = 0 if shared_kv else j
        p = jnp.exp(s - jnp.max(s, axis=-1, keepdims=True)).astype(BF16)
        o = _dot(p, v_ref[0, kj])
        outs[(j, r)] = o[:, :ATT_DV] / o[:, ATT_DV:ATT_DV + 1]
    for r in range(tq // qr):
        o_ref[0, r * qr:(r + 1) * qr, :] = jnp.concatenate([outs[(0, r)], outs[(1, r)]], axis=-1)


def _attention(q, k, v, tq, name):
    s, hq, l, dk = q.shape
    hk = k.shape[1]
    shared = hq // hk == 2
    kvb = 1 if shared else 2
    return pl.pallas_call(
        functools.partial(_attn_kernel, shared_kv=shared),
        out_shape=jax.ShapeDtypeStruct((s, l, hq * ATT_DV), F32),
        grid=(s, hq // 2, l // tq),
        in_specs=[pl.BlockSpec((1, 2, tq, dk), lambda si, hp, qi: (si, hp, qi, 0)),
                  pl.BlockSpec((1, kvb, l, dk), lambda si, hp, qi: (si, hp, 0, 0)),
                  pl.BlockSpec((1, kvb, l, LANES), lambda si, hp, qi: (si, hp, 0, 0))],
        out_specs=pl.BlockSpec((1, tq, 2 * ATT_DV), lambda si, hp, qi: (si, qi, hp)),
        compiler_params=_params(("parallel", "parallel", "arbitrary"), 48),
        name=name,
    )(q, k, v)


def _ssd_kernel(xbc_ref, dt_ref, z_ref, cw_ref, cb_ref, dtb_ref, alog_ref, dskip_ref, g_ref,
                exf_ref, exb_ref, bd_ref, o_ref, xact, ad_s, hf, hb, pb_s, xsb_s, bt_s, dec_s):
    l = xbc_ref.shape[1]
    nc = l // SSD_Q
    q = SSD_Q
    pad = SUBLANES
    zrows = jnp.zeros((pad, LANES), F32)

    def conv_tile(win, r0, cols):
        acc = jnp.zeros((q, LANES), F32) + cb_ref[:, cols]
        for k in range(D_CONV):
            lo = pad - D_CONV // 2 + k
            acc = acc + win[lo:lo + q] * cw_ref[k:k + 1, cols]
        xact[pl.ds(r0, q), cols] = _silu(acc)

    def conv_chunk(c, carry):
        r0 = pl.multiple_of(c * q, q)
        for st in range(CONV_DIM // LANES):
            cols = slice(st * LANES, (st + 1) * LANES)
            conv_tile(xbc_ref[0, pl.ds(r0 - pad, q + 2 * pad), cols], r0, cols)
        return carry

    for st in range(CONV_DIM // LANES):
        cols = slice(st * LANES, (st + 1) * LANES)
        conv_tile(jnp.concatenate([zrows, xbc_ref[0, 0:q + pad, cols]], axis=0), 0, cols)
        conv_tile(jnp.concatenate([xbc_ref[0, l - q - pad:l, cols], zrows], axis=0), l - q, cols)
    lax.fori_loop(1, nc - 1, conv_chunk, 0)
    dtr = dt_ref[0] + dtb_ref[...]
    dt = jnp.maximum(dtr, 0.0) + jnp.log1p(jnp.exp(-jnp.abs(dtr)))
    lane_l = lax.broadcasted_iota(jnp.int32, dt.shape, 1)
    a = dt * (-jnp.exp(alog_ref[...]))
    ad_s[...] = jnp.where(lane_l < 2 * C_HEADS, a, pltpu.roll(dt, 2 * C_HEADS, 1))

    ri = lax.broadcasted_iota(jnp.int32, (q, q), 0)
    ci = lax.broadcasted_iota(jnp.int32, (q, q), 1)
    lower = ri >= ci
    upper = ci >= ri
    tril = jnp.where(lower, 1.0, 0.0).astype(BF16)
    lane_q = lax.broadcasted_iota(jnp.int32, (q, LANES), 1)
    first8 = lane_q < C_HEADS
    mid8 = (lane_q >= C_HEADS) & (lane_q < 2 * C_HEADS)
    lo64 = lane_q < C_HEAD_DIM
    bd = bd_ref[...] > 0.5

    def chunk_decay(r0):
        ad = ad_s[pl.ds(r0, q), :]
        hi, lo = _split_bf16(ad)
        inc = _dot(tril, hi) + _dot(tril, lo)
        exc = inc - ad
        tot = inc[q - 1:q, :]
        dts = pltpu.roll(ad, LANES - 2 * C_HEADS, 1)
        return ad, inc, exc, tot, dts

    def expand(v, ex_ref):
        return _dot(v.astype(BF16), ex_ref[...])

    def expand_row(v_row, ex_ref):
        hi, lo = _split_bf16(jnp.broadcast_to(v_row, (8, LANES)))
        return (_dot(hi, ex_ref[...]) + _dot(lo, ex_ref[...]))[0:1]

    hf[...] = jnp.zeros_like(hf)
    hb[...] = jnp.zeros_like(hb)

    def fwd_pre(c):
        r0 = pl.multiple_of(c * q, q)
        ad, inc, exc, tot, dts = chunk_decay(r0)
        xq = xact[pl.ds(r0, q), :]
        xs = xq[:, :C_INNER]
        bm = xq[:, C_INNER:C_INNER + LANES]
        cm = xq[:, C_INNER + LANES:]
        bt_bf = bm.T.astype(BF16)
        gmat = []
        for g in range(SSM_GROUPS):
            cg = jnp.where(lo64 if g == 0 else jnp.logical_not(lo64), cm, 0.0).astype(BF16)
            gmat.append(_dot(cg, bt_bf))
        pf = expand(jnp.where(first8, jnp.exp(inc), 0.0), exf_ref)
        sf = expand(jnp.where(first8, jnp.exp(tot - inc) * dts, 0.0), exf_ref)
        dec = expand_row(jnp.where(first8[0:1], jnp.exp(tot), 0.0), exf_ref)
        pbk = expand(jnp.where(mid8, jnp.exp(tot - exc), 0.0), exb_ref)
        sbk = expand(jnp.where(mid8, jnp.exp(exc) * dts, 0.0), exb_ref)
        pb_s[pl.ds(r0, q), :] = pbk
        xsb_s[pl.ds(r0, q), :] = (xs * sbk).astype(BF16)
        bt_s[pl.ds(r0, q), :] = bt_bf
        dec_b = expand_row(jnp.where(mid8[0:1], jnp.exp(tot), 0.0), exb_ref)
        dec_s[pl.ds(pl.multiple_of(c * 8, 8), 8), :] = jnp.broadcast_to(dec_b, (8, C_INNER))
        return dict(r0=r0, ad=ad, inc=inc, exc=exc, xs=xs, bt_bf=bt_bf, cm_bf=cm.astype(BF16),
                    gmat=gmat, pf=pf, dec=dec, xsd=(xs * sf).astype(BF16))

    def fwd_state(v):
        y_off = _dot(v["cm_bf"], hf[...].astype(BF16)) * v["pf"]
        st = _dot(v["bt_bf"], v["xsd"])
        hf[...] = v["dec"] * hf[...] + jnp.where(bd, st, 0.0)
        return y_off

    def fwd_heads(v, y_off):
        inc, exc, ad, r0 = v["inc"], v["exc"], v["ad"], v["r0"]
        inc_t = inc.T
        ad_t = ad.T
        exc_t = inc_t - ad_t
        xs_bf = v["xs"].astype(BF16)
        for pr in range(C_HEADS // 2):
            ys = []
            for hh in (2 * pr, 2 * pr + 1):
                icol = jnp.broadcast_to(inc[:, hh:hh + 1], (q, q))
                irow = jnp.broadcast_to(inc_t[hh:hh + 1, :], (q, q))
                lf = jnp.exp(jnp.where(lower, icol - irow, NEG_BIG)) * ad_t[16 + hh:17 + hh, :]
                ecol = jnp.broadcast_to(exc[:, 8 + hh:9 + hh], (q, q))
                erow = jnp.broadcast_to(exc_t[8 + hh:9 + hh, :], (q, q))
                lb = jnp.exp(jnp.where(upper, erow - ecol, NEG_BIG)) * ad_t[24 + hh:25 + hh, :]
                w = (v["gmat"][hh // (C_HEADS // SSM_GROUPS)] * (lf + lb)).astype(BF16)
                ys.append(_dot(w, xs_bf[:, pr * LANES:(pr + 1) * LANES]))
            cols = slice(pr * LANES, (pr + 1) * LANES)
            o_ref[0, pl.ds(r0, q), cols] = jnp.where(lo64, ys[0], ys[1]) + y_off[:, cols]

    def fwd_body(i, carry):
        va = fwd_pre(2 * i)
        vb = fwd_pre(2 * i + 1)
        ya = fwd_state(va)
        yb = fwd_state(vb)
        fwd_heads(va, ya)
        fwd_heads(vb, yb)
        return carry

    lax.fori_loop(0, nc // 2, fwd_body, 0)

    def bwd_body(i, carry):
        c = nc - 1 - i
        r0 = pl.multiple_of(c * q, q)
        xs = xact[pl.ds(r0, q), :C_INNER]
        cm_bf = xact[pl.ds(r0, q), C_INNER + LANES:].astype(BF16)
        y = o_ref[0, pl.ds(r0, q), :] + _dot(cm_bf, hb[...].astype(BF16)) * pb_s[pl.ds(r0, q), :]
        st = _dot(bt_s[pl.ds(r0, q), :], xsb_s[pl.ds(r0, q), :])
        dec = dec_s[pl.ds(pl.multiple_of(c * 8, 8), 1), :]
        hb[...] = dec * hb[...] + jnp.where(bd, st, 0.0)
        y = (y + dskip_ref[...] * xs) * _silu(z_ref[0, pl.ds(r0, q), :])
        o_ref[0, pl.ds(r0, q), :] = _rms_rows(y, g_ref[...])
        return carry

    lax.fori_loop(0, nc // 2, lambda i, cr: bwd_body(2 * i + 1, bwd_body(2 * i, cr)), 0)


def _ssd(xbc, dt, z, lw, consts):
    s, l, _ = xbc.shape
    assert l % (2 * SSD_Q) == 0
    seq = lambda w: pl.BlockSpec((1, l, w), lambda si: (si, 0, 0))
    full = lambda shape: pl.BlockSpec(shape, lambda si: (0,) * len(shape))
    return pl.pallas_call(
        _ssd_kernel,
        out_shape=jax.ShapeDtypeStruct((s, l, C_INNER), F32),
        grid=(s,),
        in_specs=[seq(CONV_DIM), seq(LANES), seq(C_INNER),
                  full((D_CONV, CONV_DIM)), full((1, CONV_DIM)), full((1, LANES)), full((1, LANES)),
                  full((1, C_INNER)), full((1, C_INNER)),
                  full((LANES, C_INNER)), full((LANES, C_INNER)), full((LANES, C_INNER))],
        out_specs=seq(C_INNER),
        scratch_shapes=[pltpu.VMEM((l, CONV_DIM), F32),
                        pltpu.VMEM((l, LANES), F32),
                        pltpu.VMEM((LANES, C_INNER), F32), pltpu.VMEM((LANES, C_INNER), F32),
                        pltpu.VMEM((l, C_INNER), F32), pltpu.VMEM((l, C_INNER), BF16),
                        pltpu.VMEM((l, SSD_Q), BF16), pltpu.VMEM((8 * (l // SSD_Q), C_INNER), F32)],
        compiler_params=_params(("parallel",), 58),
        name="ssd",
    )(xbc, dt, z, lw["conv_w"], lw["conv_b"], lw["dt_bias"], lw["a_log"], lw["d_skip"], lw["ssm_g"],
      consts["exf"], consts["exb"], consts["bd"])


def _route(logits):
    lane = lax.broadcasted_iota(jnp.int32, logits.shape, 1)
    lane_f = lane.astype(F32)
    neg_inf = -jnp.inf
    isg = (lane >= N_EXPERTS) & (lane < N_EXPERTS + N_EGROUPS)
    gl = jnp.where(isg, logits, neg_inf)
    gmax = jnp.max(gl, axis=-1, keepdims=True)
    gsel = jnp.min(jnp.where(gl == gmax, lane_f - N_EXPERTS, 1e9), axis=-1, keepdims=True)
    gw = 1.0 / jnp.sum(jnp.where(isg, jnp.exp(gl - gmax), 0.0), axis=-1, keepdims=True)
    grp = jnp.floor(lane_f * (1.0 / EXPERTS_PER_GROUP))
    el = jnp.where((lane < N_EXPERTS) & (grp == gsel), logits, neg_inf)
    l1 = jnp.max(el, axis=-1, keepdims=True)
    i1 = jnp.min(jnp.where(el == l1, lane_f, 1e9), axis=-1, keepdims=True)
    el2 = jnp.where(lane_f == i1, neg_inf, el)
    l2 = jnp.max(el2, axis=-1, keepdims=True)
    i2 = jnp.min(jnp.where(el2 == l2, lane_f, 1e9), axis=-1, keepdims=True)
    r = jnp.exp(l2 - l1)
    g1 = gw / (1.0 + r)
    g2 = g1 * r
    return jnp.where(lane == 0, i1, jnp.where(lane == 1, i2, jnp.where(lane == 2, g1,
                                                                    jnp.where(lane == 3, g2, 0.0))))


def _outproj_kernel(oa_ref, ob_ref, oc_ref, x_ref, mod_ref, wout_ref, goa_ref, gob_ref,
                    l1g_ref, l1b_ref, wrc_ref, br_ref, x1_o, h2v_o, route_o, ct):
    hm = x_ref.shape[1] // 2
    wa = A_HEADS * HEAD_DIM
    wb = wa + B_HEADS * MLA_V
    m = mod_ref[0]

    def mix(rows):
        oa = _rms_rows(oa_ref[0, rows, :], goa_ref[...]).astype(BF16)
        ob = _rms_rows(ob_ref[0, rows, :], gob_ref[...]).astype(BF16)
        oc = oc_ref[0, rows, :].astype(BF16)
        return _dot(oa, wout_ref[0:wa, :]) + _dot(ob, wout_ref[wa:wb, :]) + _dot(oc, wout_ref[wb:, :])

    halves = [pl.ds(0, hm), pl.ds(hm, hm)]
    ys = [mix(rows) for rows in halves]
    h2s = []
    for rows, y in zip(halves, ys):
        x1 = _ln_rows(DN_ALPHA * x_ref[0, rows, :] + (1.0 + m[2:3]) * y, l1g_ref[...], l1b_ref[...])
        x1_o[0, rows, :] = x1
        h2 = x1 * (1.0 + m[4:5]) + m[3:4]
        h2s.append(h2)
        hi, lo = _split_bf16(h2)
        hw = _dot(hi, wrc_ref[...])
        logits = hw[:, :LANES] + hw[:, LANES:] + _dot(lo, wrc_ref[:, :LANES]) + br_ref[...]
        route_o[0, rows, :] = _route(logits)
    _rows_to_tiles(jnp.concatenate(h2s, axis=0), ct, h2v_o)


def _outproj(oa, ob, oc, x, mod, lw, tm):
    s, l, d = x.shape
    nt = l // tm
    nch = d // LANES
    row = lambda w: pl.BlockSpec((1, tm, w), lambda si, i: (si, i, 0))
    full = lambda shape: pl.BlockSpec(shape, lambda si, i: (0,) * len(shape))
    return pl.pallas_call(
        _outproj_kernel,
        out_shape=(jax.ShapeDtypeStruct((s, l, d), F32), jax.ShapeDtypeStruct((s * l * nch, LANES), F32),
                   jax.ShapeDtypeStruct((s, l, LANES), F32)),
        grid=(s, l // tm),
        in_specs=[row(A_HEADS * HEAD_DIM), row(B_HEADS * MLA_V), row(C_INNER), row(d),
                  pl.BlockSpec((1, 6, d), lambda si, i: (si, 0, 0)),
                  full((d, d)), full((1, A_HEADS * HEAD_DIM)), full((1, B_HEADS * MLA_V)),
                  full((1, d)), full((1, d)), full((d, 2 * LANES)), full((1, LANES))],
        out_specs=(row(d), pl.BlockSpec((tm * nch, LANES), lambda si, i: (si * nt + i, 0)), row(LANES)),
        scratch_shapes=[pltpu.VMEM((nch * _pitch(tm), LANES), F32)],
        compiler_params=_params(("parallel", "parallel"), 48),
        name="outproj",
    )(oa, ob, oc, x, mod, lw["w_out"], lw["goa"], lw["gob"], lw["ln1_g"], lw["ln1_b"],
      lw["w_r"], lw["b_r"])


ROW_STRIDE = _pitch(MOE_ROWS)


def _slot_rows(l):
    return 2 * l + N_EXPERTS * 8 + MOE_ROWS


def _route_sort_kernel(route_ref, slots_o, meta_o):
    r = route_ref[0]
    l = r.shape[0]
    lane = lax.broadcasted_iota(jnp.int32, (l, LANES), 1)
    lane_f = lane.astype(F32)
    oh1 = lane_f == jnp.broadcast_to(r[:, 0:1], (l, LANES))
    oh2 = lane_f == jnp.broadcast_to(r[:, 1:2], (l, LANES))
    member = jnp.where(oh1 | oh2, 1.0, 0.0).astype(BF16)
    tb = min(256, l)
    ri = lax.broadcasted_iota(jnp.int32, (tb, tb), 0)
    ci = lax.broadcasted_iota(jnp.int32, (tb, tb), 1)
    stril = jnp.where(ri > ci, 1.0, 0.0).astype(BF16)
    run = jnp.zeros((1, LANES), F32)
    ranks = []
    for i in range(l // tb):
        blk = member[i * tb:(i + 1) * tb]
        ranks.append(_dot(stril, blk) + run)
        run = run + jnp.sum(blk.astype(F32), axis=0, keepdims=True)
    rank_all = jnp.concatenate(ranks, axis=0)
    padded = jnp.floor((run + 7.0) * 0.125) * 8.0
    hi, lo = _split_bf16(jnp.broadcast_to(padded, (8, LANES)))
    rl = lax.broadcasted_iota(jnp.int32, (LANES, LANES), 0)
    cl = lax.broadcasted_iota(jnp.int32, (LANES, LANES), 1)
    upper = jnp.where(rl < cl, 1.0, 0.0).astype(BF16)
    base = (_dot(hi, upper) + _dot(lo, upper))[0:1]
    pos = base + rank_all
    slot1 = jnp.sum(jnp.where(oh1, pos, 0.0), axis=-1, keepdims=True)
    slot2 = jnp.sum(jnp.where(oh2, pos, 0.0), axis=-1, keepdims=True)
    packed = jnp.where(lane == 0, slot1, jnp.where(lane == 1, slot2, 0.0))
    slots_o[0] = packed.T[0:2].astype(jnp.int32)
    meta_o[0] = jnp.concatenate([base, run, jnp.zeros((6, LANES), F32)], axis=0).astype(jnp.int32)


def _route_sort(route):
    s, l, _ = route.shape
    return pl.pallas_call(
        _route_sort_kernel,
        out_shape=(jax.ShapeDtypeStruct((s, 2, l), jnp.int32), jax.ShapeDtypeStruct((s, 8, LANES), jnp.int32)),
        grid=(s,),
        in_specs=[pl.BlockSpec((1, l, LANES), lambda si: (si, 0, 0))],
        out_specs=(pl.BlockSpec((1, 2, l), lambda si: (si, 0, 0)), pl.BlockSpec((1, 8, LANES), lambda si: (si, 0, 0))),
        compiler_params=_params(("parallel",), 32),
        name="route_sort",
    )(route)


def _moe_kernel(meta_ref, h2v_ref, slots_hbm, wg_ref, wu_ref, wd_ref,
                x1_ref, route_ref, mod_ref, l2g_ref, l2b_ref, o_ref,
                slot_s, tok_s, y_v, xt, yt, ct1, ct2, sem):
    s = pl.program_id(0)
    e = pl.program_id(1)
    l = h2v_ref.shape[0] // 8
    nch = D_MODEL // LANES
    n_slots = tok_s.shape[0]
    spare = 8 * 2 * l

    @pl.when(e == 0)
    def _():
        c_slots = pltpu.make_async_copy(slots_hbm.at[s], slot_s, sem.at[0])
        c_slots.start()
        used = meta_ref[s, 0, N_EXPERTS]

        def clear_tail(g, c):
            for j in range(8):
                tok_s[used + g * 8 + j] = spare
            return c

        lax.fori_loop(0, (n_slots - used) // 8, clear_tail, 0)

        def clear(ex, c):
            n = meta_ref[s, 1, ex]
            last = meta_ref[s, 0, ex] + ((n + 7) // 8) * 8 - 8

            @pl.when(n > 0)
            def _():
                for j in range(8):
                    tok_s[last + j] = spare

            return c

        lax.fori_loop(0, N_EXPERTS, clear, 0)
        c_slots.wait()

        def invert(g, c):
            t0 = g * 8
            for k in range(2):
                for j in range(8):
                    tok_s[slot_s[k * l + t0 + j]] = 16 * t0 + (16 * j + 8 * k)
            return c

        lax.fori_loop(0, l // 8, invert, 0)

    grp_per_blk = MOE_ROWS // 8

    def ffn(sub):
        xb = jnp.concatenate([xt[sub, j * ROW_STRIDE:j * ROW_STRIDE + MOE_ROWS, :] for j in range(nch)],
                             axis=-1).astype(BF16)
        mid = (_silu(_dot(xb, wg_ref[sub])) * _dot(xb, wu_ref[sub])).astype(BF16)
        y = _dot(mid, wd_ref[sub])
        for j in range(nch):
            yt[sub, j * ROW_STRIDE:j * ROW_STRIDE + MOE_ROWS, :] = y[:, j * LANES:(j + 1) * LANES]

    def gather_row(sub, slot, i):
        src = pl.multiple_of(lax.shift_right_logical(tok_s[slot], 1) & (8 * l - 8), 8)
        xt[sub, pl.ds(i, nch, stride=ROW_STRIDE), :] = h2v_ref[pl.ds(src, 8), :]

    def scatter_row(sub, slot, i):
        y_v[pl.ds(pl.multiple_of(tok_s[slot], 8), 8), :] = yt[sub, pl.ds(i, nch, stride=ROW_STRIDE), :]

    @pl.when(e < MOE_EXPERT_STEPS)
    def _():
        info = []
        for sub in range(MOE_EXPERTS_PER_STEP):
            ex = e * MOE_EXPERTS_PER_STEP + sub
            info.append((meta_ref[s, 0, ex], meta_ref[s, 1, ex]))
        for sub, (base, n) in enumerate(info):
            for i in range(MOE_ROWS):
                gather_row(sub, base + i, i)
        for sub in range(MOE_EXPERTS_PER_STEP):
            ffn(sub)
        for sub, (base, n) in enumerate(info):
            for i in range(MOE_ROWS):
                scatter_row(sub, base + i, i)
        for sub, (base, n) in enumerate(info):
            ngrp = (n + 7) // 8

            def block(blk, c, sub=sub, base=base, ngrp=ngrp):
                g_lo = blk * grp_per_blk
                ng = jnp.minimum(grp_per_blk, ngrp - g_lo)

                def gather(g, cc):
                    for j in range(8):
                        gather_row(sub, base + (g_lo + g) * 8 + j, g * 8 + j)
                    return cc

                lax.fori_loop(0, ng, gather, 0)
                ffn(sub)

                def scatter(g, cc):
                    for j in range(8):
                        scatter_row(sub, base + (g_lo + g) * 8 + j, g * 8 + j)
                    return cc

                lax.fori_loop(0, ng, scatter, 0)
                return c

            lax.fori_loop(1, (ngrp + grp_per_blk - 1) // grp_per_blk, block, 0)

    @pl.when(e >= MOE_EXPERT_STEPS)
    def _():
        tmo = x1_ref.shape[1]
        t0 = (e - MOE_EXPERT_STEPS) * tmo
        r = route_ref[0]
        y = (r[:, 2:3] * _tiles_to_rows(y_v, t0, tmo, ct1, 2, 0)
             + r[:, 3:4] * _tiles_to_rows(y_v, t0, tmo, ct2, 2, 1))
        m = mod_ref[0]
        o_ref[0] = _ln_rows(DN_ALPHA * x1_ref[0] + (1.0 + m[5:6]) * y, l2g_ref[...], l2b_ref[...])


def _moe(h2v, slots, meta, x1, route, mod, lw, experts, layer, tm):
    s, l, d = x1.shape
    assert l & (l - 1) == 0, "the gather address mask needs a power-of-two sequence length"
    n_slots = _slot_rows(l)
    eps = MOE_EXPERTS_PER_STEP
    wspec = lambda shape: pl.BlockSpec(
        shape, lambda si, e: (layer * MOE_EXPERT_STEPS + jnp.minimum(e, MOE_EXPERT_STEPS - 1), 0, 0))
    blk_idx = lambda si, e: (si, jnp.maximum(e - MOE_EXPERT_STEPS, 0), 0)
    tok_blk = pl.BlockSpec((1, tm, d), blk_idx)
    vec = pl.BlockSpec((1, d), lambda si, e: (0, 0))
    return pl.pallas_call(
        _moe_kernel,
        out_shape=jax.ShapeDtypeStruct((s, l, d), F32),
        grid=(s, MOE_EXPERT_STEPS + l // tm),
        in_specs=[pl.BlockSpec(memory_space=pltpu.SMEM),
                  pl.BlockSpec((l * 8, LANES), lambda si, e: (si, 0)),
                  pl.BlockSpec(memory_space=pl.ANY),
                  wspec((eps, d, D_FF_EXPERT)), wspec((eps, d, D_FF_EXPERT)), wspec((eps, D_FF_EXPERT, d)),
                  tok_blk, pl.BlockSpec((1, tm, LANES), blk_idx),
                  pl.BlockSpec((1, 6, d), lambda si, e: (si, 0, 0)), vec, vec],
        out_specs=tok_blk,
        scratch_shapes=[pltpu.SMEM((2 * l,), jnp.int32), pltpu.SMEM((n_slots,), jnp.int32),
                        pltpu.VMEM(((2 * l + 1) * 8, LANES), F32),
                        pltpu.VMEM((eps, 8 * ROW_STRIDE, LANES), F32), pltpu.VMEM((eps, 8 * ROW_STRIDE, LANES), F32),
                        pltpu.VMEM((8 * _pitch(tm), LANES), F32), pltpu.VMEM((8 * _pitch(tm), LANES), F32),
                        pltpu.SemaphoreType.DMA((1,))],
        compiler_params=_params(("parallel", "arbitrary"), 60),
        name="moe",
    )(meta, h2v, slots, *experts, x1, route, mod,
      lw["ln2_g"], lw["ln2_b"])


def _rope_tables(seq_len):
    t = jnp.arange(seq_len, dtype=jnp.int32)
    row = (t // GRID_W).astype(F32)[:, None]
    col = (t % GRID_W).astype(F32)[:, None]

    def axis_tables(d_axis):
        inv = ROPE_THETA ** (-jnp.arange(0, d_axis, 2, dtype=F32) / d_axis)
        zero = jnp.zeros((seq_len, d_axis // 2), F32)
        cs, ps, ms = [], [], []
        for pos in (row, col):
            ang = pos * inv
            c, sn = jnp.cos(ang), jnp.sin(ang)
            cs += [c, c]
            ps += [zero, sn]
            ms += [-sn, zero]
        return [jnp.concatenate(v, axis=-1) for v in (cs, ps, ms)]

    ta = [jnp.tile(v, (1, 2)) for v in axis_tables(HEAD_DIM // 2)]
    one = jnp.ones((seq_len, MLA_NOPE), F32)
    zero = jnp.zeros((seq_len, MLA_NOPE), F32)
    pad1 = jnp.ones((seq_len, LANES - MLA_NOPE - MLA_ROPE), F32)
    pad0 = jnp.zeros((seq_len, LANES - MLA_NOPE - MLA_ROPE), F32)
    cb, pb, mb = axis_tables(MLA_ROPE // 2)
    tb = [jnp.concatenate([one, cb, pad1], -1), jnp.concatenate([zero, pb, pad0], -1),
          jnp.concatenate([zero, mb, pad0], -1)]
    return ta + tb


def _ssd_consts():
    r = np.arange(LANES)[:, None]
    c = np.arange(C_INNER)[None, :]
    exf = ((r < C_HEADS) & (c // C_HEAD_DIM == r)).astype(np.float32)
    exb = ((r >= C_HEADS) & (r < 2 * C_HEADS) & (c // C_HEAD_DIM == r - C_HEADS)).astype(np.float32)
    heads_per_group = C_HEADS // SSM_GROUPS
    bd = (r // D_STATE == c // (C_HEAD_DIM * heads_per_group)).astype(np.float32)
    return {"exf": jnp.asarray(exf, BF16), "exb": jnp.asarray(exb, BF16), "bd": jnp.asarray(bd, F32)}


def _pack_weights(p):
    w_in = p["w_in"].astype(BF16)
    depth = w_in.shape[0]
    off = np.cumsum((0, 256, 128, 128, Q_LORA, KV_LORA, MLA_ROPE, C_INNER, CONV_DIM, 2 * C_HEADS))
    w_kr = w_in[:, :, off[5]:off[6]]
    zpad = lambda n: jnp.zeros((depth, D_MODEL, n), BF16)
    dtkr = jnp.concatenate([w_in[:, :, off[8]:], zpad(KR_LANE - 2 * C_HEADS), w_kr,
                            zpad(LANES - KR_LANE - MLA_ROPE)], -1)
    w_in_p = jnp.concatenate([w_in[:, :, :off[5]], w_in[:, :, off[6]:off[8]], dtkr], -1)
    dq = MLA_NOPE + MLA_ROPE
    w_uq = p["w_uq"].reshape(depth, Q_LORA, B_HEADS, dq)
    w_uq_p = jnp.concatenate([w_uq, jnp.zeros((depth, Q_LORA, B_HEADS, LANES - dq), F32)], -1)
    w_ukv = p["w_ukv"].reshape(depth, KV_LORA, B_HEADS, MLA_NOPE + MLA_V)
    w_ukk = jnp.concatenate([w_ukv[..., :MLA_NOPE],
                             jnp.zeros((depth, KV_LORA, B_HEADS, LANES - MLA_NOPE), F32)], -1)
    pad_lanes = lambda v: jnp.concatenate([v, jnp.zeros(v.shape[:-1] + (LANES - v.shape[-1],), F32)], -1)
    w_r = pad_lanes(jnp.concatenate([p["w_re"], p["w_rg"]], -1))
    w_r_hi = w_r.astype(BF16)
    return {
        "w_in": w_in_p,
        "w_uq": w_uq_p.reshape(depth, Q_LORA, B_HEADS * LANES).astype(BF16),
        "w_ukk": w_ukk.reshape(depth, KV_LORA, B_HEADS * LANES).astype(BF16),
        "w_ukv": w_ukv[..., MLA_NOPE:].reshape(depth, KV_LORA, B_HEADS * MLA_V).astype(BF16),
        "gqa": jnp.tile(p["qa_norm_g"], (1, 2))[:, None, :], "gka": jnp.tile(p["ka_norm_g"], (1, 2))[:, None, :],
        "gqb": p["qb_norm_g"][:, None, :], "gkvb": p["kvb_norm_g"][:, None, :],
        "goa": p["oa_norm_g"][:, None, :], "gob": p["ob_norm_g"][:, None, :],
        "conv_w": p["conv_w"], "conv_b": p["conv_b"][:, None, :],
        "dt_bias": pad_lanes(p["dt_bias"].reshape(depth, 2 * C_HEADS))[:, None, :],
        "a_log": pad_lanes(p["a_log"].reshape(depth, 2 * C_HEADS))[:, None, :],
        "d_skip": jnp.repeat(p["d_skip"], C_HEAD_DIM, axis=-1)[:, None, :],
        "ssm_g": p["ssm_norm_g"][:, None, :],
        "w_out": p["w_out"].astype(BF16),
        "ln1_g": p["ln1_g"][:, None, :], "ln1_b": p["ln1_b"][:, None, :],
        "w_r": jnp.concatenate([w_r_hi, (w_r - w_r_hi.astype(F32)).astype(BF16)], -1),
        "b_r": pad_lanes(jnp.concatenate([p["b_re"], p["b_rg"]], -1))[:, None, :],
        "ln2_g": p["ln2_g"][:, None, :], "ln2_b": p["ln2_b"][:, None, :],
        "experts": tuple(p[k].astype(BF16).reshape((depth * N_EXPERTS,) + p[k].shape[2:])
                         for k in ("w_e_gate", "w_e_up", "w_e_down")),
    }


def _trunk(x, c, p, pw=None, mods=None, tm=512, tq=2048):
    s, l, d = x.shape
    tm = min(tm, l)
    tq = min(tq, l)
    pw = _pack_weights(p) if pw is None else pw
    tabs = _rope_tables(l)
    consts = _ssd_consts()
    if mods is None:
        mods = _ada(c, p["w_ada"], p["b_ada"])
    mods = mods.reshape(DEPTH, s, 6, d)
    for layer in range(DEPTH):
        lw = {k: v[layer] for k, v in pw.items() if k != "experts"}
        mod = mods[layer]
        if layer == 0:
            *proj, x = _inproj(x, mod, lw, tabs, tm, ln0=(p["ln0_g"][None, :], p["ln0_b"][None, :]))
        else:
            proj = _inproj(x, mod, lw, tabs, tm)
        qa, ka, va, qb, kb, vb, z, xbc, dt = proj
        oa = _attention(qa, ka, va, tq, "attn_a")
        ob = _attention(qb, kb, vb, tq, "attn_b")
        oc = _ssd(xbc, dt, z, lw, consts)
        x1, h2v, route = _outproj(oa, ob, oc, x, mod, lw, tm)
        slots, meta = _route_sort(route)
        x = _moe(h2v, slots.reshape(s, 2 * l), meta, x1, route, mod, lw, pw["experts"], layer, tm)
    return x


def kernel(x_prompt, x_sample, c_prompt, c_sample, ln0_g, ln0_b, w_ada, b_ada, w_in, qa_norm_g, ka_norm_g, oa_norm_g, qb_norm_g, w_uq, kvb_norm_g, w_ukv, ob_norm_g, conv_w, conv_b, dt_bias, a_log, d_skip, ssm_norm_g, w_out, ln1_g, ln1_b, w_rg, b_rg, w_re, b_re, w_e_gate, w_e_up, w_e_down, ln2_g, ln2_b):
    p = dict(ln0_g=ln0_g, ln0_b=ln0_b, w_ada=w_ada, b_ada=b_ada, w_in=w_in,
             qa_norm_g=qa_norm_g, ka_norm_g=ka_norm_g, oa_norm_g=oa_norm_g,
             qb_norm_g=qb_norm_g, w_uq=w_uq, kvb_norm_g=kvb_norm_g, w_ukv=w_ukv, ob_norm_g=ob_norm_g,
             conv_w=conv_w, conv_b=conv_b, dt_bias=dt_bias, a_log=a_log, d_skip=d_skip,
             ssm_norm_g=ssm_norm_g, w_out=w_out, ln1_g=ln1_g, ln1_b=ln1_b,
             w_rg=w_rg, b_rg=b_rg, w_re=w_re, b_re=b_re,
             w_e_gate=w_e_gate, w_e_up=w_e_up, w_e_down=w_e_down, ln2_g=ln2_g, ln2_b=ln2_b)
    pw = _pack_weights(p)
    nb = x_prompt.shape[0]
    mods = _ada(jnp.concatenate([c_prompt, c_sample], 0), w_ada, b_ada)
    return (_trunk(x_prompt, c_prompt, p, pw, mods[:, :nb]), _trunk(x_sample, c_sample, p, pw, mods[:, nb:]))
```
